```python
import jax
import jax.numpy as jnp
from jax import lax
import numpy as np

D_MODEL = 1024
BATCH = 16
SEQ = 2048
DEPTH = 2

N_EVEN = (DEPTH + 1) // 2
N_ODD = DEPTH // 2

A_WIDTH = 512
A_GROUPS = 4
A_CHUNK = 128
B_WIDTH = 512
CONV_WIDTH = 31
N_HEADS = 8
HEAD_DIM = D_MODEL // N_HEADS
MOBA_BLOCK = 256
MOBA_TOPK = 3
MOBA_QCHUNK = 64
N_GROUPS = 4
EXPERTS_PER_GROUP = 4
N_EXPERTS = N_GROUPS * EXPERTS_PER_GROUP
EXPERT_TOPK = 2
D_EXPERT = 512
DN_ALPHA = (2.0 * DEPTH) ** 0.25
DN_BETA = (8.0 * DEPTH) ** -0.25
LN_EPS = 1e-5

kernel_name = 'hybrid_sgu_conv_moba_hmoe_block'


def layer_norm(x, g, b):
    xf = x.astype(jnp.float32)
    mu = xf.mean(-1, keepdims=True)
    var = jnp.square(xf - mu).mean(-1, keepdims=True)
    return ((xf - mu) * lax.rsqrt(var + LN_EPS) * g.astype(jnp.float32) + b.astype(jnp.float32)).astype(x.dtype)


def spatial_gating(z, ln_g, ln_b, w_s, b_s):
    bsz, t, _ = z.shape
    u, v = jnp.split(z, 2, axis=-1)
    v = layer_norm(v, ln_g, ln_b)
    n_chunks = t // A_CHUNK
    v = v.reshape(bsz, n_chunks, A_CHUNK, A_GROUPS, A_WIDTH // A_GROUPS)
    causal = jnp.tril(jnp.ones((A_CHUNK, A_CHUNK), dtype=bool))
    w = jnp.where(causal[None], w_s, 0)
    s = jnp.einsum('gpq,bcqgd->bcpgd', w, v) + b_s.T[:, :, None]
    return u * s.reshape(bsz, t, A_WIDTH)


def conv_module(z, w_dw, b_dw, ln_g, ln_b):
    a, gate = jnp.split(z, 2, axis=-1)
    a = a * jax.nn.sigmoid(gate)
    y = lax.conv_general_dilated(
        a, w_dw[:, None, :], window_strides=(1,), padding=[(CONV_WIDTH - 1, 0)],
        dimension_numbers=('NWC', 'WIO', 'NWC'), feature_group_count=B_WIDTH) + b_dw
    return jax.nn.silu(layer_norm(y, ln_g, ln_b))


def even_mixer(h, w_in, sgu_ln_g, sgu_ln_b, w_s, b_s, w_dw, b_dw, conv_ln_g, conv_ln_b, w_out):
    z = h @ w_in
    z_a, z_b = jnp.split(z, [2 * A_WIDTH], axis=-1)
    y_a = spatial_gating(jax.nn.gelu(z_a, approximate=False), sgu_ln_g, sgu_ln_b, w_s, b_s)
    y_b = conv_module(z_b, w_dw, b_dw, conv_ln_g, conv_ln_b)
    return jnp.concatenate([y_a, y_b], axis=-1) @ w_out


def moba_attention(q, k, v):
    bsz, t, h, dh = q.shape
    n_blocks = -(-t // MOBA_BLOCK)
    tp = n_blocks * MOBA_BLOCK
    pad = ((0, 0), (0, tp - t), (0, 0), (0, 0))
    q, k, v = jnp.pad(q, pad), jnp.pad(k, pad), jnp.pad(v, pad)
    kb = k.reshape(bsz, n_blocks, MOBA_BLOCK, h, dh).transpose(0, 3, 1, 2, 4)
    vb = v.reshape(bsz, n_blocks, MOBA_BLOCK, h, dh).transpose(0, 3, 1, 2, 4)
    own = jnp.arange(tp) // MOBA_BLOCK
    own_idx = jnp.broadcast_to(own[None, :, None, None], (bsz, tp, h, 1))
    own_valid = jnp.ones((bsz, tp, h, 1), dtype=bool)
    n_sel = min(MOBA_TOPK, n_blocks - 1)
    if n_sel > 0:
        k_mean = kb.astype(jnp.float32).mean(axis=3)
        gate = jnp.einsum('bthd,bhnd->bthn', q.astype(jnp.float32), k_mean)
        past = jnp.arange(n_blocks)[None, :] < own[:, None]
        gate = jnp.where(past[None, :, None, :], gate, -jnp.inf)
        _, top_idx = lax.top_k(gate, n_sel)
        sel_idx = jnp.concatenate([top_idx, own_idx], axis=-1)
        sel_valid = jnp.concatenate([top_idx < own[None, :, None, None], own_valid], axis=-1)
    else:
        sel_idx, sel_valid = own_idx, own_valid
    n_s = sel_idx.shape[-1]
    n_q = tp // MOBA_QCHUNK
    qs = q.reshape(bsz * n_q, MOBA_QCHUNK, h, dh)
    idxs = sel_idx.reshape(bsz * n_q, MOBA_QCHUNK, h, n_s)
    valids = sel_valid.reshape(bsz * n_q, MOBA_QCHUNK, h, n_s)
    b_ids = jnp.repeat(jnp.arange(bsz), n_q)
    j_ids = jnp.tile(jnp.arange(n_q), bsz)
    heads = jnp.arange(h)[None, :, None]
    offs = jnp.arange(MOBA_BLOCK)
    scale = dh ** -0.5

    def attend_chunk(args):
        b, j, qc, idx, valid = args
        k_sel = kb[b][heads, idx]
        v_sel = vb[b][heads, idx]
        s = jnp.einsum('qhd,qhspd->qhsp', qc.astype(jnp.float32), k_sel.astype(jnp.float32)) * scale
        qpos = j * MOBA_QCHUNK + jnp.arange(MOBA_QCHUNK)
        kpos = idx[..., None] * MOBA_BLOCK + offs
        mask = valid[..., None] & (kpos <= qpos[:, None, None, None])
        s = jnp.where(mask, s, -jnp.inf).reshape(MOBA_QCHUNK, h, n_s * MOBA_BLOCK)
        p = jax.nn.softmax(s, axis=-1)
        o = jnp.einsum('qhk,qhkd->qhd', p,
                       v_sel.reshape(MOBA_QCHUNK, h, n_s * MOBA_BLOCK, dh).astype(jnp.float32))
        return o.astype(qc.dtype)

    out = lax.map(attend_chunk, (b_ids, j_ids, qs, idxs, valids))
    return out.reshape(bsz, tp, h, dh)[:, :t]


def odd_mixer(h, w_qkv, w_out):
    bsz, t, _ = h.shape
    qkv = (h @ w_qkv).reshape(bsz, t, 3, N_HEADS, HEAD_DIM)
    o = moba_attention(qkv[:, :, 0], qkv[:, :, 1], qkv[:, :, 2])
    return o.reshape(bsz, t, N_HEADS * HEAD_DIM) @ w_out


def hier_moe(h, w_grp, b_grp, w_er, b_er, w_gate, w_up, w_down):
    x = h.reshape(-1, h.shape[-1])
    n = x.shape[0]
    xf = x.astype(jnp.float32)
    grp_logits = xf @ w_grp.astype(jnp.float32) + b_grp.astype(jnp.float32)
    grp_prob = jax.nn.softmax(grp_logits, axis=-1)
    g_w, g_sel = lax.top_k(grp_prob, 1)
    exp_logits = jnp.einsum('nd,gde->nge', xf, w_er.astype(jnp.float32)) + b_er.astype(jnp.float32)
    in_grp = jnp.take_along_axis(exp_logits, g_sel[:, :, None], axis=1)[:, 0]
    top_v, top_i = lax.top_k(in_grp, EXPERT_TOPK)
    top_w = jax.nn.softmax(top_v, axis=-1) * g_w
    expert_id = g_sel * EXPERTS_PER_GROUP + top_i
    combine = jnp.einsum('nk,nke->ne', top_w, jax.nn.one_hot(expert_id, N_EXPERTS, dtype=jnp.float32))
    out = jnp.zeros((n, x.shape[-1]), jnp.float32)
    for e in range(N_EXPERTS):
        hid = jax.nn.silu(x @ w_gate[e]) * (x @ w_up[e])
        out = out + combine[:, e:e + 1] * (hid @ w_down[e]).astype(jnp.float32)
    return out.astype(h.dtype).reshape(h.shape)


def setup_inputs(seed: int = 0) -> dict:
    key = jax.random.key(seed)
    ks = iter(jax.random.split(key, 32))
    d = D_MODEL

    def nrm(shape, s):
        return jax.random.normal(next(ks), shape, jnp.float32) * s

    return {
        'x': nrm((BATCH, SEQ, d), 1.0),
        'c': nrm((BATCH, d), 1.0),
        'ada_w': nrm((DEPTH, d, 6 * d), 0.1 * d ** -0.5),
        'ada_b': nrm((DEPTH, 6 * d), 0.01),
        'ln_mix_g': 1.0 + nrm((DEPTH, d), 0.02),
        'ln_mix_b': nrm((DEPTH, d), 0.02),
        'ln_ffn_g': 1.0 + nrm((DEPTH, d), 0.02),
        'ln_ffn_b': nrm((DEPTH, d), 0.02),
        'ev_w_in': nrm((N_EVEN, d, 2 * A_WIDTH + 2 * B_WIDTH), d ** -0.5),
        'ev_sgu_ln_g': 1.0 + nrm((N_EVEN, A_WIDTH), 0.02),
        'ev_sgu_ln_b': nrm((N_EVEN, A_WIDTH), 0.02),
        'ev_w_s': nrm((N_EVEN, A_GROUPS, A_CHUNK, A_CHUNK), A_CHUNK ** -0.5),
        'ev_b_s': 1.0 + nrm((N_EVEN, A_GROUPS, A_CHUNK), 0.02),
        'ev_w_dw': nrm((N_EVEN, CONV_WIDTH, B_WIDTH), CONV_WIDTH ** -0.5),
        'ev_b_dw': nrm((N_EVEN, B_WIDTH), 0.02),
        'ev_conv_ln_g': 1.0 + nrm((N_EVEN, B_WIDTH), 0.02),
        'ev_conv_ln_b': nrm((N_EVEN, B_WIDTH), 0.02),
        'ev_w_out': nrm((N_EVEN, A_WIDTH + B_WIDTH, d), DN_BETA * (A_WIDTH + B_WIDTH) ** -0.5),
        'od_w_qkv': nrm((N_ODD, d, 3 * N_HEADS * HEAD_DIM), d ** -0.5),
        'od_w_out': nrm((N_ODD, N_HEADS * HEAD_DIM, d), DN_BETA * (N_HEADS * HEAD_DIM) ** -0.5),
        'moe_w_grp': nrm((DEPTH, d, N_GROUPS), d ** -0.5),
        'moe_b_grp': nrm((DEPTH, N_GROUPS), 0.01),
        'moe_w_er': nrm((DEPTH, N_GROUPS, d, EXPERTS_PER_GROUP), d ** -0.5),
        'moe_b_er': nrm((DEPTH, N_GROUPS, EXPERTS_PER_GROUP), 0.01),
        'moe_w_gate': nrm((DEPTH, N_EXPERTS, d, D_EXPERT), d ** -0.5),
        'moe_w_up': nrm((DEPTH, N_EXPERTS, d, D_EXPERT), d ** -0.5),
        'moe_w_down': nrm((DEPTH, N_EXPERTS, D_EXPERT, d), DN_BETA * D_EXPERT ** -0.5),
    }


def reference(x, c, ada_w, ada_b, ln_mix_g, ln_mix_b, ln_ffn_g, ln_ffn_b,
              ev_w_in, ev_sgu_ln_g, ev_sgu_ln_b, ev_w_s, ev_b_s, ev_w_dw, ev_b_dw,
              ev_conv_ln_g, ev_conv_ln_b, ev_w_out, od_w_qkv, od_w_out,
              moe_w_grp, moe_b_grp, moe_w_er, moe_b_er, moe_w_gate, moe_w_up, moe_w_down):
    c_act = jax.nn.silu(c)
    for i in range(DEPTH):
        cond = c_act @ ada_w[i] + ada_b[i]
        sh_m, sc_m, g_m, sh_f, sc_f, g_f = jnp.split(cond[:, None, :], 6, axis=-1)
        h = x * (1.0 + sc_m) + sh_m
        j = i // 2
        if i % 2 == 0:
            y = even_mixer(h, ev_w_in[j], ev_sgu_ln_g[j], ev_sgu_ln_b[j], ev_w_s[j], ev_b_s[j],
                           ev_w_dw[j], ev_b_dw[j], ev_conv_ln_g[j], ev_conv_ln_b[j], ev_w_out[j])
        else:
            y = odd_mixer(h, od_w_qkv[j], od_w_out[j])
        x = layer_norm(DN_ALPHA * x + (1.0 + g_m) * y, ln_mix_g[i], ln_mix_b[i])
        h = x * (1.0 + sc_f) + sh_f
        y = hier_moe(h, moe_w_grp[i], moe_b_grp[i], moe_w_er[i], moe_b_er[i],
                     moe_w_gate[i], moe_w_up[i], moe_w_down[i])
        x = layer_norm(DN_ALPHA * x + (1.0 + g_f) * y, ln_ffn_g[i], ln_ffn_b[i])
    return x
```

```python
import functools

import jax
import jax.numpy as jnp
import numpy as np
from jax import lax
from jax.experimental import pallas as pl
from jax.experimental.pallas import tpu as pltpu

D_MODEL = 1024
A_WIDTH = 512
A_GROUPS = 4
A_CHUNK = 128
B_WIDTH = 512
CONV_WIDTH = 31
N_HEADS = 8
HEAD_DIM = D_MODEL // N_HEADS
MOBA_BLOCK = 256
MOBA_TOPK = 3
N_GROUPS = 4
EXPERTS_PER_GROUP = 4
N_EXPERTS = N_GROUPS * EXPERTS_PER_GROUP
D_EXPERT = 512
DEPTH = 2
DN_ALPHA = (2.0 * DEPTH) ** 0.25
LN_EPS = 1e-5

LANES = 128
SUBLANES = 8
ROW_TILE = 256
MOE_TILE = 256
CONV_HALO = 32
PAIRS_PER_GROUP = EXPERTS_PER_GROUP * (EXPERTS_PER_GROUP - 1) // 2
N_BUCKETS = N_GROUPS * PAIRS_PER_GROUP
BUCKET_ROWS = 32
META_W = LANES
HX_W = D_MODEL + META_W
VMEM_LIMIT = 56 * 1024 * 1024
NEG_BIG = -1e30
F32 = jnp.float32
BF16 = jnp.bfloat16

_PAIR_LO = np.array([0, 0, 0, 1, 1, 2], np.int32)
_PAIR_HI = np.array([1, 2, 3, 2, 3, 3], np.int32)
_BUCKET_LO = np.concatenate([g * EXPERTS_PER_GROUP + _PAIR_LO for g in range(N_GROUPS)])
_BUCKET_HI = np.concatenate([g * EXPERTS_PER_GROUP + _PAIR_HI for g in range(N_GROUPS)])


def _params(*sem):
    return pltpu.CompilerParams(dimension_semantics=sem, vmem_limit_bytes=VMEM_LIMIT)


def _layer_norm(x, g, b):
    mu = jnp.mean(x, axis=-1, keepdims=True)
    xc = x - mu
    var = jnp.mean(xc * xc, axis=-1, keepdims=True)
    return xc * lax.rsqrt(var + LN_EPS) * g + b


def _bdot(a, b):
    return jnp.dot(a.astype(BF16), b.astype(BF16), preferred_element_type=F32)


def _cond_kernel(c_ref, w_ref, b_ref, o_ref):
    c = c_ref[...]
    c_act = c * (1.0 / (1.0 + jnp.exp(-c)))
    o_ref[...] = jnp.dot(c_act, w_ref[...], precision=lax.Precision.HIGHEST,
                         preferred_element_type=F32) + b_ref[...]


def _cond_call(c, ada_w, ada_b):
    depth, d, d6 = ada_w.shape
    bsz = c.shape[0]
    n_col = d6 // d
    return pl.pallas_call(
        _cond_kernel,
        grid=(depth, n_col),
        in_specs=[pl.BlockSpec((bsz, d), lambda i, j: (0, 0)),
                  pl.BlockSpec((None, d, d), lambda i, j: (i, 0, j)),
                  pl.BlockSpec((None, 1, d), lambda i, j: (i, 0, j))],
        out_specs=pl.BlockSpec((None, bsz, d), lambda i, j: (i, 0, j)),
        out_shape=jax.ShapeDtypeStruct((depth, bsz, d6), F32),
        compiler_params=_params("arbitrary", "arbitrary"),
        name="cond",
    )(c, ada_w, ada_b.reshape(depth, 1, d6))


def _route(logits):
    lane = lax.broadcasted_iota(jnp.int32, logits.shape, 1)
    neg_inf = -jnp.inf
    far = LANES - 1
    gl = jnp.where(lane < N_GROUPS, logits, neg_inf)
    gmax = jnp.max(gl, axis=-1, keepdims=True)
    gsel = jnp.min(jnp.where(gl == gmax, lane, far), axis=-1, keepdims=True)
    gsum = jnp.sum(jnp.where(lane < N_GROUPS, jnp.exp(logits - gmax), 0.0), axis=-1, keepdims=True)
    g_w = 1.0 / gsum
    first = N_GROUPS + EXPERTS_PER_GROUP * gsel
    el = jnp.where((lane >= first) & (lane < first + EXPERTS_PER_GROUP), logits, neg_inf)
    v1 = jnp.max(el, axis=-1, keepdims=True)
    i1 = jnp.min(jnp.where(el == v1, lane, far), axis=-1, keepdims=True)
    el2 = jnp.where(lane == i1, neg_inf, el)
    v2 = jnp.max(el2, axis=-1, keepdims=True)
    i2 = jnp.min(jnp.where(el2 == v2, lane, far), axis=-1, keepdims=True)
    t = jnp.exp(v2 - v1)
    w1 = g_w / (1.0 + t)
    w2 = g_w * t / (1.0 + t)
    first_lower = i1 < i2
    a = jnp.minimum(i1, i2) - first
    b = jnp.maximum(i1, i2) - first
    pair = ((a * (2 * EXPERTS_PER_GROUP - 1 - a)) >> 1) + (b - a - 1)
    bucket = gsel * PAIRS_PER_GROUP + pair
    w_lo = jnp.where(first_lower, w1, w2)
    w_hi = jnp.where(first_lower, w2, w1)
    return bucket.astype(F32), w_lo, w_hi


def _sublayer_epilogue(x, y, cond_ref, lng_ref, lnb_ref, wr_ref, br_ref, x_out_ref, hx_ref, meta_t_ref):
    g_m = cond_ref[2:3, :]
    sh_f = cond_ref[3:4, :]
    sc_f = cond_ref[4:5, :]
    x1 = _layer_norm(DN_ALPHA * x + (1.0 + g_m) * y, lng_ref[...], lnb_ref[...])
    x_out_ref[...] = x1
    h2 = x1 * (1.0 + sc_f) + sh_f
    logits = jnp.dot(h2, wr_ref[...], precision=lax.Precision.HIGHEST,
                     preferred_element_type=F32) + br_ref[...]
    bucket, w_lo, w_hi = _route(logits)
    lane = lax.broadcasted_iota(jnp.int32, logits.shape, 1)
    meta = jnp.where(lane == 0, bucket, jnp.where(lane == 1, w_lo, jnp.where(lane == 2, w_hi, 0.0)))
    hx_ref[:, :D_MODEL] = h2
    hx_ref[:, D_MODEL:] = meta
    meta_t_ref[...] = jnp.transpose(meta)[:SUBLANES, :]


def _epilogue_out(n):
    shapes = (jax.ShapeDtypeStruct((n, D_MODEL), F32),
              jax.ShapeDtypeStruct((n, HX_W), F32),
              jax.ShapeDtypeStruct((SUBLANES, n), F32))
    return shapes


def _gelu(x):
    return 0.5 * x * (1.0 + lax.erf(x * (2.0 ** -0.5)))


def _even_kernel(x_ref, cond_ref, w_in_ref, sgu_g_ref, sgu_b_ref, w_s_ref, bs_ref, w_dw_ref, b_dw_ref,
                 cg_ref, cb_ref, w_out_ref, lng_ref, lnb_ref, wr_ref, br_ref,
                 x_out_ref, hx_ref, meta_t_ref, hist_ref):
    rows = x_ref.shape[0]
    t = pl.program_id(1)

    @pl.when(t == 0)
    def _():
        hist_ref[0:CONV_HALO, :] = jnp.zeros((CONV_HALO, B_WIDTH), F32)

    x = x_ref[...]
    h = x * (1.0 + cond_ref[1:2, :]) + cond_ref[0:1, :]
    z = _bdot(h, w_in_ref[...])

    u = _gelu(z[:, :A_WIDTH])
    v = _layer_norm(_gelu(z[:, A_WIDTH:2 * A_WIDTH]), sgu_g_ref[...], sgu_b_ref[...])
    r_i = lax.broadcasted_iota(jnp.int32, (A_CHUNK, A_CHUNK), 0)
    c_i = lax.broadcasted_iota(jnp.int32, (A_CHUNK, A_CHUNK), 1)
    gw = A_WIDTH // A_GROUPS
    s_rows = []
    for ck in range(rows // A_CHUNK):
        s_cols = []
        for g in range(A_GROUPS):
            w_g = jnp.where(c_i <= r_i, w_s_ref[g], 0.0)
            s_cols.append(_bdot(w_g, v[ck * A_CHUNK:(ck + 1) * A_CHUNK, g * gw:(g + 1) * gw]))
        s_rows.append(jnp.concatenate(s_cols, axis=1) + bs_ref[...])
    y_a = u * jnp.concatenate(s_rows, axis=0)

    zb = z[:, 2 * A_WIDTH:]
    gate = zb[:, B_WIDTH:]
    a = zb[:, :B_WIDTH] * (1.0 / (1.0 + jnp.exp(-gate)))
    hist_ref[CONV_HALO:CONV_HALO + rows, :] = a
    first = CONV_HALO - (CONV_WIDTH - 1)
    acc = jnp.zeros((rows, B_WIDTH), F32) + b_dw_ref[...]
    for k in range(CONV_WIDTH):
        acc = acc + w_dw_ref[k:k + 1, :] * hist_ref[first + k:first + k + rows, :]
    hist_ref[0:CONV_HALO, :] = hist_ref[rows:rows + CONV_HALO, :]
    yb = _layer_norm(acc, cg_ref[...], cb_ref[...])
    y_b = yb * (1.0 / (1.0 + jnp.exp(-yb)))

    y = _bdot(jnp.concatenate([y_a, y_b], axis=1), w_out_ref[...])
    _sublayer_epilogue(x, y, cond_ref, lng_ref, lnb_ref, wr_ref, br_ref, x_out_ref, hx_ref, meta_t_ref)


def _full(shape):
    nd = len(shape)
    return pl.BlockSpec(shape, lambda *_: (0,) * nd)


def _even_call(x2d, cond, bsz, seq, w_in, sgu_g, sgu_b, w_s, bs_full, w_dw, b_dw, cg, cb, w_out,
               lng, lnb, wr, br):
    n = bsz * seq
    n_t = seq // ROW_TILE
    row = lambda b, t: (b * n_t + t, 0)
    small = [sgu_g, sgu_b, w_s, bs_full, w_dw, b_dw, cg, cb, w_out, lng, lnb, wr, br]
    return pl.pallas_call(
        _even_kernel,
        grid=(bsz, n_t),
        in_specs=[pl.BlockSpec((ROW_TILE, D_MODEL), row),
                  pl.BlockSpec((None, 6, D_MODEL), lambda b, t: (b, 0, 0)),
                  _full(w_in.shape)] + [_full(a.shape) for a in small],
        out_specs=(pl.BlockSpec((ROW_TILE, D_MODEL), row),
                   pl.BlockSpec((ROW_TILE, HX_W), row),
                   pl.BlockSpec((SUBLANES, ROW_TILE), lambda b, t: (0, b * n_t + t))),
        out_shape=_epilogue_out(n),
        scratch_shapes=[pltpu.VMEM((CONV_HALO + ROW_TILE, B_WIDTH), F32)],
        compiler_params=_params("arbitrary", "arbitrary"),
        name="even_mixer",
    )(x2d, cond, w_in, *small)


def _rank_kernel(meta_t_ref, pos_ref, counts_ref, run_ref):
    phase = pl.program_id(0)
    i = pl.program_id(1)
    rows = meta_t_ref.shape[1]

    @pl.when((phase == 0) & (i == 0))
    def _():
        run_ref[...] = jnp.zeros_like(run_ref)

    bucket = meta_t_ref[0:1, :].astype(jnp.int32)
    sub = lax.broadcasted_iota(jnp.int32, (BUCKET_ROWS, rows), 0)
    onehot = jnp.where(sub == bucket, 1.0, 0.0)
    r_i = lax.broadcasted_iota(jnp.int32, (rows, rows), 0)
    c_i = lax.broadcasted_iota(jnp.int32, (rows, rows), 1)
    tri = jnp.where(r_i <= c_i, 1.0, 0.0)
    cum = _bdot(onehot, tri)
    tile_cnt = cum[:, rows - 1:rows]

    @pl.when((phase == 1) & (i == 0))
    def _():
        counts = run_ref[...]
        counts_ref[...] = counts
        padded = jnp.floor((counts + (MOE_TILE - 1)) * (1.0 / MOE_TILE)) * MOE_TILE
        sub_l = lax.broadcasted_iota(jnp.int32, counts.shape, 0)
        off = jnp.zeros_like(counts)
        for b in range(N_BUCKETS):
            off = off + jnp.where(sub_l > b, padded[b:b + 1, :], 0.0)
        run_ref[...] = off

    @pl.when(phase == 1)
    def _():
        base = run_ref[:, 0:1]
        pos = jnp.sum(onehot * (cum + base), axis=0, keepdims=True) - 1.0
        pos_ref[...] = pos.astype(jnp.int32)

    run_ref[...] = run_ref[...] + tile_cnt


def _rank_call(meta_t):
    n = meta_t.shape[1]
    n_t = n // ROW_TILE
    return pl.pallas_call(
        _rank_kernel,
        grid=(2, n_t),
        in_specs=[pl.BlockSpec((SUBLANES, ROW_TILE), lambda p, i: (0, i))],
        out_specs=(pl.BlockSpec((1, ROW_TILE), lambda p, i: (0, i * p)),
                   pl.BlockSpec((BUCKET_ROWS, LANES), lambda p, i: (0, 0))),
        out_shape=(jax.ShapeDtypeStruct((1, n), jnp.int32),
                   jax.ShapeDtypeStruct((BUCKET_ROWS, LANES), F32)),
        scratch_shapes=[pltpu.VMEM((BUCKET_ROWS, LANES), F32)],
        compiler_params=_params("arbitrary", "arbitrary"),
        name="moe_rank",
    )(meta_t)


DISPATCH_CHUNK = 256


def _dispatch_kernel(pos_ref, hx_ref, xs_in_ref, xs_ref, sem):
    del xs_in_ref
    base = pl.program_id(0) * DISPATCH_CHUNK

    def row_copy(r):
        n = base + r
        return pltpu.make_async_copy(hx_ref.at[pl.ds(n, 1), :], xs_ref.at[pl.ds(pos_ref[n], 1), :], sem)

    def start(r, c):
        row_copy(r).start()
        return c

    def wait(r, c):
        row_copy(r).wait()
        return c

    lax.fori_loop(0, DISPATCH_CHUNK, start, 0)
    lax.fori_loop(0, DISPATCH_CHUNK, wait, 0)


def _dispatch_call(pos, hx, n_sorted):
    n = hx.shape[0]
    xs0 = jnp.zeros((n_sorted, HX_W), F32)
    return pl.pallas_call(
        _dispatch_kernel,
        grid_spec=pltpu.PrefetchScalarGridSpec(
            num_scalar_prefetch=1,
            grid=(n // DISPATCH_CHUNK,),
            in_specs=[pl.BlockSpec(memory_space=pl.ANY), pl.BlockSpec(memory_space=pl.ANY)],
            out_specs=pl.BlockSpec(memory_space=pl.ANY),
            scratch_shapes=[pltpu.SemaphoreType.DMA(())]),
        out_shape=jax.ShapeDtypeStruct((n_sorted, HX_W), F32),
        input_output_aliases={2: 0},
        compiler_params=_params("arbitrary"),
        name="moe_dispatch",
    )(pos, hx, xs0)


def _moe_kernel(lo_ref, hi_ref, nact_ref, xs_ref, wgu_lo_ref, wd_lo_ref, wgu_hi_ref, wd_hi_ref, ys_ref):
    del lo_ref, hi_ref

    @pl.when(pl.program_id(0) < nact_ref[0])
    def _():
        x = xs_ref[:, :D_MODEL].astype(BF16)
        w_lo = xs_ref[:, D_MODEL + 1:D_MODEL + 2]
        w_hi = xs_ref[:, D_MODEL + 2:D_MODEL + 3]

        def expert(wgu_ref, wd_ref):
            gu = jnp.dot(x, wgu_ref[...], preferred_element_type=F32)
            gate = gu[:, :D_EXPERT]
            hid = gate * (1.0 / (1.0 + jnp.exp(-gate))) * gu[:, D_EXPERT:]
            return jnp.dot(hid.astype(BF16), wd_ref[...], preferred_element_type=F32)

        ys_ref[...] = w_lo * expert(wgu_lo_ref, wd_lo_ref) + w_hi * expert(wgu_hi_ref, wd_hi_ref)

    @pl.when(pl.program_id(0) >= nact_ref[0])
    def _():
        ys_ref[...] = jnp.zeros_like(ys_ref)


def _moe_call(tile_lo, tile_hi, nact, xs, wgu, wd):
    n_sorted = xs.shape[0]
    n_tiles = n_sorted // MOE_TILE

    def tile(t, lo, hi, na):
        return (jnp.minimum(t, na[0] - 1), 0)

    def w_lo(t, lo, hi, na):
        return (lo[jnp.minimum(t, na[0] - 1)], 0, 0)

    def w_hi(t, lo, hi, na):
        return (hi[jnp.minimum(t, na[0] - 1)], 0, 0)

    return pl.pallas_call(
        _moe_kernel,
        grid_spec=pltpu.PrefetchScalarGridSpec(
            num_scalar_prefetch=3,
            grid=(n_tiles,),
            in_specs=[pl.BlockSpec((MOE_TILE, HX_W), tile),
                      pl.BlockSpec((None, D_MODEL, 2 * D_EXPERT), w_lo),
                      pl.BlockSpec((None, D_EXPERT, D_MODEL), w_lo),
                      pl.BlockSpec((None, D_MODEL, 2 * D_EXPERT), w_hi),
                      pl.BlockSpec((None, D_EXPERT, D_MODEL), w_hi)],
            out_specs=pl.BlockSpec((MOE_TILE, D_MODEL), lambda t, lo, hi, na: (t, 0))),
        out_shape=jax.ShapeDtypeStruct((n_sorted, D_MODEL), F32),
        compiler_params=_params("arbitrary"),
        name="moe_experts",
    )(tile_lo, tile_hi, nact, xs, wgu, wd, wgu, wd)


def _combine_kernel(pos_ref, x_ref, cond_ref, lng_ref, lnb_ref, ys_ref, o_ref, ybuf, sems):
    i = pl.program_id(0)
    n_steps = pl.num_programs(0)
    rows = x_ref.shape[0]

    def row_copy(step, slot, r):
        src = ys_ref.at[pl.ds(pos_ref[step * rows + r], 1), :]
        return pltpu.make_async_copy(src, ybuf.at[slot, pl.ds(r, 1), :], sems.at[slot])

    def issue(step, slot):
        def body(r, c):
            row_copy(step, slot, r).start()
            return c
        lax.fori_loop(0, rows, body, 0)

    @pl.when(i == 0)
    def _():
        issue(0, 0)

    @pl.when(i + 1 < n_steps)
    def _():
        issue(i + 1, (i + 1) % 2)

    slot = i % 2

    def wait_body(r, c):
        row_copy(i, slot, r).wait()
        return c
    lax.fori_loop(0, rows, wait_body, 0)

    y = ybuf[slot]
    g_f = cond_ref[5:6, :]
    o_ref[...] = _layer_norm(DN_ALPHA * x_ref[...] + (1.0 + g_f) * y, lng_ref[...], lnb_ref[...])


def _combine_call(pos, x1, cond, bsz, seq, lng, lnb, ys):
    n = x1.shape[0]
    n_t = seq // ROW_TILE
    return pl.pallas_call(
        _combine_kernel,
        grid_spec=pltpu.PrefetchScalarGridSpec(
            num_scalar_prefetch=1,
            grid=(n // ROW_TILE,),
            in_specs=[pl.BlockSpec((ROW_TILE, D_MODEL), lambda i, p: (i, 0)),
                      pl.BlockSpec((None, 6, D_MODEL), lambda i, p: (i // n_t, 0, 0)),
                      pl.BlockSpec((1, D_MODEL), lambda i, p: (0, 0)),
                      pl.BlockSpec((1, D_MODEL), lambda i, p: (0, 0)),
                      pl.BlockSpec(memory_space=pl.ANY)],
            out_specs=pl.BlockSpec((ROW_TILE, D_MODEL), lambda i, p: (i, 0)),
            scratch_shapes=[pltpu.VMEM((2, ROW_TILE, D_MODEL), F32),
                            pltpu.SemaphoreType.DMA((2,))]),
        out_shape=jax.ShapeDtypeStruct((n, D_MODEL), F32),
        compiler_params=_params("arbitrary"),
        name="moe_combine",
    )(pos, x1, cond, lng, lnb, ys)


def _moe_sublayer(x1, hx, meta_t, cond, bsz, seq, lng, lnb, w_gate, w_up, w_down):
    n = x1.shape[0]
    n_sorted = n + N_BUCKETS * MOE_TILE
    pos2d, counts = _rank_call(meta_t)
    pos = pos2d.reshape(n)
    cnt = counts[:N_BUCKETS, 0].astype(jnp.int32)
    padded = ((cnt + MOE_TILE - 1) // MOE_TILE) * MOE_TILE
    ends = jnp.cumsum(padded)
    starts = jnp.arange(n_sorted // MOE_TILE, dtype=jnp.int32) * MOE_TILE
    tile_bucket = jnp.minimum(jnp.sum((ends[None, :] <= starts[:, None]).astype(jnp.int32), axis=1),
                              N_BUCKETS - 1)
    tile_lo = jnp.asarray(_BUCKET_LO)[tile_bucket]
    tile_hi = jnp.asarray(_BUCKET_HI)[tile_bucket]
    nact = (ends[-1:] // MOE_TILE).astype(jnp.int32)
    xs = _dispatch_call(pos, hx, n_sorted)
    wgu = jnp.concatenate([w_gate, w_up], axis=-1).astype(BF16)
    ys = _moe_call(tile_lo, tile_hi, nact, xs, wgu, w_down.astype(BF16))
    return _combine_call(pos, x1, cond, bsz, seq, lng, lnb, ys)


def _qkv_kernel(x_ref, cond_ref, wq_t_ref, wk_ref, wv_t_ref, q_t_ref, k_ref, v_t_ref, kmean_ref):
    h = (x_ref[...] * (1.0 + cond_ref[1:2, :]) + cond_ref[0:1, :]).astype(BF16)
    nt = (((1,), (1,)), ((), ()))
    q_t_ref[...] = lax.dot_general(wq_t_ref[...], h, nt, preferred_element_type=F32).astype(BF16)
    v_t_ref[...] = lax.dot_general(wv_t_ref[...], h, nt, preferred_element_type=F32).astype(BF16)
    k = jnp.dot(h, wk_ref[...], preferred_element_type=F32)
    k_ref[...] = k.astype(BF16)
    kmean_ref[...] = jnp.broadcast_to(jnp.mean(k, axis=0, keepdims=True), kmean_ref.shape)


def _qkv_call(x2d, cond, bsz, seq, wq_t, wk, wv_t):
    n_t = seq // MOBA_BLOCK
    hd = N_HEADS * HEAD_DIM
    return pl.pallas_call(
        _qkv_kernel,
        grid=(bsz, n_t),
        in_specs=[pl.BlockSpec((MOBA_BLOCK, D_MODEL), lambda b, t: (b * n_t + t, 0)),
                  pl.BlockSpec((None, 6, D_MODEL), lambda b, t: (b, 0, 0)),
                  _full(wq_t.shape), _full(wk.shape), _full(wv_t.shape)],
        out_specs=(pl.BlockSpec((None, hd, MOBA_BLOCK), lambda b, t: (b, 0, t)),
                   pl.BlockSpec((None, MOBA_BLOCK, hd), lambda b, t: (b, t, 0)),
                   pl.BlockSpec((None, hd, MOBA_BLOCK), lambda b, t: (b, 0, t)),
                   pl.BlockSpec((None, None, SUBLANES, hd), lambda b, t: (b, t, 0, 0))),
        out_shape=(jax.ShapeDtypeStruct((bsz, hd, seq), BF16),
                   jax.ShapeDtypeStruct((bsz, seq, hd), BF16),
                   jax.ShapeDtypeStruct((bsz, hd, seq), BF16),
                   jax.ShapeDtypeStruct((bsz, n_t, SUBLANES, hd), F32)),
        compiler_params=_params("arbitrary", "arbitrary"),
        name="qkv_proj",
    )(x2d, cond, wq_t, wk, wv_t)


def _moba_kernel(q_t_ref, k_ref, v_t_ref, kmean_ref, o_ref, sel_ref):
    i = pl.program_id(2)
    n_blocks = kmean_ref.shape[0]
    blk = MOBA_BLOCK
    q_t = q_t_ref[...]
    scale = HEAD_DIM ** -0.5

    km = kmean_ref[...]
    km_hi = km.astype(BF16)
    km_lo = (km - km_hi.astype(F32)).astype(BF16)
    gate = (jnp.dot(km_hi, q_t, preferred_element_type=F32)
            + jnp.dot(km_lo, q_t, preferred_element_type=F32))
    bidx = lax.broadcasted_iota(jnp.int32, gate.shape, 0)
    past = bidx < i
    gate = jnp.where(past, gate, -jnp.inf)
    rank = jnp.zeros(gate.shape, F32)
    for m in range(n_blocks):
        g_m = gate[m:m + 1, :]
        beats = (g_m > gate) | ((g_m == gate) & (m < bidx))
        rank = rank + jnp.where(beats, 1.0, 0.0)
    sel_ref[...] = jnp.where(past & (rank < MOBA_TOPK), 1.0, 0.0)

    def attend(j, allowed, carry):
        m_prev, l_prev, acc = carry
        start = pl.multiple_of(j * blk, blk)
        k_j = k_ref[pl.ds(start, blk), :]
        s = jnp.dot(k_j, q_t, preferred_element_type=F32) * scale
        s = jnp.where(allowed, s, NEG_BIG)
        m_new = jnp.maximum(m_prev, jnp.max(s, axis=0, keepdims=True))
        p = jnp.exp(s - m_new)
        corr = jnp.exp(m_prev - m_new)
        l_new = l_prev * corr + jnp.sum(p, axis=0, keepdims=True)
        v_j = v_t_ref[:, pl.ds(start, blk)]
        acc = acc * corr + jnp.dot(v_j, p.astype(BF16), preferred_element_type=F32)
        return m_new, l_new, acc

    key_i = lax.broadcasted_iota(jnp.int32, (blk, blk), 0)
    qry_i = lax.broadcasted_iota(jnp.int32, (blk, blk), 1)
    init = (jnp.full((1, blk), NEG_BIG, F32), jnp.zeros((1, blk), F32), jnp.zeros((HEAD_DIM, blk), F32))
    carry = attend(i, key_i <= qry_i, init)

    def past_block(j, carry):
        return attend(j, sel_ref[pl.ds(j, 1), :] > 0.5, carry)

    _, l_fin, acc = lax.fori_loop(0, i, past_block, carry)
    o_ref[...] = jnp.transpose(acc * (1.0 / l_fin)).astype(BF16)


def _moba_call(q_t, k, v_t, kmean):
    bsz, hd, seq = q_t.shape
    n_b = seq // MOBA_BLOCK
    return pl.pallas_call(
        _moba_kernel,
        grid=(bsz, N_HEADS, n_b),
        in_specs=[pl.BlockSpec((None, HEAD_DIM, MOBA_BLOCK), lambda b, h, i: (b, h, i)),
                  pl.BlockSpec((None, seq, HEAD_DIM), lambda b, h, i: (b, 0, h)),
                  pl.BlockSpec((None, HEAD_DIM, seq), lambda b, h, i: (b, h, 0)),
                  pl.BlockSpec((None, n_b, HEAD_DIM), lambda b, h, i: (b, 0, h))],
        out_specs=pl.BlockSpec((None, MOBA_BLOCK, HEAD_DIM), lambda b, h, i: (b, i, h)),
        out_shape=jax.ShapeDtypeStruct((bsz, seq, hd), BF16),
        scratch_shapes=[pltpu.VMEM((n_b, MOBA_BLOCK), F32)],
        compiler_params=_params("arbitrary", "arbitrary", "arbitrary"),
        name="moba_attention",
    )(q_t, k, v_t, kmean)


def _attn_out_kernel(x_ref, o_ref, cond_ref, w_out_ref, lng_ref, lnb_ref, wr_ref, br_ref,
                     x_out_ref, hx_ref, meta_t_ref):
    y = jnp.dot(o_ref[...], w_out_ref[...], preferred_element_type=F32)
    _sublayer_epilogue(x_ref[...], y, cond_ref, lng_ref, lnb_ref, wr_ref, br_ref, x_out_ref, hx_ref, meta_t_ref)


def _attn_out_call(x2d, o2d, cond, seq, w_out, lng, lnb, wr, br):
    n = x2d.shape[0]
    n_t = seq // ROW_TILE
    row = lambda i: (i, 0)
    small = [w_out, lng, lnb, wr, br]
    return pl.pallas_call(
        _attn_out_kernel,
        grid=(n // ROW_TILE,),
        in_specs=[pl.BlockSpec((ROW_TILE, D_MODEL), row),
                  pl.BlockSpec((ROW_TILE, D_MODEL), row),
                  pl.BlockSpec((None, 6, D_MODEL), lambda i: (i // n_t, 0, 0))]
                 + [_full(a.shape) for a in small],
        out_specs=(pl.BlockSpec((ROW_TILE, D_MODEL), row),
                   pl.BlockSpec((ROW_TILE, HX_W), row),
                   pl.BlockSpec((SUBLANES, ROW_TILE), lambda i: (0, i))),
        out_shape=_epilogue_out(n),
        compiler_params=_params("arbitrary"),
        name="attn_out",
    )(x2d, o2d, cond, *small)


def _router_weights(w_grp, b_grp, w_er, b_er):
    d = w_grp.shape[0]
    w_exp = jnp.transpose(w_er, (1, 0, 2)).reshape(d, N_EXPERTS)
    pad = LANES - N_GROUPS - N_EXPERTS
    wr = jnp.concatenate([w_grp, w_exp, jnp.zeros((d, pad), F32)], axis=1)
    br = jnp.concatenate([b_grp, b_er.reshape(N_EXPERTS), jnp.zeros((pad,), F32)]).reshape(1, LANES)
    return wr, br


def kernel(x, c, ada_w, ada_b, ln_mix_g, ln_mix_b, ln_ffn_g, ln_ffn_b, ev_w_in, ev_sgu_ln_g, ev_sgu_ln_b,
           ev_w_s, ev_b_s, ev_w_dw, ev_b_dw, ev_conv_ln_g, ev_conv_ln_b, ev_w_out, od_w_qkv, od_w_out,
           moe_w_grp, moe_b_grp, moe_w_er, moe_b_er, moe_w_gate, moe_w_up, moe_w_down):
    bsz, seq, d = x.shape
    n = bsz * seq
    depth = ada_w.shape[0]
    cond_all = _cond_call(c, ada_w, ada_b).reshape(depth, bsz, 6, d)
    row = lambda v: v.reshape(1, -1)
    xc = x.reshape(n, d)
    for i in range(depth):
        cond = cond_all[i]
        j = i // 2
        wr, br = _router_weights(moe_w_grp[i], moe_b_grp[i], moe_w_er[i], moe_b_er[i])
        lng, lnb = row(ln_mix_g[i]), row(ln_mix_b[i])
        if i % 2 == 0:
            gw = A_WIDTH // A_GROUPS
            bs_full = jnp.repeat(ev_b_s[j].T, gw, axis=1)
            w_dw = jnp.concatenate([ev_w_dw[j], jnp.zeros((1, B_WIDTH), F32)], axis=0)
            x1, hx, meta_t = _even_call(
                xc, cond, bsz, seq, ev_w_in[j].astype(BF16), row(ev_sgu_ln_g[j]), row(ev_sgu_ln_b[j]),
                ev_w_s[j], bs_full, w_dw, row(ev_b_dw[j]), row(ev_conv_ln_g[j]), row(ev_conv_ln_b[j]),
                ev_w_out[j].astype(BF16), lng, lnb, wr, br)
        else:
            hd = N_HEADS * HEAD_DIM
            w_qkv = od_w_qkv[j].astype(BF16)
            q_t, k, v_t, kmean8 = _qkv_call(xc, cond, bsz, seq, w_qkv[:, :hd].T, w_qkv[:, hd:2 * hd],
                                            w_qkv[:, 2 * hd:].T)
            o = _moba_call(q_t, k, v_t, kmean8[:, :, 0, :])
            x1, hx, meta_t = _attn_out_call(xc, o.reshape(n, hd), cond, seq, od_w_out[j].astype(BF16),
                                            lng, lnb, wr, br)
        xc = _moe_sublayer(x1, hx, meta_t, cond, bsz, seq, row(ln_ffn_g[i]), row(ln_ffn_b[i]),
                           moe_w_gate[i], moe_w_up[i], moe_w_down[i])
    return xc.reshape(bsz, seq, d)
```

```python
import functools

import jax
import jax.numpy as jnp
import numpy as np
from jax import lax
from jax.experimental import pallas as pl
from jax.experimental.pallas import tpu as pltpu

D_MODEL = 1024
A_WIDTH = 512
A_GROUPS = 4
A_CHUNK = 128
B_WIDTH = 512
CONV_WIDTH = 31
N_HEADS = 8
HEAD_DIM = D_MODEL // N_HEADS
MOBA_BLOCK = 256
MOBA_TOPK = 3
N_GROUPS = 4
EXPERTS_PER_GROUP = 4
N_EXPERTS = N_GROUPS * EXPERTS_PER_GROUP
D_EXPERT = 512
DEPTH = 2
DN_ALPHA = (2.0 * DEPTH) ** 0.25
LN_EPS = 1e-5

LANES = 128
SUBLANES = 8
ROW_TILE = 256
MOE_TILE = 256
CONV_HALO = 32
PAIRS_PER_GROUP = EXPERTS_PER_GROUP * (EXPERTS_PER_GROUP - 1) // 2
N_BUCKETS = N_GROUPS * PAIRS_PER_GROUP
BUCKET_ROWS = 32
META_W = LANES
HX_W = D_MODEL + META_W
VMEM_LIMIT = 56 * 1024 * 1024
NEG_BIG = -1e30
F32 = jnp.float32
BF16 = jnp.bfloat16

_PAIR_LO = np.array([0, 0, 0, 1, 1, 2], np.int32)
_PAIR_HI = np.array([1, 2, 3, 2, 3, 3], np.int32)
_BUCKET_LO = np.concatenate([g * EXPERTS_PER_GROUP + _PAIR_LO for g in range(N_GROUPS)])
_BUCKET_HI = np.concatenate([g * EXPERTS_PER_GROUP + _PAIR_HI for g in range(N_GROUPS)])


def _params(*sem):
    return pltpu.CompilerParams(dimension_semantics=sem, vmem_limit_bytes=VMEM_LIMIT)


def _layer_norm(x, g, b):
    mu = jnp.mean(x, axis=-1, keepdims=True)
    xc = x - mu
    var = jnp.mean(xc * xc, axis=-1, keepdims=True)
    return xc * lax.rsqrt(var + LN_EPS) * g + b


def _bdot(a, b):
    return jnp.dot(a.astype(BF16), b.astype(BF16), preferred_element_type=F32)


def _cond_kernel(c_ref, w_ref, b_ref, o_ref):
    c = c_ref[...]
    c_act = c * (1.0 / (1.0 + jnp.exp(-c)))
    o_ref[...] = jnp.dot(c_act, w_ref[...], precision=lax.Precision.HIGHEST,
                         preferred_element_type=F32) + b_ref[...]


def _cond_call(c, ada_w, ada_b):
    depth, d, d6 = ada_w.shape
    bsz = c.shape[0]
    n_col = d6 // d
    return pl.pallas_call(
        _cond_kernel,
        grid=(depth, n_col),
        in_specs=[pl.BlockSpec((bsz, d), lambda i, j: (0, 0)),
                  pl.BlockSpec((None, d, d), lambda i, j: (i, 0, j)),
                  pl.BlockSpec((None, 1, d), lambda i, j: (i, 0, j))],
        out_specs=pl.BlockSpec((None, bsz, d), lambda i, j: (i, 0, j)),
        out_shape=jax.ShapeDtypeStruct((depth, bsz, d6), F32),
        compiler_params=_params("arbitrary", "arbitrary"),
        name="cond",
    )(c, ada_w, ada_b.reshape(depth, 1, d6))


def _route(logits):
    lane = lax.broadcasted_iota(jnp.int32, logits.shape, 1)
    neg_inf = -jnp.inf
    far = LANES - 1
    gl = jnp.where(lane < N_GROUPS, logits, neg_inf)
    gmax = jnp.max(gl, axis=-1, keepdims=True)
    gsel = jnp.min(jnp.where(gl == gmax, lane, far), axis=-1, keepdims=True)
    gsum = jnp.sum(jnp.where(lane < N_GROUPS, jnp.exp(logits - gmax), 0.0), axis=-1, keepdims=True)
    g_w = 1.0 / gsum
    first = N_GROUPS + EXPERTS_PER_GROUP * gsel
    el = jnp.where((lane >= first) & (lane < first + EXPERTS_PER_GROUP), logits, neg_inf)
    v1 = jnp.max(el, axis=-1, keepdims=True)
    i1 = jnp.min(jnp.where(el == v1, lane, far), axis=-1, keepdims=True)
    el2 = jnp.where(lane == i1, neg_inf, el)
    v2 = jnp.max(el2, axis=-1, keepdims=True)
    i2 = jnp.min(jnp.where(el2 == v2, lane, far), axis=-1, keepdims=True)
    t = jnp.exp(v2 - v1)
    w1 = g_w / (1.0 + t)
    w2 = g_w * t / (1.0 + t)
    first_lower = i1 < i2
    a = jnp.minimum(i1, i2) - first
    b = jnp.maximum(i1, i2) - first
    pair = ((a * (2 * EXPERTS_PER_GROUP - 1 - a)) >> 1) + (b - a - 1)
    bucket = gsel * PAIRS_PER_GROUP + pair
    w_lo = jnp.where(first_lower, w1, w2)
    w_hi = jnp.where(first_lower, w2, w1)
    return bucket.astype(F32), w_lo, w_hi


def _sublayer_epilogue(x, y, cond_ref, lng_ref, lnb_ref, wr_ref, br_ref, x_out_ref, hx_ref, meta_t_ref):
    g_m = cond_ref[2:3, :]
    sh_f = cond_ref[3:4, :]
    sc_f = cond_ref[4:5, :]
    x1 = _layer_norm(DN_ALPHA * x + (1.0 + g_m) * y, lng_ref[...], lnb_ref[...])
    x_out_ref[...] = x1
    h2 = x1 * (1.0 + sc_f) + sh_f
    logits = jnp.dot(h2, wr_ref[...], precision=lax.Precision.HIGHEST,
                     preferred_element_type=F32) + br_ref[...]
    bucket, w_lo, w_hi = _route(logits)
    lane = lax.broadcasted_iota(jnp.int32, logits.shape, 1)
    meta = jnp.where(lane == 0, bucket, jnp.where(lane == 1, w_lo, jnp.where(lane == 2, w_hi, 0.0)))
    hx_ref[:, :D_MODEL] = h2
    hx_ref[:, D_MODEL:] = meta
    meta_t_ref[...] = jnp.transpose(meta)[:SUBLANES, :]


def _epilogue_out(n):
    shapes = (jax.ShapeDtypeStruct((n, D_MODEL), F32),
              jax.ShapeDtypeStruct((n, HX_W), F32),
              jax.ShapeDtypeStruct((SUBLANES, n), F32))
    return shapes


def _gelu(x):
    return 0.5 * x * (1.0 + lax.erf(x * (2.0 ** -0.5)))


def _even_kernel(x_ref, cond_ref, w_in_ref, sgu_g_ref, sgu_b_ref, w_s_ref, bs_ref, w_dw_ref, b_dw_ref,
                 cg_ref, cb_ref, w_out_ref, lng_ref, lnb_ref, wr_ref, br_ref,
                 x_out_ref, hx_ref, meta_t_ref, hist_ref):
    rows = x_ref.shape[0]
    t = pl.program_id(1)

    @pl.when(t == 0)
    def _():
        hist_ref[0:CONV_HALO, :] = jnp.zeros((CONV_HALO, B_WIDTH), F32)

    x = x_ref[...]
    h = x * (1.0 + cond_ref[1:2, :]) + cond_ref[0:1, :]
    z = _bdot(h, w_in_ref[...])

    u = _gelu(z[:, :A_WIDTH])
    v = _layer_norm(_gelu(z[:, A_WIDTH:2 * A_WIDTH]), sgu_g_ref[...], sgu_b_ref[...])
    r_i = lax.broadcasted_iota(jnp.int32, (A_CHUNK, A_CHUNK), 0)
    c_i = lax.broadcasted_iota(jnp.int32, (A_CHUNK, A_CHUNK), 1)
    gw = A_WIDTH // A_GROUPS
    s_rows = []
    for ck in range(rows // A_CHUNK):
        s_cols = []
        for g in range(A_GROUPS):
            w_g = jnp.where(c_i <= r_i, w_s_ref[g], 0.0)
            s_cols.append(_bdot(w_g, v[ck * A_CHUNK:(ck + 1) * A_CHUNK, g * gw:(g + 1) * gw]))
        s_rows.append(jnp.concatenate(s_cols, axis=1) + bs_ref[...])
    y_a = u * jnp.concatenate(s_rows, axis=0)

    zb = z[:, 2 * A_WIDTH:]
    gate = zb[:, B_WIDTH:]
    a = zb[:, :B_WIDTH] * (1.0 / (1.0 + jnp.exp(-gate)))
    hist_ref[CONV_HALO:CONV_HALO + rows, :] = a
    first = CONV_HALO - (CONV_WIDTH - 1)
    acc = jnp.zeros((rows, B_WIDTH), F32) + b_dw_ref[...]
    for k in range(CONV_WIDTH):
        acc = acc + w_dw_ref[k:k + 1, :] * hist_ref[first + k:first + k + rows, :]
    hist_ref[0:CONV_HALO, :] = hist_ref[rows:rows + CONV_HALO, :]
    yb = _layer_norm(acc, cg_ref[...], cb_ref[...])
    y_b = yb * (1.0 / (1.0 + jnp.exp(-yb)))

    y = _bdot(jnp.concatenate([y_a, y_b], axis=1), w_out_ref[...])
    _sublayer_epilogue(x, y, cond_ref, lng_ref, lnb_ref, wr_ref, br_ref, x_out_ref, hx_ref, meta_t_ref)


def _full(shape):
    nd = len(shape)
    return pl.BlockSpec(shape, lambda *_: (0,) * nd)


def _even_call(x2d, cond, bsz, seq, w_in, sgu_g, sgu_b, w_s, bs_full, w_dw, b_dw, cg, cb, w_out,
               lng, lnb, wr, br):
    n = bsz * seq
    n_t = seq // ROW_TILE
    row = lambda b, t: (b * n_t + t, 0)
    small = [sgu_g, sgu_b, w_s, bs_full, w_dw, b_dw, cg, cb, w_out, lng, lnb, wr, br]
    return pl.pallas_call(
        _even_kernel,
        grid=(bsz, n_t),
        in_specs=[pl.BlockSpec((ROW_TILE, D_MODEL), row),
                  pl.BlockSpec((None, 6, D_MODEL), lambda b, t: (b, 0, 0)),
                  _full(w_in.shape)] + [_full(a.shape) for a in small],
        out_specs=(pl.BlockSpec((ROW_TILE, D_MODEL), row),
                   pl.BlockSpec((ROW_TILE, HX_W), row),
                   pl.BlockSpec((SUBLANES, ROW_TILE), lambda b, t: (0, b * n_t + t))),
        out_shape=_epilogue_out(n),
        scratch_shapes=[pltpu.VMEM((CONV_HALO + ROW_TILE, B_WIDTH), F32)],
        compiler_params=_params("arbitrary", "arbitrary"),
        name="even_mixer",
    )(x2d, cond, w_in, *small)


def _rank_kernel(meta_t_ref, pos_ref, counts_ref, run_ref):
    phase = pl.program_id(0)
    i = pl.program_id(1)
    rows = meta_t_ref.shape[1]

    @pl.when((phase == 0) & (i == 0))
    def _():
        run_ref[...] = jnp.zeros_like(run_ref)

    bucket = meta_t_ref[0:1, :].astype(jnp.int32)
    sub = lax.broadcasted_iota(jnp.int32, (BUCKET_ROWS, rows), 0)
    onehot = jnp.where(sub == bucket, 1.0, 0.0)
    r_i = lax.broadcasted_iota(jnp.int32, (rows, rows), 0)
    c_i = lax.broadcasted_iota(jnp.int32, (rows, rows), 1)
    tri = jnp.where(r_i <= c_i, 1.0, 0.0)
    cum = _bdot(onehot, tri)
    tile_cnt = cum[:, rows - 1:rows]

    @pl.when((phase == 1) & (i == 0))
    def _():
        counts = run_ref[...]
        counts_ref[...] = counts
        padded = jnp.floor((counts + (MOE_TILE - 1)) * (1.0 / MOE_TILE)) * MOE_TILE
        sub_l = lax.broadcasted_iota(jnp.int32, counts.shape, 0)
        off = jnp.zeros_like(counts)
        for b in range(N_BUCKETS):
            off = off + jnp.where(sub_l > b, padded[b:b + 1, :], 0.0)
        run_ref[...] = off

    @pl.when(phase == 1)
    def _():
        base = run_ref[:, 0:1]
        pos = jnp.sum(onehot * (cum + base), axis=0, keepdims=True) - 1.0
        pos_ref[...] = pos.astype(jnp.int32)

    run_ref[...] = run_ref[...] + tile_cnt


def _rank_call(meta_t):
    n = meta_t.shape[1]
    n_t = n // ROW_TILE
    return pl.pallas_call(
        _rank_kernel,
        grid=(2, n_t),
        in_specs=[pl.BlockSpec((SUBLANES, ROW_TILE), lambda p, i: (0, i))],
        out_specs=(pl.BlockSpec((1, ROW_TILE), lambda p, i: (0, i * p)),
                   pl.BlockSpec((BUCKET_ROWS, LANES), lambda p, i: (0, 0))),
        out_shape=(jax.ShapeDtypeStruct((1, n), jnp.int32),
                   jax.ShapeDtypeStruct((BUCKET_ROWS, LANES), F32)),
        scratch_shapes=[pltpu.VMEM((BUCKET_ROWS, LANES), F32)],
        compiler_params=_params("arbitrary", "arbitrary"),
        name="moe_rank",
    )(meta_t)


DISPATCH_ROWS = 512
DMA_UNROLL = 8


def _dispatch_kernel(pos_ref, hx_ref, xs_in_ref, xs_ref, sem):
    del xs_in_ref
    rows = hx_ref.shape[0]
    base = pl.program_id(0) * rows

    def start_group(g, c):
        for u in range(DMA_UNROLL):
            r = g * DMA_UNROLL + u
            pltpu.make_async_copy(hx_ref.at[pl.ds(r, 1), :], xs_ref.at[pl.ds(pos_ref[base + r], 1), :],
                                  sem).start()
        return c

    lax.fori_loop(0, rows // DMA_UNROLL, start_group, 0)
    pltpu.make_async_copy(hx_ref, xs_ref.at[pl.ds(0, rows), :], sem).wait()


def _dispatch_call(pos, hx, n_sorted):
    n = hx.shape[0]
    xs0 = jnp.zeros((n_sorted, HX_W), F32)
    return pl.pallas_call(
        _dispatch_kernel,
        grid_spec=pltpu.PrefetchScalarGridSpec(
            num_scalar_prefetch=1,
            grid=(n // DISPATCH_ROWS,),
            in_specs=[pl.BlockSpec((DISPATCH_ROWS, HX_W), lambda i, p: (i, 0)),
                      pl.BlockSpec(memory_space=pl.ANY)],
            out_specs=pl.BlockSpec(memory_space=pl.ANY),
            scratch_shapes=[pltpu.SemaphoreType.DMA(())]),
        out_shape=jax.ShapeDtypeStruct((n_sorted, HX_W), F32),
        input_output_aliases={2: 0},
        compiler_params=_params("arbitrary"),
        name="moe_dispatch",
    )(pos, hx, xs0)


def _moe_kernel(lo_ref, hi_ref, nact_ref, xs_ref, wgu_lo_ref, wd_lo_ref, wgu_hi_ref, wd_hi_ref, ys_ref):
    del lo_ref, hi_ref

    @pl.when(pl.program_id(0) < nact_ref[0])
    def _():
        x = xs_ref[:, :D_MODEL].astype(BF16)
        w_lo = xs_ref[:, D_MODEL + 1:D_MODEL + 2]
        w_hi = xs_ref[:, D_MODEL + 2:D_MODEL + 3]

        def expert(wgu_ref, wd_ref):
            gu = jnp.dot(x, wgu_ref[...], preferred_element_type=F32)
            gate = gu[:, :D_EXPERT]
            hid = gate * (1.0 / (1.0 + jnp.exp(-gate))) * gu[:, D_EXPERT:]
            return jnp.dot(hid.astype(BF16), wd_ref[...], preferred_element_type=F32)

        ys_ref[...] = w_lo * expert(wgu_lo_ref, wd_lo_ref) + w_hi * expert(wgu_hi_ref, wd_hi_ref)

    @pl.when(pl.program_id(0) >= nact_ref[0])
    def _():
        ys_ref[...] = jnp.zeros_like(ys_ref)


def _moe_call(tile_lo, tile_hi, nact, xs, wgu, wd):
    n_sorted = xs.shape[0]
    n_tiles = n_sorted // MOE_TILE

    def tile(t, lo, hi, na):
        return (jnp.minimum(t, na[0] - 1), 0)

    def w_lo(t, lo, hi, na):
        return (lo[jnp.minimum(t, na[0] - 1)], 0, 0)

    def w_hi(t, lo, hi, na):
        return (hi[jnp.minimum(t, na[0] - 1)], 0, 0)

    return pl.pallas_call(
        _moe_kernel,
        grid_spec=pltpu.PrefetchScalarGridSpec(
            num_scalar_prefetch=3,
            grid=(n_tiles,),
            in_specs=[pl.BlockSpec((MOE_TILE, HX_W), tile),
                      pl.BlockSpec((None, D_MODEL, 2 * D_EXPERT), w_lo),
                      pl.BlockSpec((None, D_EXPERT, D_MODEL), w_lo),
                      pl.BlockSpec((None, D_MODEL, 2 * D_EXPERT), w_hi),
                      pl.BlockSpec((None, D_EXPERT, D_MODEL), w_hi)],
            out_specs=pl.BlockSpec((MOE_TILE, D_MODEL), lambda t, lo, hi, na: (t, 0))),
        out_shape=jax.ShapeDtypeStruct((n_sorted, D_MODEL), F32),
        compiler_params=_params("arbitrary"),
        name="moe_experts",
    )(tile_lo, tile_hi, nact, xs, wgu, wd, wgu, wd)


def _combine_kernel(pos_ref, x_ref, cond_ref, lng_ref, lnb_ref, ys_ref, o_ref, ybuf, sems):
    i = pl.program_id(0)
    n_steps = pl.num_programs(0)
    rows = x_ref.shape[0]

    def row_copy(step, slot, r):
        src = ys_ref.at[pl.ds(pos_ref[step * rows + r], 1), :]
        return pltpu.make_async_copy(src, ybuf.at[slot, pl.ds(r, 1), :], sems.at[slot])

    def issue(step, slot):
        def body(g, c):
            for u in range(DMA_UNROLL):
                row_copy(step, slot, g * DMA_UNROLL + u).start()
            return c
        lax.fori_loop(0, rows // DMA_UNROLL, body, 0)

    @pl.when(i == 0)
    def _():
        issue(0, 0)

    @pl.when(i + 1 < n_steps)
    def _():
        issue(i + 1, (i + 1) % 2)

    slot = i % 2

    pltpu.make_async_copy(ys_ref.at[pl.ds(0, rows), :], ybuf.at[slot], sems.at[slot]).wait()

    y = ybuf[slot]
    g_f = cond_ref[5:6, :]
    o_ref[...] = _layer_norm(DN_ALPHA * x_ref[...] + (1.0 + g_f) * y, lng_ref[...], lnb_ref[...])


def _combine_call(pos, x1, cond, bsz, seq, lng, lnb, ys):
    n = x1.shape[0]
    n_t = seq // ROW_TILE
    return pl.pallas_call(
        _combine_kernel,
        grid_spec=pltpu.PrefetchScalarGridSpec(
            num_scalar_prefetch=1,
            grid=(n // ROW_TILE,),
            in_specs=[pl.BlockSpec((ROW_TILE, D_MODEL), lambda i, p: (i, 0)),
                      pl.BlockSpec((None, 6, D_MODEL), lambda i, p: (i // n_t, 0, 0)),
                      pl.BlockSpec((1, D_MODEL), lambda i, p: (0, 0)),
                      pl.BlockSpec((1, D_MODEL), lambda i, p: (0, 0)),
                      pl.BlockSpec(memory_space=pl.ANY)],
            out_specs=pl.BlockSpec((ROW_TILE, D_MODEL), lambda i, p: (i, 0)),
            scratch_shapes=[pltpu.VMEM((2, ROW_TILE, D_MODEL), F32),
                            pltpu.SemaphoreType.DMA((2,))]),
        out_shape=jax.ShapeDtypeStruct((n, D_MODEL), F32),
        compiler_params=_params("arbitrary"),
        name="moe_combine",
    )(pos, x1, cond, lng, lnb, ys)


def _moe_sublayer(x1, hx, meta_t, cond, bsz, seq, lng, lnb, w_gate, w_up, w_down):
    n = x1.shape[0]
    n_sorted = n + N_BUCKETS * MOE_TILE
    pos2d, counts = _rank_call(meta_t)
    pos = pos2d.reshape(n)
    cnt = counts[:N_BUCKETS, 0].astype(jnp.int32)
    padded = ((cnt + MOE_TILE - 1) // MOE_TILE) * MOE_TILE
    ends = jnp.cumsum(padded)
    starts = jnp.arange(n_sorted // MOE_TILE, dtype=jnp.int32) * MOE_TILE
    tile_bucket = jnp.minimum(jnp.sum((ends[None, :] <= starts[:, None]).astype(jnp.int32), axis=1),
                              N_BUCKETS - 1)
    tile_lo = jnp.asarray(_BUCKET_LO)[tile_bucket]
    tile_hi = jnp.asarray(_BUCKET_HI)[tile_bucket]
    nact = (ends[-1:] // MOE_TILE).astype(jnp.int32)
    xs = _dispatch_call(pos, hx, n_sorted)
    wgu = jnp.concatenate([w_gate, w_up], axis=-1).astype(BF16)
    ys = _moe_call(tile_lo, tile_hi, nact, xs, wgu, w_down.astype(BF16))
    return _combine_call(pos, x1, cond, bsz, seq, lng, lnb, ys)


def _qkv_kernel(x_ref, cond_ref, wq_t_ref, wk_ref, wv_t_ref, q_t_ref, k_ref, v_t_ref, kmean_ref):
    h = (x_ref[...] * (1.0 + cond_ref[1:2, :]) + cond_ref[0:1, :]).astype(BF16)
    nt = (((1,), (1,)), ((), ()))
    q_t_ref[...] = lax.dot_general(wq_t_ref[...], h, nt, preferred_element_type=F32).astype(BF16)
    v_t_ref[...] = lax.dot_general(wv_t_ref[...], h, nt, preferred_element_type=F32).astype(BF16)
    k = jnp.dot(h, wk_ref[...], preferred_element_type=F32)
    k_ref[...] = k.astype(BF16)
    kmean_ref[...] = jnp.broadcast_to(jnp.mean(k, axis=0, keepdims=True), kmean_ref.shape)


def _qkv_call(x2d, cond, bsz, seq, wq_t, wk, wv_t):
    n_t = seq // MOBA_BLOCK
    hd = N_HEADS * HEAD_DIM
    return pl.pallas_call(
        _qkv_kernel,
        grid=(bsz, n_t),
        in_specs=[pl.BlockSpec((MOBA_BLOCK, D_MODEL), lambda b, t: (b * n_t + t, 0)),
                  pl.BlockSpec((None, 6, D_MODEL), lambda b, t: (b, 0, 0)),
                  _full(wq_t.shape), _full(wk.shape), _full(wv_t.shape)],
        out_specs=(pl.BlockSpec((None, hd, MOBA_BLOCK), lambda b, t: (b, 0, t)),
                   pl.BlockSpec((None, MOBA_BLOCK, hd), lambda b, t: (b, t, 0)),
                   pl.BlockSpec((None, hd, MOBA_BLOCK), lambda b, t: (b, 0, t)),
                   pl.BlockSpec((None, None, SUBLANES, hd), lambda b, t: (b, t, 0, 0))),
        out_shape=(jax.ShapeDtypeStruct((bsz, hd, seq), BF16),
                   jax.ShapeDtypeStruct((bsz, seq, hd), BF16),
                   jax.ShapeDtypeStruct((bsz, hd, seq), BF16),
                   jax.ShapeDtypeStruct((bsz, n_t, SUBLANES, hd), F32)),
        compiler_params=_params("arbitrary", "arbitrary"),
        name="qkv_proj",
    )(x2d, cond, wq_t, wk, wv_t)


def _moba_kernel(q_t_ref, k_ref, v_t_ref, kmean_ref, o_ref, sel_ref):
    i = pl.program_id(2)
    n_blocks = kmean_ref.shape[0]
    blk = MOBA_BLOCK
    q_t = q_t_ref[...]
    scale = HEAD_DIM ** -0.5

    km = kmean_ref[...]
    km_hi = km.astype(BF16)
    km_lo = (km - km_hi.astype(F32)).astype(BF16)
    gate = (jnp.dot(km_hi, q_t, preferred_element_type=F32)
            + jnp.dot(km_lo, q_t, preferred_element_type=F32))
    bidx = lax.broadcasted_iota(jnp.int32, gate.shape, 0)
    past = bidx < i
    gate = jnp.where(past, gate, -jnp.inf)
    rank = jnp.zeros(gate.shape, F32)
    for m in range(n_blocks):
        g_m = gate[m:m + 1, :]
        beats = (g_m > gate) | ((g_m == gate) & (m < bidx))
        rank = rank + jnp.where(beats, 1.0, 0.0)
    sel_ref[...] = jnp.where(past & (rank < MOBA_TOPK), 1.0, 0.0)

    def attend(j, allowed, carry):
        m_prev, l_prev, acc = carry
        start = pl.multiple_of(j * blk, blk)
        k_j = k_ref[pl.ds(start, blk), :]
        s = jnp.dot(k_j, q_t, preferred_element_type=F32) * scale
        s = jnp.where(allowed, s, NEG_BIG)
        m_new = jnp.maximum(m_prev, jnp.max(s, axis=0, keepdims=True))
        p = jnp.exp(s - m_new)
        corr = jnp.exp(m_prev - m_new)
        l_new = l_prev * corr + jnp.sum(p, axis=0, keepdims=True)
        v_j = v_t_ref[:, pl.ds(start, blk)]
        acc = acc * corr + jnp.dot(v_j, p.astype(BF16), preferred_element_type=F32)
        return m_new, l_new, acc

    key_i = lax.broadcasted_iota(jnp.int32, (blk, blk), 0)
    qry_i = lax.broadcasted_iota(jnp.int32, (blk, blk), 1)
    init = (jnp.full((1, blk), NEG_BIG, F32), jnp.zeros((1, blk), F32), jnp.zeros((HEAD_DIM, blk), F32))
    carry = attend(i, key_i <= qry_i, init)

    def past_block(j, carry):
        return attend(j, sel_ref[pl.ds(j, 1), :] > 0.5, carry)

    _, l_fin, acc = lax.fori_loop(0, i, past_block, carry)
    o_ref[...] = jnp.transpose(acc * (1.0 / l_fin)).astype(BF16)


def _moba_call(q_t, k, v_t, kmean):
    bsz, hd, seq = q_t.shape
    n_b = seq // MOBA_BLOCK
    return pl.pallas_call(
        _moba_kernel,
        grid=(bsz, N_HEADS, n_b),
        in_specs=[pl.BlockSpec((None, HEAD_DIM, MOBA_BLOCK), lambda b, h, i: (b, h, i)),
                  pl.BlockSpec((None, seq, HEAD_DIM), lambda b, h, i: (b, 0, h)),
                  pl.BlockSpec((None, HEAD_DIM, seq), lambda b, h, i: (b, h, 0)),
                  pl.BlockSpec((None, n_b, HEAD_DIM), lambda b, h, i: (b, 0, h))],
        out_specs=pl.BlockSpec((None, MOBA_BLOCK, HEAD_DIM), lambda b, h, i: (b, i, h)),
        out_shape=jax.ShapeDtypeStruct((bsz, seq, hd), BF16),
        scratch_shapes=[pltpu.VMEM((n_b, MOBA_BLOCK), F32)],
        compiler_params=_params("arbitrary", "arbitrary", "arbitrary"),
        name="moba_attention",
    )(q_t, k, v_t, kmean)


def _attn_out_kernel(x_ref, o_ref, cond_ref, w_out_ref, lng_ref, lnb_ref, wr_ref, br_ref,
                     x_out_ref, hx_ref, meta_t_ref):
    y = jnp.dot(o_ref[...], w_out_ref[...], preferred_element_type=F32)
    _sublayer_epilogue(x_ref[...], y, cond_ref, lng_ref, lnb_ref, wr_ref, br_ref, x_out_ref, hx_ref, meta_t_ref)


def _attn_out_call(x2d, o2d, cond, seq, w_out, lng, lnb, wr, br):
    n = x2d.shape[0]
    n_t = seq // ROW_TILE
    row = lambda i: (i, 0)
    small = [w_out, lng, lnb, wr, br]
    return pl.pallas_call(
        _attn_out_kernel,
        grid=(n // ROW_TILE,),
        in_specs=[pl.BlockSpec((ROW_TILE, D_MODEL), row),
                  pl.BlockSpec((ROW_TILE, D_MODEL), row),
                  pl.BlockSpec((None, 6, D_MODEL), lambda i: (i // n_t, 0, 0))]
                 + [_full(a.shape) for a in small],
        out_specs=(pl.BlockSpec((ROW_TILE, D_MODEL), row),
                   pl.BlockSpec((ROW_TILE, HX_W), row),
                   pl.BlockSpec((SUBLANES, ROW_TILE), lambda i: (0, i))),
        out_shape=_epilogue_out(n),
        compiler_params=_params("arbitrary"),
        name="attn_out",
    )(x2d, o2d, cond, *small)


def _router_weights(w_grp, b_grp, w_er, b_er):
    d = w_grp.shape[0]
    w_exp = jnp.transpose(w_er, (1, 0, 2)).reshape(d, N_EXPERTS)
    pad = LANES - N_GROUPS - N_EXPERTS
    wr = jnp.concatenate([w_grp, w_exp, jnp.zeros((d, pad), F32)], axis=1)
    br = jnp.concatenate([b_grp, b_er.reshape(N_EXPERTS), jnp.zeros((pad,), F32)]).reshape(1, LANES)
    return wr, br


def kernel(x, c, ada_w, ada_b, ln_mix_g, ln_mix_b, ln_ffn_g, ln_ffn_b, ev_w_in, ev_sgu_ln_g, ev_sgu_ln_b,
           ev_w_s, ev_b_s, ev_w_dw, ev_b_dw, ev_conv_ln_g, ev_conv_ln_b, ev_w_out, od_w_qkv, od_w_out,
           moe_w_grp, moe_b_grp, moe_w_er, moe_b_er, moe_w_gate, moe_w_up, moe_w_down):
    bsz, seq, d = x.shape
    n = bsz * seq
    depth = ada_w.shape[0]
    cond_all = _cond_call(c, ada_w, ada_b).reshape(depth, bsz, 6, d)
    row = lambda v: v.reshape(1, -1)
    xc = x.reshape(n, d)
    for i in range(depth):
        cond = cond_all[i]
        j = i // 2
        wr, br = _router_weights(moe_w_grp[i], moe_b_grp[i], moe_w_er[i], moe_b_er[i])
        lng, lnb = row(ln_mix_g[i]), row(ln_mix_b[i])
        if i % 2 == 0:
            gw = A_WIDTH // A_GROUPS
            bs_full = jnp.repeat(ev_b_s[j].T, gw, axis=1)
            w_dw = jnp.concatenate([ev_w_dw[j], jnp.zeros((1, B_WIDTH), F32)], axis=0)
            x1, hx, meta_t = _even_call(
                xc, cond, bsz, seq, ev_w_in[j].astype(BF16), row(ev_sgu_ln_g[j]), row(ev_sgu_ln_b[j]),
                ev_w_s[j], bs_full, w_dw, row(ev_b_dw[j]), row(ev_conv_ln_g[j]), row(ev_conv_ln_b[j]),
                ev_w_out[j].astype(BF16), lng, lnb, wr, br)
        else:
            hd = N_HEADS * HEAD_DIM
            w_qkv = od_w_qkv[j].astype(BF16)
            q_t, k, v_t, kmean8 = _qkv_call(xc, cond, bsz, seq, w_qkv[:, :hd].T, w_qkv[:, hd:2 * hd],
                                            w_qkv[:, 2 * hd:].T)
            o = _moba_call(q_t, k, v_t, kmean8[:, :, 0, :])
            x1, hx, meta_t = _attn_out_call(xc, o.reshape(n, hd), cond, seq, od_w_out[j].astype(BF16),
                                            lng, lnb, wr, br)
        xc = _moe_sublayer(x1, hx, meta_t, cond, bsz, seq, row(ln_ffn_g[i]), row(ln_ffn_b[i]),
                           moe_w_gate[i], moe_w_up[i], moe_w_down[i])
    return xc.reshape(bsz, seq, d)
```

```python
import functools

import jax
import jax.numpy as jnp
import numpy as np
from jax import lax
from jax.experimental import pallas as pl
from jax.experimental.pallas import tpu as pltpu

D_MODEL = 1024
A_WIDTH = 512
A_GROUPS = 4
A_CHUNK = 128
B_WIDTH = 512
CONV_WIDTH = 31
N_HEADS = 8
HEAD_DIM = D_MODEL // N_HEADS
MOBA_BLOCK = 256
MOBA_TOPK = 3
N_GROUPS = 4
EXPERTS_PER_GROUP = 4
N_EXPERTS = N_GROUPS * EXPERTS_PER_GROUP
D_EXPERT = 512
DEPTH = 2
DN_ALPHA = (2.0 * DEPTH) ** 0.25
LN_EPS = 1e-5

LANES = 128
SUBLANES = 8
ROW_TILE = 256
MOE_TILE = 256
CONV_HALO = 32
PAIRS_PER_GROUP = EXPERTS_PER_GROUP * (EXPERTS_PER_GROUP - 1) // 2
N_BUCKETS = N_GROUPS * PAIRS_PER_GROUP
BUCKET_ROWS = 32
ROUTER_LO_LANE = 32
META_W = LANES
HX_W = D_MODEL + META_W
VMEM_LIMIT = 56 * 1024 * 1024
NEG_BIG = -1e30
LOG2E = 1.4426950408889634
F32 = jnp.float32
BF16 = jnp.bfloat16

_PAIR_LO = np.array([0, 0, 0, 1, 1, 2], np.int32)
_PAIR_HI = np.array([1, 2, 3, 2, 3, 3], np.int32)
_BUCKET_LO = np.concatenate([g * EXPERTS_PER_GROUP + _PAIR_LO for g in range(N_GROUPS)])
_BUCKET_HI = np.concatenate([g * EXPERTS_PER_GROUP + _PAIR_HI for g in range(N_GROUPS)])


def _params(*sem):
    return pltpu.CompilerParams(dimension_semantics=sem, vmem_limit_bytes=VMEM_LIMIT)


def _layer_norm(x, g, b):
    mu = jnp.mean(x, axis=-1, keepdims=True)
    xc = x - mu
    var = jnp.mean(xc * xc, axis=-1, keepdims=True)
    return xc * lax.rsqrt(var + LN_EPS) * g + b


def _bdot(a, b):
    return jnp.dot(a.astype(BF16), b.astype(BF16), preferred_element_type=F32)


def _cond_kernel(c_ref, w_ref, b_ref, o_ref):
    c = c_ref[...]
    c_act = c * (1.0 / (1.0 + jnp.exp(-c)))
    o_ref[...] = jnp.dot(c_act, w_ref[...], precision=lax.Precision.HIGHEST,
                         preferred_element_type=F32) + b_ref[...]


def _cond_call(c, ada_w, ada_b):
    depth, d, d6 = ada_w.shape
    bsz = c.shape[0]
    n_col = d6 // d
    return pl.pallas_call(
        _cond_kernel,
        grid=(depth, n_col),
        in_specs=[pl.BlockSpec((bsz, d), lambda i, j: (0, 0)),
                  pl.BlockSpec((None, d, d), lambda i, j: (i, 0, j)),
                  pl.BlockSpec((None, 1, d), lambda i, j: (i, 0, j))],
        out_specs=pl.BlockSpec((None, bsz, d), lambda i, j: (i, 0, j)),
        out_shape=jax.ShapeDtypeStruct((depth, bsz, d6), F32),
        compiler_params=_params("arbitrary", "arbitrary"),
        name="cond",
    )(c, ada_w, ada_b.reshape(depth, 1, d6))


def _route(logits):
    lane = lax.broadcasted_iota(jnp.int32, logits.shape, 1)
    neg_inf = -jnp.inf
    far = LANES - 1
    gl = jnp.where(lane < N_GROUPS, logits, neg_inf)
    gmax = jnp.max(gl, axis=-1, keepdims=True)
    gsel = jnp.min(jnp.where(gl == gmax, lane, far), axis=-1, keepdims=True)
    gsum = jnp.sum(jnp.where(lane < N_GROUPS, jnp.exp(logits - gmax), 0.0), axis=-1, keepdims=True)
    g_w = 1.0 / gsum
    first = N_GROUPS + EXPERTS_PER_GROUP * gsel
    el = jnp.where((lane >= first) & (lane < first + EXPERTS_PER_GROUP), logits, neg_inf)
    v1 = jnp.max(el, axis=-1, keepdims=True)
    i1 = jnp.min(jnp.where(el == v1, lane, far), axis=-1, keepdims=True)
    el2 = jnp.where(lane == i1, neg_inf, el)
    v2 = jnp.max(el2, axis=-1, keepdims=True)
    i2 = jnp.min(jnp.where(el2 == v2, lane, far), axis=-1, keepdims=True)
    t = jnp.exp(v2 - v1)
    w1 = g_w / (1.0 + t)
    w2 = g_w * t / (1.0 + t)
    first_lower = i1 < i2
    a = jnp.minimum(i1, i2) - first
    b = jnp.maximum(i1, i2) - first
    pair = ((a * (2 * EXPERTS_PER_GROUP - 1 - a)) >> 1) + (b - a - 1)
    bucket = gsel * PAIRS_PER_GROUP + pair
    w_lo = jnp.where(first_lower, w1, w2)
    w_hi = jnp.where(first_lower, w2, w1)
    return bucket.astype(F32), w_lo, w_hi


def _sublayer_epilogue(x, y, cond_ref, lng_ref, lnb_ref, wr_ref, br_ref, x_out_ref, hx_ref, meta_t_ref):
    g_m = cond_ref[2:3, :]
    sh_f = cond_ref[3:4, :]
    sc_f = cond_ref[4:5, :]
    x1 = _layer_norm(DN_ALPHA * x + (1.0 + g_m) * y, lng_ref[...], lnb_ref[...])
    x_out_ref[...] = x1
    h2 = x1 * (1.0 + sc_f) + sh_f
    h_hi = h2.astype(BF16)
    h_lo = (h2 - h_hi.astype(F32)).astype(BF16)
    both = (jnp.dot(h_hi, wr_ref[...], preferred_element_type=F32)
            + jnp.dot(h_lo, wr_ref[...], preferred_element_type=F32))
    logits = both + pltpu.roll(both, LANES - ROUTER_LO_LANE, axis=1) + br_ref[...]
    bucket, w_lo, w_hi = _route(logits)
    lane = lax.broadcasted_iota(jnp.int32, logits.shape, 1)
    meta = jnp.where(lane == 0, bucket, jnp.where(lane == 1, w_lo, jnp.where(lane == 2, w_hi, 0.0)))
    hx_ref[:, :D_MODEL] = h2
    hx_ref[:, D_MODEL:] = meta
    meta_t_ref[...] = jnp.transpose(meta)[:SUBLANES, :]


def _epilogue_out(n):
    shapes = (jax.ShapeDtypeStruct((n, D_MODEL), F32),
              jax.ShapeDtypeStruct((n, HX_W), F32),
              jax.ShapeDtypeStruct((SUBLANES, n), F32))
    return shapes


def _gelu(x):
    return 0.5 * x * (1.0 + lax.erf(x * (2.0 ** -0.5)))


def _even_kernel(x_ref, cond_ref, w_in_ref, sgu_g_ref, sgu_b_ref, w_s_ref, bs_ref, w_dw_ref, b_dw_ref,
                 cg_ref, cb_ref, w_out_ref, lng_ref, lnb_ref, wr_ref, br_ref,
                 x_out_ref, hx_ref, meta_t_ref, hist_ref):
    rows = x_ref.shape[0]
    t = pl.program_id(1)

    @pl.when(t == 0)
    def _():
        hist_ref[0:CONV_HALO, :] = jnp.zeros((CONV_HALO, B_WIDTH), F32)

    x = x_ref[...]
    h = x * (1.0 + cond_ref[1:2, :]) + cond_ref[0:1, :]
    z = _bdot(h, w_in_ref[...])

    u = _gelu(z[:, :A_WIDTH])
    v = _layer_norm(_gelu(z[:, A_WIDTH:2 * A_WIDTH]), sgu_g_ref[...], sgu_b_ref[...])
    r_i = lax.broadcasted_iota(jnp.int32, (A_CHUNK, A_CHUNK), 0)
    c_i = lax.broadcasted_iota(jnp.int32, (A_CHUNK, A_CHUNK), 1)
    gw = A_WIDTH // A_GROUPS
    s_rows = []
    for ck in range(rows // A_CHUNK):
        s_cols = []
        for g in range(A_GROUPS):
            w_g = jnp.where(c_i <= r_i, w_s_ref[g], 0.0)
            s_cols.append(_bdot(w_g, v[ck * A_CHUNK:(ck + 1) * A_CHUNK, g * gw:(g + 1) * gw]))
        s_rows.append(jnp.concatenate(s_cols, axis=1) + bs_ref[...])
    y_a = u * jnp.concatenate(s_rows, axis=0)

    zb = z[:, 2 * A_WIDTH:]
    gate = zb[:, B_WIDTH:]
    a = zb[:, :B_WIDTH] * (1.0 / (1.0 + jnp.exp(-gate)))
    hist_ref[CONV_HALO:CONV_HALO + rows, :] = a
    first = CONV_HALO - (CONV_WIDTH - 1)
    acc = jnp.zeros((rows, B_WIDTH), F32) + b_dw_ref[...]
    for r in range(SUBLANES):
        taps = [k for k in range(CONV_WIDTH) if (first + k) % SUBLANES == r]
        if not taps:
            continue
        span = max(first + k for k in taps) - r + rows
        shifted = hist_ref[r:r + span, :]
        for k in taps:
            lo = first + k - r
            acc = acc + w_dw_ref[k:k + 1, :] * shifted[lo:lo + rows, :]
    hist_ref[0:CONV_HALO, :] = hist_ref[rows:rows + CONV_HALO, :]
    yb = _layer_norm(acc, cg_ref[...], cb_ref[...])
    y_b = yb * (1.0 / (1.0 + jnp.exp(-yb)))

    y = _bdot(jnp.concatenate([y_a, y_b], axis=1), w_out_ref[...])
    _sublayer_epilogue(x, y, cond_ref, lng_ref, lnb_ref, wr_ref, br_ref, x_out_ref, hx_ref, meta_t_ref)


def _full(shape):
    nd = len(shape)
    return pl.BlockSpec(shape, lambda *_: (0,) * nd)


def _even_call(x2d, cond, bsz, seq, w_in, sgu_g, sgu_b, w_s, bs_full, w_dw, b_dw, cg, cb, w_out,
               lng, lnb, wr, br):
    n = bsz * seq
    n_t = seq // ROW_TILE
    row = lambda b, t: (b * n_t + t, 0)
    small = [sgu_g, sgu_b, w_s, bs_full, w_dw, b_dw, cg, cb, w_out, lng, lnb, wr, br]
    return pl.pallas_call(
        _even_kernel,
        grid=(bsz, n_t),
        in_specs=[pl.BlockSpec((ROW_TILE, D_MODEL), row),
                  pl.BlockSpec((None, 6, D_MODEL), lambda b, t: (b, 0, 0)),
                  _full(w_in.shape)] + [_full(a.shape) for a in small],
        out_specs=(pl.BlockSpec((ROW_TILE, D_MODEL), row),
                   pl.BlockSpec((ROW_TILE, HX_W), row),
                   pl.BlockSpec((SUBLANES, ROW_TILE), lambda b, t: (0, b * n_t + t))),
        out_shape=_epilogue_out(n),
        scratch_shapes=[pltpu.VMEM((CONV_HALO + ROW_TILE, B_WIDTH), F32)],
        compiler_params=_params("arbitrary", "arbitrary"),
        name="even_mixer",
    )(x2d, cond, w_in, *small)


RANK_SUB = 256
RANK_ROWS = 2048


def _rank_kernel(meta_t_ref, pos_ref, counts_ref, run_ref):
    phase = pl.program_id(0)
    i = pl.program_id(1)
    rows = meta_t_ref.shape[1]

    @pl.when((phase == 0) & (i == 0))
    def _():
        run_ref[...] = jnp.zeros_like(run_ref)

    @pl.when((phase == 1) & (i == 0))
    def _():
        counts = run_ref[...]
        counts_ref[...] = counts
        padded = jnp.floor((counts + (MOE_TILE - 1)) * (1.0 / MOE_TILE)) * MOE_TILE
        sub_l = lax.broadcasted_iota(jnp.int32, counts.shape, 0)
        off = jnp.zeros_like(counts)
        for b in range(N_BUCKETS):
            off = off + jnp.where(sub_l > b, padded[b:b + 1, :], 0.0)
        run_ref[...] = off

    r_i = lax.broadcasted_iota(jnp.int32, (RANK_SUB, RANK_SUB), 0)
    c_i = lax.broadcasted_iota(jnp.int32, (RANK_SUB, RANK_SUB), 1)
    tri = jnp.where(r_i <= c_i, 1.0, 0.0).astype(BF16)
    sub = lax.broadcasted_iota(jnp.int32, (BUCKET_ROWS, RANK_SUB), 0)
    run = run_ref[...]
    for t in range(rows // RANK_SUB):
        bucket = meta_t_ref[0:1, t * RANK_SUB:(t + 1) * RANK_SUB].astype(jnp.int32)
        onehot = jnp.where(sub == bucket, 1.0, 0.0)
        cum = jnp.dot(onehot.astype(BF16), tri, preferred_element_type=F32)
        pos = jnp.sum(onehot * (cum + run[:, 0:1]), axis=0, keepdims=True) - 1.0
        pos_ref[:, t * RANK_SUB:(t + 1) * RANK_SUB] = pos.astype(jnp.int32)
        run = run + cum[:, RANK_SUB - 1:RANK_SUB]
    run_ref[...] = run


def _rank_call(meta_t):
    n = meta_t.shape[1]
    n_t = n // RANK_ROWS
    return pl.pallas_call(
        _rank_kernel,
        grid=(2, n_t),
        in_specs=[pl.BlockSpec((SUBLANES, RANK_ROWS), lambda p, i: (0, i))],
        out_specs=(pl.BlockSpec((1, RANK_ROWS), lambda p, i: (0, i * p)),
                   pl.BlockSpec((BUCKET_ROWS, LANES), lambda p, i: (0, 0))),
        out_shape=(jax.ShapeDtypeStruct((1, n), jnp.int32),
                   jax.ShapeDtypeStruct((BUCKET_ROWS, LANES), F32)),
        scratch_shapes=[pltpu.VMEM((BUCKET_ROWS, LANES), F32)],
        compiler_params=_params("arbitrary", "arbitrary"),
        name="moe_rank",
    )(meta_t)


DISPATCH_ROWS = 512
DMA_UNROLL = 8


def _dispatch_kernel(pos_ref, hx_ref, xs_in_ref, xs_ref, sem):
    del xs_in_ref
    rows = hx_ref.shape[0]
    base = pl.program_id(0) * rows

    def start_group(g, c):
        for u in range(DMA_UNROLL):
            r = g * DMA_UNROLL + u
            pltpu.make_async_copy(hx_ref.at[pl.ds(r, 1), :], xs_ref.at[pl.ds(pos_ref[base + r], 1), :],
                                  sem).start()
        return c

    lax.fori_loop(0, rows // DMA_UNROLL, start_group, 0)
    pltpu.make_async_copy(hx_ref, xs_ref.at[pl.ds(0, rows), :], sem).wait()


def _dispatch_call(pos, hx, n_sorted):
    n = hx.shape[0]
    xs0 = jnp.zeros((n_sorted, HX_W), F32)
    return pl.pallas_call(
        _dispatch_kernel,
        grid_spec=pltpu.PrefetchScalarGridSpec(
            num_scalar_prefetch=1,
            grid=(n // DISPATCH_ROWS,),
            in_specs=[pl.BlockSpec((DISPATCH_ROWS, HX_W), lambda i, p: (i, 0)),
                      pl.BlockSpec(memory_space=pl.ANY)],
            out_specs=pl.BlockSpec(memory_space=pl.ANY),
            scratch_shapes=[pltpu.SemaphoreType.DMA(())]),
        out_shape=jax.ShapeDtypeStruct((n_sorted, HX_W), F32),
        input_output_aliases={2: 0},
        compiler_params=_params("arbitrary"),
        name="moe_dispatch",
    )(pos, hx, xs0)


def _moe_kernel(lo_ref, hi_ref, nact_ref, xs_ref, wgu_lo_ref, wd_lo_ref, wgu_hi_ref, wd_hi_ref, ys_ref):
    del lo_ref, hi_ref

    @pl.when(pl.program_id(0) < nact_ref[0])
    def _():
        x = xs_ref[:, :D_MODEL].astype(BF16)
        w_lo = xs_ref[:, D_MODEL + 1:D_MODEL + 2]
        w_hi = xs_ref[:, D_MODEL + 2:D_MODEL + 3]

        def expert(wgu_ref, wd_ref):
            gu = jnp.dot(x, wgu_ref[...], preferred_element_type=F32)
            gate = gu[:, :D_EXPERT]
            hid = gate * (1.0 / (1.0 + jnp.exp(-gate))) * gu[:, D_EXPERT:]
            return jnp.dot(hid.astype(BF16), wd_ref[...], preferred_element_type=F32)

        ys_ref[...] = w_lo * expert(wgu_lo_ref, wd_lo_ref) + w_hi * expert(wgu_hi_ref, wd_hi_ref)

    @pl.when(pl.program_id(0) >= nact_ref[0])
    def _():
        ys_ref[...] = jnp.zeros_like(ys_ref)


def _moe_call(tile_lo, tile_hi, nact, xs, wgu, wd):
    n_sorted = xs.shape[0]
    n_tiles = n_sorted // MOE_TILE

    def tile(t, lo, hi, na):
        return (jnp.minimum(t, na[0] - 1), 0)

    def w_lo(t, lo, hi, na):
        return (lo[jnp.minimum(t, na[0] - 1)], 0, 0)

    def w_hi(t, lo, hi, na):
        return (hi[jnp.minimum(t, na[0] - 1)], 0, 0)

    return pl.pallas_call(
        _moe_kernel,
        grid_spec=pltpu.PrefetchScalarGridSpec(
            num_scalar_prefetch=3,
            grid=(n_tiles,),
            in_specs=[pl.BlockSpec((MOE_TILE, HX_W), tile),
                      pl.BlockSpec((None, D_MODEL, 2 * D_EXPERT), w_lo),
                      pl.BlockSpec((None, D_EXPERT, D_MODEL), w_lo),
                      pl.BlockSpec((None, D_MODEL, 2 * D_EXPERT), w_hi),
                      pl.BlockSpec((None, D_EXPERT, D_MODEL), w_hi)],
            out_specs=pl.BlockSpec((MOE_TILE, D_MODEL), lambda t, lo, hi, na: (t, 0))),
        out_shape=jax.ShapeDtypeStruct((n_sorted, D_MODEL), F32),
        compiler_params=_params("arbitrary"),
        name="moe_experts",
    )(tile_lo, tile_hi, nact, xs, wgu, wd, wgu, wd)


def _combine_kernel(pos_ref, x_ref, cond_ref, lng_ref, lnb_ref, ys_ref, o_ref, ybuf, sems):
    i = pl.program_id(0)
    n_steps = pl.num_programs(0)
    rows = x_ref.shape[0]

    def row_copy(step, slot, r):
        src = ys_ref.at[pl.ds(pos_ref[step * rows + r], 1), :]
        return pltpu.make_async_copy(src, ybuf.at[slot, pl.ds(r, 1), :], sems.at[slot])

    def issue(step, slot):
        def body(g, c):
            for u in range(DMA_UNROLL):
                row_copy(step, slot, g * DMA_UNROLL + u).start()
            return c
        lax.fori_loop(0, rows // DMA_UNROLL, body, 0)

    @pl.when(i == 0)
    def _():
        issue(0, 0)

    @pl.when(i + 1 < n_steps)
    def _():
        issue(i + 1, (i + 1) % 2)

    slot = i % 2

    pltpu.make_async_copy(ys_ref.at[pl.ds(0, rows), :], ybuf.at[slot], sems.at[slot]).wait()

    y = ybuf[slot]
    g_f = cond_ref[5:6, :]
    o_ref[...] = _layer_norm(DN_ALPHA * x_ref[...] + (1.0 + g_f) * y, lng_ref[...], lnb_ref[...])


def _combine_call(pos, x1, cond, bsz, seq, lng, lnb, ys):
    n = x1.shape[0]
    n_t = seq // ROW_TILE
    return pl.pallas_call(
        _combine_kernel,
        grid_spec=pltpu.PrefetchScalarGridSpec(
            num_scalar_prefetch=1,
            grid=(n // ROW_TILE,),
            in_specs=[pl.BlockSpec((ROW_TILE, D_MODEL), lambda i, p: (i, 0)),
                      pl.BlockSpec((None, 6, D_MODEL), lambda i, p: (i // n_t, 0, 0)),
                      pl.BlockSpec((1, D_MODEL), lambda i, p: (0, 0)),
                      pl.BlockSpec((1, D_MODEL), lambda i, p: (0, 0)),
                      pl.BlockSpec(memory_space=pl.ANY)],
            out_specs=pl.BlockSpec((ROW_TILE, D_MODEL), lambda i, p: (i, 0)),
            scratch_shapes=[pltpu.VMEM((2, ROW_TILE, D_MODEL), F32),
                            pltpu.SemaphoreType.DMA((2,))]),
        out_shape=jax.ShapeDtypeStruct((n, D_MODEL), F32),
        compiler_params=_params("arbitrary"),
        name="moe_combine",
    )(pos, x1, cond, lng, lnb, ys)


def _moe_sublayer(x1, hx, meta_t, cond, bsz, seq, lng, lnb, w_gate, w_up, w_down):
    n = x1.shape[0]
    n_sorted = n + N_BUCKETS * MOE_TILE
    pos2d, counts = _rank_call(meta_t)
    pos = pos2d.reshape(n)
    cnt = counts[:N_BUCKETS, 0].astype(jnp.int32)
    padded = ((cnt + MOE_TILE - 1) // MOE_TILE) * MOE_TILE
    ends = jnp.cumsum(padded)
    starts = jnp.arange(n_sorted // MOE_TILE, dtype=jnp.int32) * MOE_TILE
    tile_bucket = jnp.minimum(jnp.sum((ends[None, :] <= starts[:, None]).astype(jnp.int32), axis=1),
                              N_BUCKETS - 1)
    tile_lo = jnp.asarray(_BUCKET_LO)[tile_bucket]
    tile_hi = jnp.asarray(_BUCKET_HI)[tile_bucket]
    nact = (ends[-1:] // MOE_TILE).astype(jnp.int32)
    xs = _dispatch_call(pos, hx, n_sorted)
    wgu = jnp.concatenate([w_gate, w_up], axis=-1).astype(BF16)
    ys = _moe_call(tile_lo, tile_hi, nact, xs, wgu, w_down.astype(BF16))
    return _combine_call(pos, x1, cond, bsz, seq, lng, lnb, ys)


def _qkv_kernel(x_ref, cond_ref, wq_t_ref, wk_ref, wv_t_ref, q_t_ref, k_ref, v_t_ref, kmean_ref):
    h = (x_ref[...] * (1.0 + cond_ref[1:2, :]) + cond_ref[0:1, :]).astype(BF16)
    nt = (((1,), (1,)), ((), ()))
    q_t_ref[...] = lax.dot_general(wq_t_ref[...], h, nt, preferred_element_type=F32).astype(BF16)
    v_t_ref[...] = lax.dot_general(wv_t_ref[...], h, nt, preferred_element_type=F32).astype(BF16)
    k = jnp.dot(h, wk_ref[...], preferred_element_type=F32)
    k_ref[...] = k.astype(BF16)
    kmean_ref[...] = jnp.broadcast_to(jnp.mean(k, axis=0, keepdims=True), kmean_ref.shape)


def _qkv_call(x2d, cond, bsz, seq, wq_t, wk, wv_t):
    n_t = seq // MOBA_BLOCK
    hd = N_HEADS * HEAD_DIM
    return pl.pallas_call(
        _qkv_kernel,
        grid=(bsz, n_t),
        in_specs=[pl.BlockSpec((MOBA_BLOCK, D_MODEL), lambda b, t: (b * n_t + t, 0)),
                  pl.BlockSpec((None, 6, D_MODEL), lambda b, t: (b, 0, 0)),
                  _full(wq_t.shape), _full(wk.shape), _full(wv_t.shape)],
        out_specs=(pl.BlockSpec((None, hd, MOBA_BLOCK), lambda b, t: (b, 0, t)),
                   pl.BlockSpec((None, MOBA_BLOCK, hd), lambda b, t: (b, t, 0)),
                   pl.BlockSpec((None, hd, MOBA_BLOCK), lambda b, t: (b, 0, t)),
                   pl.BlockSpec((None, None, SUBLANES, hd), lambda b, t: (b, t, 0, 0))),
        out_shape=(jax.ShapeDtypeStruct((bsz, hd, seq), BF16),
                   jax.ShapeDtypeStruct((bsz, seq, hd), BF16),
                   jax.ShapeDtypeStruct((bsz, hd, seq), BF16),
                   jax.ShapeDtypeStruct((bsz, n_t, SUBLANES, hd), F32)),
        compiler_params=_params("arbitrary", "arbitrary"),
        name="qkv_proj",
    )(x2d, cond, wq_t, wk, wv_t)


def _moba_kernel(q_t_ref, k_ref, v_t_ref, kmean_ref, o_ref):
    n_blocks = kmean_ref.shape[0]
    blk = MOBA_BLOCK
    c = HEAD_DIM ** -0.5 * LOG2E
    km = kmean_ref[...]
    km_hi = km.astype(BF16)
    km_lo = (km - km_hi.astype(F32)).astype(BF16)
    causal = (lax.broadcasted_iota(jnp.int32, (blk, blk), 0)
              <= lax.broadcasted_iota(jnp.int32, (blk, blk), 1))
    bidx = lax.broadcasted_iota(jnp.int32, (n_blocks, blk), 0)
    for i in range(n_blocks):
        q_t = q_t_ref[:, i * blk:(i + 1) * blk]
        n_keys = (i + 1) * blk
        s = jnp.dot(k_ref[0:n_keys, :], q_t, preferred_element_type=F32)
        s_own = jnp.where(causal, s[i * blk:, :], NEG_BIG)
        m = jnp.max(s_own, axis=0, keepdims=True)
        sel = []
        if i > 0:
            gate = (jnp.dot(km_hi, q_t, preferred_element_type=F32)
                    + jnp.dot(km_lo, q_t, preferred_element_type=F32))
            gate = jnp.where(bidx < i, gate, -jnp.inf)
            rank = jnp.zeros(gate.shape, F32)
            for j in range(i):
                g_j = gate[j:j + 1, :]
                beats = (g_j > gate) | ((g_j == gate) & (j < bidx))
                rank = rank + jnp.where(beats, 1.0, 0.0)
            chosen = jnp.where((bidx < i) & (rank < MOBA_TOPK), 1.0, 0.0)
            for j in range(i):
                sel_j = chosen[j:j + 1, :] > 0.5
                sel.append(sel_j)
                bm = jnp.max(s[j * blk:(j + 1) * blk, :], axis=0, keepdims=True)
                m = jnp.maximum(m, jnp.where(sel_j, bm, NEG_BIG))
        p_own = jnp.exp2((s_own - m) * c)
        l = jnp.sum(p_own, axis=0, keepdims=True)
        parts = []
        for j in range(i):
            shift = jnp.where(sel[j], m, -NEG_BIG)
            p_j = jnp.exp2((s[j * blk:(j + 1) * blk, :] - shift) * c)
            l = l + jnp.sum(p_j, axis=0, keepdims=True)
            parts.append(p_j.astype(BF16))
        parts.append(p_own.astype(BF16))
        p = jnp.concatenate(parts, axis=0) if i > 0 else parts[0]
        acc = jnp.dot(v_t_ref[:, 0:n_keys], p, preferred_element_type=F32)
        o_ref[i * blk:(i + 1) * blk, :] = jnp.transpose(acc * (1.0 / l)).astype(BF16)


def _moba_call(q_t, k, v_t, kmean):
    bsz, hd, seq = q_t.shape
    n_b = seq // MOBA_BLOCK
    return pl.pallas_call(
        _moba_kernel,
        grid=(bsz, N_HEADS),
        in_specs=[pl.BlockSpec((None, HEAD_DIM, seq), lambda b, h: (b, h, 0)),
                  pl.BlockSpec((None, seq, HEAD_DIM), lambda b, h: (b, 0, h)),
                  pl.BlockSpec((None, HEAD_DIM, seq), lambda b, h: (b, h, 0)),
                  pl.BlockSpec((None, n_b, HEAD_DIM), lambda b, h: (b, 0, h))],
        out_specs=pl.BlockSpec((None, seq, HEAD_DIM), lambda b, h: (b, 0, h)),
        out_shape=jax.ShapeDtypeStruct((bsz, seq, hd), BF16),
        compiler_params=_params("arbitrary", "arbitrary"),
        name="moba_attention",
    )(q_t, k, v_t, kmean)


def _attn_out_kernel(x_ref, o_ref, cond_ref, w_out_ref, lng_ref, lnb_ref, wr_ref, br_ref,
                     x_out_ref, hx_ref, meta_t_ref):
    y = jnp.dot(o_ref[...], w_out_ref[...], preferred_element_type=F32)
    _sublayer_epilogue(x_ref[...], y, cond_ref, lng_ref, lnb_ref, wr_ref, br_ref, x_out_ref, hx_ref, meta_t_ref)


def _attn_out_call(x2d, o2d, cond, seq, w_out, lng, lnb, wr, br):
    n = x2d.shape[0]
    n_t = seq // ROW_TILE
    row = lambda i: (i, 0)
    small = [w_out, lng, lnb, wr, br]
    return pl.pallas_call(
        _attn_out_kernel,
        grid=(n // ROW_TILE,),
        in_specs=[pl.BlockSpec((ROW_TILE, D_MODEL), row),
                  pl.BlockSpec((ROW_TILE, D_MODEL), row),
                  pl.BlockSpec((None, 6, D_MODEL), lambda i: (i // n_t, 0, 0))]
                 + [_full(a.shape) for a in small],
        out_specs=(pl.BlockSpec((ROW_TILE, D_MODEL), row),
                   pl.BlockSpec((ROW_TILE, HX_W), row),
                   pl.BlockSpec((SUBLANES, ROW_TILE), lambda i: (0, i))),
        out_shape=_epilogue_out(n),
        compiler_params=_params("arbitrary"),
        name="attn_out",
    )(x2d, o2d, cond, *small)


def _router_weights(w_grp, b_grp, w_er, b_er):
    d = w_grp.shape[0]
    w_exp = jnp.transpose(w_er, (1, 0, 2)).reshape(d, N_EXPERTS)
    pad = LANES - N_GROUPS - N_EXPERTS
    w_cat = jnp.concatenate([w_grp, w_exp, jnp.zeros((d, ROUTER_LO_LANE - N_GROUPS - N_EXPERTS), F32)], axis=1)
    w_hi = w_cat.astype(BF16)
    w_lo = (w_cat - w_hi.astype(F32)).astype(BF16)
    wr = jnp.concatenate([w_hi, w_lo, jnp.zeros((d, LANES - 2 * ROUTER_LO_LANE), BF16)], axis=1)
    br = jnp.concatenate([b_grp, b_er.reshape(N_EXPERTS), jnp.zeros((pad,), F32)]).reshape(1, LANES)
    return wr, br


def kernel(x, c, ada_w, ada_b, ln_mix_g, ln_mix_b, ln_ffn_g, ln_ffn_b, ev_w_in, ev_sgu_ln_g, ev_sgu_ln_b,
           ev_w_s, ev_b_s, ev_w_dw, ev_b_dw, ev_conv_ln_g, ev_conv_ln_b, ev_w_out, od_w_qkv, od_w_out,
           moe_w_grp, moe_b_grp, moe_w_er, moe_b_er, moe_w_gate, moe_w_up, moe_w_down):
    bsz, seq, d = x.shape
    n = bsz * seq
    depth = ada_w.shape[0]
    cond_all = _cond_call(c, ada_w, ada_b).reshape(depth, bsz, 6, d)
    row = lambda v: v.reshape(1, -1)
    xc = x.reshape(n, d)
    for i in range(depth):
        cond = cond_all[i]
        j = i // 2
        wr, br = _router_weights(moe_w_grp[i], moe_b_grp[i], moe_w_er[i], moe_b_er[i])
        lng, lnb = row(ln_mix_g[i]), row(ln_mix_b[i])
        if i % 2 == 0:
            gw = A_WIDTH // A_GROUPS
            bs_full = jnp.repeat(ev_b_s[j].T, gw, axis=1)
            w_dw = jnp.concatenate([ev_w_dw[j], jnp.zeros((1, B_WIDTH), F32)], axis=0)
            x1, hx, meta_t = _even_call(
                xc, cond, bsz, seq, ev_w_in[j].astype(BF16), row(ev_sgu_ln_g[j]), row(ev_sgu_ln_b[j]),
                ev_w_s[j], bs_full, w_dw, row(ev_b_dw[j]), row(ev_conv_ln_g[j]), row(ev_conv_ln_b[j]),
                ev_w_out[j].astype(BF16), lng, lnb, wr, br)
        else:
            hd = N_HEADS * HEAD_DIM
            w_qkv = od_w_qkv[j].astype(BF16)
            q_t, k, v_t, kmean8 = _qkv_call(xc, cond, bsz, seq, w_qkv[:, :hd].T, w_qkv[:, hd:2 * hd],
                                            w_qkv[:, 2 * hd:].T)
            o = _moba_call(q_t, k, v_t, kmean8[:, :, 0, :])
            x1, hx, meta_t = _attn_out_call(xc, o.reshape(n, hd), cond, seq, od_w_out[j].astype(BF16),
                                            lng, lnb, wr, br)
        xc = _moe_sublayer(x1, hx, meta_t, cond, bsz, seq, row(ln_ffn_g[i]), row(ln_ffn_b[i]),
                           moe_w_gate[i], moe_w_up[i], moe_w_down[i])
    return xc.reshape(bsz, seq, d)
```

```python
import functools

import jax
import jax.numpy as jnp
import numpy as np
from jax import lax
from jax.experimental import pallas as pl
from jax.experimental.pallas import tpu as pltpu

D_MODEL = 1024
A_WIDTH = 512
A_GROUPS = 4
A_CHUNK = 128
B_WIDTH = 512
CONV_WIDTH = 31
N_HEADS = 8
HEAD_DIM = D_MODEL // N_HEADS
MOBA_BLOCK = 256
MOBA_TOPK = 3
N_GROUPS = 4
EXPERTS_PER_GROUP = 4
N_EXPERTS = N_GROUPS * EXPERTS_PER_GROUP
D_EXPERT = 512
DEPTH = 2
DN_ALPHA = (2.0 * DEPTH) ** 0.25
LN_EPS = 1e-5

LANES = 128
SUBLANES = 8
ROW_TILE = 256
MOE_TILE = 256
CONV_HALO = 32
PAIRS_PER_GROUP = EXPERTS_PER_GROUP * (EXPERTS_PER_GROUP - 1) // 2
N_BUCKETS = N_GROUPS * PAIRS_PER_GROUP
BUCKET_ROWS = 32
ROUTER_LO_LANE = 32
META_W = LANES
HX_W = D_MODEL + META_W
VMEM_LIMIT = 56 * 1024 * 1024
NEG_BIG = -1e30
LOG2E = 1.4426950408889634
F32 = jnp.float32
BF16 = jnp.bfloat16

_PAIR_LO = np.array([0, 0, 0, 1, 1, 2], np.int32)
_PAIR_HI = np.array([1, 2, 3, 2, 3, 3], np.int32)
_BUCKET_LO = np.concatenate([g * EXPERTS_PER_GROUP + _PAIR_LO for g in range(N_GROUPS)])
_BUCKET_HI = np.concatenate([g * EXPERTS_PER_GROUP + _PAIR_HI for g in range(N_GROUPS)])


def _params(*sem):
    return pltpu.CompilerParams(dimension_semantics=sem, vmem_limit_bytes=VMEM_LIMIT)


def _layer_norm(x, g, b):
    mu = jnp.mean(x, axis=-1, keepdims=True)
    xc = x - mu
    var = jnp.mean(xc * xc, axis=-1, keepdims=True)
    return xc * lax.rsqrt(var + LN_EPS) * g + b


def _bdot(a, b):
    return jnp.dot(a.astype(BF16), b.astype(BF16), preferred_element_type=F32)


def _cond_kernel(c_ref, w_ref, b_ref, o_ref):
    c = c_ref[...]
    c_act = c * (1.0 / (1.0 + jnp.exp(-c)))
    o_ref[...] = jnp.dot(c_act, w_ref[...], precision=lax.Precision.HIGHEST,
                         preferred_element_type=F32) + b_ref[...]


def _cond_call(c, ada_w, ada_b):
    depth, d, d6 = ada_w.shape
    bsz = c.shape[0]
    n_col = d6 // d
    return pl.pallas_call(
        _cond_kernel,
        grid=(depth, n_col),
        in_specs=[pl.BlockSpec((bsz, d), lambda i, j: (0, 0)),
                  pl.BlockSpec((None, d, d), lambda i, j: (i, 0, j)),
                  pl.BlockSpec((None, 1, d), lambda i, j: (i, 0, j))],
        out_specs=pl.BlockSpec((None, bsz, d), lambda i, j: (i, 0, j)),
        out_shape=jax.ShapeDtypeStruct((depth, bsz, d6), F32),
        compiler_params=_params("arbitrary", "arbitrary"),
        name="cond",
    )(c, ada_w, ada_b.reshape(depth, 1, d6))


def _route(logits):
    lane = lax.broadcasted_iota(jnp.int32, logits.shape, 1)
    neg_inf = -jnp.inf
    far = LANES - 1
    gl = jnp.where(lane < N_GROUPS, logits, neg_inf)
    gmax = jnp.max(gl, axis=-1, keepdims=True)
    gsel = jnp.min(jnp.where(gl == gmax, lane, far), axis=-1, keepdims=True)
    gsum = jnp.sum(jnp.where(lane < N_GROUPS, jnp.exp(logits - gmax), 0.0), axis=-1, keepdims=True)
    g_w = 1.0 / gsum
    first = N_GROUPS + EXPERTS_PER_GROUP * gsel
    el = jnp.where((lane >= first) & (lane < first + EXPERTS_PER_GROUP), logits, neg_inf)
    v1 = jnp.max(el, axis=-1, keepdims=True)
    i1 = jnp.min(jnp.where(el == v1, lane, far), axis=-1, keepdims=True)
    el2 = jnp.where(lane == i1, neg_inf, el)
    v2 = jnp.max(el2, axis=-1, keepdims=True)
    i2 = jnp.min(jnp.where(el2 == v2, lane, far), axis=-1, keepdims=True)
    t = jnp.exp(v2 - v1)
    w1 = g_w / (1.0 + t)
    w2 = g_w * t / (1.0 + t)
    first_lower = i1 < i2
    a = jnp.minimum(i1, i2) - first
    b = jnp.maximum(i1, i2) - first
    pair = ((a * (2 * EXPERTS_PER_GROUP - 1 - a)) >> 1) + (b - a - 1)
    bucket = gsel * PAIRS_PER_GROUP + pair
    w_lo = jnp.where(first_lower, w1, w2)
    w_hi = jnp.where(first_lower, w2, w1)
    return bucket.astype(F32), w_lo, w_hi


def _sublayer_epilogue(x, y, cond_ref, lng_ref, lnb_ref, wr_ref, br_ref, x_out_ref, hx_ref, meta_t_ref):
    g_m = cond_ref[2:3, :]
    sh_f = cond_ref[3:4, :]
    sc_f = cond_ref[4:5, :]
    x1 = _layer_norm(DN_ALPHA * x + (1.0 + g_m) * y, lng_ref[...], lnb_ref[...])
    x_out_ref[...] = x1
    h2 = x1 * (1.0 + sc_f) + sh_f
    h_hi = h2.astype(BF16)
    h_lo = (h2 - h_hi.astype(F32)).astype(BF16)
    both = (jnp.dot(h_hi, wr_ref[...], preferred_element_type=F32)
            + jnp.dot(h_lo, wr_ref[...], preferred_element_type=F32))
    logits = both + pltpu.roll(both, LANES - ROUTER_LO_LANE, axis=1) + br_ref[...]
    bucket, w_lo, w_hi = _route(logits)
    lane = lax.broadcasted_iota(jnp.int32, logits.shape, 1)
    meta = jnp.where(lane == 0, bucket, jnp.where(lane == 1, w_lo, jnp.where(lane == 2, w_hi, 0.0)))
    hx_ref[:, :D_MODEL] = h2
    hx_ref[:, D_MODEL:] = meta
    meta_t_ref[...] = jnp.transpose(meta)[:SUBLANES, :]


def _epilogue_out(n):
    shapes = (jax.ShapeDtypeStruct((n, D_MODEL), F32),
              jax.ShapeDtypeStruct((n, HX_W), F32),
              jax.ShapeDtypeStruct((SUBLANES, n), F32))
    return shapes


def _gelu(x):
    return 0.5 * x * (1.0 + lax.erf(x * (2.0 ** -0.5)))


def _even_kernel(x_ref, cond_ref, w_in_ref, sgu_g_ref, sgu_b_ref, w_s_ref, bs_ref, w_dw_ref, b_dw_ref,
                 cg_ref, cb_ref, w_out_ref, lng_ref, lnb_ref, wr_ref, br_ref,
                 x_out_ref, hx_ref, meta_t_ref, hist_ref, shift_ref):
    rows = x_ref.shape[0]
    t = pl.program_id(1)

    @pl.when(t == 0)
    def _():
        hist_ref[0:CONV_HALO, :] = jnp.zeros((CONV_HALO, B_WIDTH), F32)

    x = x_ref[...]
    h = x * (1.0 + cond_ref[1:2, :]) + cond_ref[0:1, :]
    z = _bdot(h, w_in_ref[...])

    u = _gelu(z[:, :A_WIDTH])
    v = _layer_norm(_gelu(z[:, A_WIDTH:2 * A_WIDTH]), sgu_g_ref[...], sgu_b_ref[...])
    r_i = lax.broadcasted_iota(jnp.int32, (A_CHUNK, A_CHUNK), 0)
    c_i = lax.broadcasted_iota(jnp.int32, (A_CHUNK, A_CHUNK), 1)
    gw = A_WIDTH // A_GROUPS
    s_rows = []
    for ck in range(rows // A_CHUNK):
        s_cols = []
        for g in range(A_GROUPS):
            w_g = jnp.where(c_i <= r_i, w_s_ref[g], 0.0)
            s_cols.append(_bdot(w_g, v[ck * A_CHUNK:(ck + 1) * A_CHUNK, g * gw:(g + 1) * gw]))
        s_rows.append(jnp.concatenate(s_cols, axis=1) + bs_ref[...])
    y_a = u * jnp.concatenate(s_rows, axis=0)

    zb = z[:, 2 * A_WIDTH:]
    gate = zb[:, B_WIDTH:]
    a = zb[:, :B_WIDTH] * (1.0 / (1.0 + jnp.exp(-gate)))
    hist_ref[CONV_HALO:CONV_HALO + rows, :] = a
    first = CONV_HALO - (CONV_WIDTH - 1)
    acc = jnp.zeros((rows, B_WIDTH), F32) + b_dw_ref[...]
    for r in range(SUBLANES):
        taps = [k for k in range(CONV_WIDTH) if (first + k) % SUBLANES == r]
        if not taps:
            continue
        span = max(first + k for k in taps) - r + rows
        if r == 0:
            shifted = hist_ref
        else:
            shifted = shift_ref.at[r - 1]
            shifted[0:span, :] = hist_ref[r:r + span, :]
        for k in taps:
            lo = first + k - r
            acc = acc + w_dw_ref[k:k + 1, :] * shifted[lo:lo + rows, :]
    hist_ref[0:CONV_HALO, :] = hist_ref[rows:rows + CONV_HALO, :]
    yb = _layer_norm(acc, cg_ref[...], cb_ref[...])
    y_b = yb * (1.0 / (1.0 + jnp.exp(-yb)))

    y = _bdot(jnp.concatenate([y_a, y_b], axis=1), w_out_ref[...])
    _sublayer_epilogue(x, y, cond_ref, lng_ref, lnb_ref, wr_ref, br_ref, x_out_ref, hx_ref, meta_t_ref)


def _full(shape):
    nd = len(shape)
    return pl.BlockSpec(shape, lambda *_: (0,) * nd)


def _even_call(x2d, cond, bsz, seq, w_in, sgu_g, sgu_b, w_s, bs_full, w_dw, b_dw, cg, cb, w_out,
               lng, lnb, wr, br):
    n = bsz * seq
    n_t = seq // ROW_TILE
    row = lambda b, t: (b * n_t + t, 0)
    small = [sgu_g, sgu_b, w_s, bs_full, w_dw, b_dw, cg, cb, w_out, lng, lnb, wr, br]
    return pl.pallas_call(
        _even_kernel,
        grid=(bsz, n_t),
        in_specs=[pl.BlockSpec((ROW_TILE, D_MODEL), row),
                  pl.BlockSpec((None, 6, D_MODEL), lambda b, t: (b, 0, 0)),
                  _full(w_in.shape)] + [_full(a.shape) for a in small],
        out_specs=(pl.BlockSpec((ROW_TILE, D_MODEL), row),
                   pl.BlockSpec((ROW_TILE, HX_W), row),
                   pl.BlockSpec((SUBLANES, ROW_TILE), lambda b, t: (0, b * n_t + t))),
        out_shape=_epilogue_out(n),
        scratch_shapes=[pltpu.VMEM((CONV_HALO + ROW_TILE, B_WIDTH), F32),
                        pltpu.VMEM((SUBLANES - 1, CONV_HALO + ROW_TILE, B_WIDTH), F32)],
        compiler_params=_params("arbitrary", "arbitrary"),
        name="even_mixer",
    )(x2d, cond, w_in, *small)


RANK_SUB = 256
RANK_ROWS = 2048


def _rank_kernel(meta_t_ref, pos_ref, counts_ref, run_ref):
    phase = pl.program_id(0)
    i = pl.program_id(1)
    rows = meta_t_ref.shape[1]

    @pl.when((phase == 0) & (i == 0))
    def _():
        run_ref[...] = jnp.zeros_like(run_ref)

    @pl.when((phase == 1) & (i == 0))
    def _():
        counts = run_ref[...]
        counts_ref[...] = counts
        padded = jnp.floor((counts + (MOE_TILE - 1)) * (1.0 / MOE_TILE)) * MOE_TILE
        sub_l = lax.broadcasted_iota(jnp.int32, counts.shape, 0)
        off = jnp.zeros_like(counts)
        for b in range(N_BUCKETS):
            off = off + jnp.where(sub_l > b, padded[b:b + 1, :], 0.0)
        run_ref[...] = off

    r_i = lax.broadcasted_iota(jnp.int32, (RANK_SUB, RANK_SUB), 0)
    c_i = lax.broadcasted_iota(jnp.int32, (RANK_SUB, RANK_SUB), 1)
    tri = jnp.where(r_i <= c_i, 1.0, 0.0).astype(BF16)
    sub = lax.broadcasted_iota(jnp.int32, (BUCKET_ROWS, RANK_SUB), 0)
    run = run_ref[...]
    for t in range(rows // RANK_SUB):
        bucket = meta_t_ref[0:1, t * RANK_SUB:(t + 1) * RANK_SUB].astype(jnp.int32)
        onehot = jnp.where(sub == bucket, 1.0, 0.0)
        cum = jnp.dot(onehot.astype(BF16), tri, preferred_element_type=F32)
        pos = jnp.sum(onehot * (cum + run[:, 0:1]), axis=0, keepdims=True) - 1.0
        pos_ref[:, t * RANK_SUB:(t + 1) * RANK_SUB] = pos.astype(jnp.int32)
        run = run + cum[:, RANK_SUB - 1:RANK_SUB]
    run_ref[...] = run


def _rank_call(meta_t):
    n = meta_t.shape[1]
    n_t = n // RANK_ROWS
    return pl.pallas_call(
        _rank_kernel,
        grid=(2, n_t),
        in_specs=[pl.BlockSpec((SUBLANES, RANK_ROWS), lambda p, i: (0, i))],
        out_specs=(pl.BlockSpec((1, RANK_ROWS), lambda p, i: (0, i * p)),
                   pl.BlockSpec((BUCKET_ROWS, LANES), lambda p, i: (0, 0))),
        out_shape=(jax.ShapeDtypeStruct((1, n), jnp.int32),
                   jax.ShapeDtypeStruct((BUCKET_ROWS, LANES), F32)),
        scratch_shapes=[pltpu.VMEM((BUCKET_ROWS, LANES), F32)],
        compiler_params=_params("arbitrary", "arbitrary"),
        name="moe_rank",
    )(meta_t)


DISPATCH_ROWS = 1024


def _dispatch_kernel(pos_ref, hx_ref, xs_in_ref, xs_ref, sem):
    del xs_in_ref
    rows = hx_ref.shape[0]
    base = pl.program_id(0) * rows

    for r in range(rows):
        pltpu.make_async_copy(hx_ref.at[pl.ds(r, 1), :], xs_ref.at[pl.ds(pos_ref[base + r], 1), :],
                              sem).start()
    pltpu.make_async_copy(hx_ref, xs_ref.at[pl.ds(0, rows), :], sem).wait()


def _dispatch_call(pos, hx, n_sorted):
    n = hx.shape[0]
    xs0 = jnp.zeros((n_sorted, HX_W), F32)
    return pl.pallas_call(
        _dispatch_kernel,
        grid_spec=pltpu.PrefetchScalarGridSpec(
            num_scalar_prefetch=1,
            grid=(n // DISPATCH_ROWS,),
            in_specs=[pl.BlockSpec((DISPATCH_ROWS, HX_W), lambda i, p: (i, 0)),
                      pl.BlockSpec(memory_space=pl.ANY)],
            out_specs=pl.BlockSpec(memory_space=pl.ANY),
            scratch_shapes=[pltpu.SemaphoreType.DMA(())]),
        out_shape=jax.ShapeDtypeStruct((n_sorted, HX_W), F32),
        input_output_aliases={2: 0},
        compiler_params=_params("arbitrary"),
        name="moe_dispatch",
    )(pos, hx, xs0)


def _moe_kernel(lo_ref, hi_ref, nact_ref, xs_ref, wgu_lo_ref, wd_lo_ref, wgu_hi_ref, wd_hi_ref, ys_ref):
    del lo_ref, hi_ref

    @pl.when(pl.program_id(0) < nact_ref[0])
    def _():
        x = xs_ref[:, :D_MODEL].astype(BF16)
        w_lo = xs_ref[:, D_MODEL + 1:D_MODEL + 2]
        w_hi = xs_ref[:, D_MODEL + 2:D_MODEL + 3]

        def expert(wgu_ref, wd_ref):
            gu = jnp.dot(x, wgu_ref[...], preferred_element_type=F32)
            gate = gu[:, :D_EXPERT]
            hid = gate * (1.0 / (1.0 + jnp.exp(-gate))) * gu[:, D_EXPERT:]
            return jnp.dot(hid.astype(BF16), wd_ref[...], preferred_element_type=F32)

        ys_ref[...] = w_lo * expert(wgu_lo_ref, wd_lo_ref) + w_hi * expert(wgu_hi_ref, wd_hi_ref)

    @pl.when(pl.program_id(0) >= nact_ref[0])
    def _():
        ys_ref[...] = jnp.zeros_like(ys_ref)


def _moe_call(tile_lo, tile_hi, nact, xs, wgu, wd):
    n_sorted = xs.shape[0]
    n_tiles = n_sorted // MOE_TILE

    def tile(t, lo, hi, na):
        return (jnp.minimum(t, na[0] - 1), 0)

    def w_lo(t, lo, hi, na):
        return (lo[jnp.minimum(t, na[0] - 1)], 0, 0)

    def w_hi(t, lo, hi, na):
        return (hi[jnp.minimum(t, na[0] - 1)], 0, 0)

    return pl.pallas_call(
        _moe_kernel,
        grid_spec=pltpu.PrefetchScalarGridSpec(
            num_scalar_prefetch=3,
            grid=(n_tiles,),
            in_specs=[pl.BlockSpec((MOE_TILE, HX_W), tile),
                      pl.BlockSpec((None, D_MODEL, 2 * D_EXPERT), w_lo),
                      pl.BlockSpec((None, D_EXPERT, D_MODEL), w_lo),
                      pl.BlockSpec((None, D_MODEL, 2 * D_EXPERT), w_hi),
                      pl.BlockSpec((None, D_EXPERT, D_MODEL), w_hi)],
            out_specs=pl.BlockSpec((MOE_TILE, D_MODEL), lambda t, lo, hi, na: (t, 0))),
        out_shape=jax.ShapeDtypeStruct((n_sorted, D_MODEL), F32),
        compiler_params=_params("arbitrary"),
        name="moe_experts",
    )(tile_lo, tile_hi, nact, xs, wgu, wd, wgu, wd)


def _combine_kernel(pos_ref, x_ref, cond_ref, lng_ref, lnb_ref, ys_ref, o_ref, ybuf, sems):
    i = pl.program_id(0)
    n_steps = pl.num_programs(0)
    rows = x_ref.shape[0]

    def row_copy(step, slot, r):
        src = ys_ref.at[pl.ds(pos_ref[step * rows + r], 1), :]
        return pltpu.make_async_copy(src, ybuf.at[slot, pl.ds(r, 1), :], sems.at[slot])

    def issue(step, slot):
        for r in range(rows):
            row_copy(step, slot, r).start()

    @pl.when(i == 0)
    def _():
        issue(0, 0)

    @pl.when(i + 1 < n_steps)
    def _():
        issue(i + 1, (i + 1) % 2)

    slot = i % 2

    pltpu.make_async_copy(ys_ref.at[pl.ds(0, rows), :], ybuf.at[slot], sems.at[slot]).wait()

    y = ybuf[slot]
    g_f = cond_ref[5:6, :]
    o_ref[...] = _layer_norm(DN_ALPHA * x_ref[...] + (1.0 + g_f) * y, lng_ref[...], lnb_ref[...])


def _combine_call(pos, x1, cond, bsz, seq, lng, lnb, ys):
    n = x1.shape[0]
    n_t = seq // ROW_TILE
    return pl.pallas_call(
        _combine_kernel,
        grid_spec=pltpu.PrefetchScalarGridSpec(
            num_scalar_prefetch=1,
            grid=(n // ROW_TILE,),
            in_specs=[pl.BlockSpec((ROW_TILE, D_MODEL), lambda i, p: (i, 0)),
                      pl.BlockSpec((None, 6, D_MODEL), lambda i, p: (i // n_t, 0, 0)),
                      pl.BlockSpec((1, D_MODEL), lambda i, p: (0, 0)),
                      pl.BlockSpec((1, D_MODEL), lambda i, p: (0, 0)),
                      pl.BlockSpec(memory_space=pl.ANY)],
            out_specs=pl.BlockSpec((ROW_TILE, D_MODEL), lambda i, p: (i, 0)),
            scratch_shapes=[pltpu.VMEM((2, ROW_TILE, D_MODEL), F32),
                            pltpu.SemaphoreType.DMA((2,))]),
        out_shape=jax.ShapeDtypeStruct((n, D_MODEL), F32),
        compiler_params=_params("arbitrary"),
        name="moe_combine",
    )(pos, x1, cond, lng, lnb, ys)


def _moe_sublayer(x1, hx, meta_t, cond, bsz, seq, lng, lnb, w_gate, w_up, w_down):
    n = x1.shape[0]
    n_sorted = n + N_BUCKETS * MOE_TILE
    pos2d, counts = _rank_call(meta_t)
    pos = pos2d.reshape(n)
    cnt = counts[:N_BUCKETS, 0].astype(jnp.int32)
    padded = ((cnt + MOE_TILE - 1) // MOE_TILE) * MOE_TILE
    ends = jnp.cumsum(padded)
    starts = jnp.arange(n_sorted // MOE_TILE, dtype=jnp.int32) * MOE_TILE
    tile_bucket = jnp.minimum(jnp.sum((ends[None, :] <= starts[:, None]).astype(jnp.int32), axis=1),
                              N_BUCKETS - 1)
    tile_lo = jnp.asarray(_BUCKET_LO)[tile_bucket]
    tile_hi = jnp.asarray(_BUCKET_HI)[tile_bucket]
    nact = (ends[-1:] // MOE_TILE).astype(jnp.int32)
    xs = _dispatch_call(pos, hx, n_sorted)
    wgu = jnp.concatenate([w_gate, w_up], axis=-1).astype(BF16)
    ys = _moe_call(tile_lo, tile_hi, nact, xs, wgu, w_down.astype(BF16))
    return _combine_call(pos, x1, cond, bsz, seq, lng, lnb, ys)


def _qkv_kernel(x_ref, cond_ref, wq_t_ref, wk_ref, wv_t_ref, q_t_ref, k_ref, v_t_ref, kmean_ref):
    h = (x_ref[...] * (1.0 + cond_ref[1:2, :]) + cond_ref[0:1, :]).astype(BF16)
    nt = (((1,), (1,)), ((), ()))
    q_t_ref[...] = lax.dot_general(wq_t_ref[...], h, nt, preferred_element_type=F32).astype(BF16)
    v_t_ref[...] = lax.dot_general(wv_t_ref[...], h, nt, preferred_element_type=F32).astype(BF16)
    k = jnp.dot(h, wk_ref[...], preferred_element_type=F32)
    k_ref[...] = k.astype(BF16)
    kmean_ref[...] = jnp.broadcast_to(jnp.mean(k, axis=0, keepdims=True), kmean_ref.shape)


def _qkv_call(x2d, cond, bsz, seq, wq_t, wk, wv_t):
    n_t = seq // MOBA_BLOCK
    hd = N_HEADS * HEAD_DIM
    return pl.pallas_call(
        _qkv_kernel,
        grid=(bsz, n_t),
        in_specs=[pl.BlockSpec((MOBA_BLOCK, D_MODEL), lambda b, t: (b * n_t + t, 0)),
                  pl.BlockSpec((None, 6, D_MODEL), lambda b, t: (b, 0, 0)),
                  _full(wq_t.shape), _full(wk.shape), _full(wv_t.shape)],
        out_specs=(pl.BlockSpec((None, hd, MOBA_BLOCK), lambda b, t: (b, 0, t)),
                   pl.BlockSpec((None, MOBA_BLOCK, hd), lambda b, t: (b, t, 0)),
                   pl.BlockSpec((None, hd, MOBA_BLOCK), lambda b, t: (b, 0, t)),
                   pl.BlockSpec((None, None, SUBLANES, hd), lambda b, t: (b, t, 0, 0))),
        out_shape=(jax.ShapeDtypeStruct((bsz, hd, seq), BF16),
                   jax.ShapeDtypeStruct((bsz, seq, hd), BF16),
                   jax.ShapeDtypeStruct((bsz, hd, seq), BF16),
                   jax.ShapeDtypeStruct((bsz, n_t, SUBLANES, hd), F32)),
        compiler_params=_params("arbitrary", "arbitrary"),
        name="qkv_proj",
    )(x2d, cond, wq_t, wk, wv_t)


def _moba_kernel(q_t_ref, k_ref, v_t_ref, kmean_ref, o_ref):
    n_blocks = kmean_ref.shape[0]
    blk = MOBA_BLOCK
    c = HEAD_DIM ** -0.5 * LOG2E
    km = kmean_ref[...]
    km_hi = km.astype(BF16)
    km_lo = (km - km_hi.astype(F32)).astype(BF16)
    causal = (lax.broadcasted_iota(jnp.int32, (blk, blk), 0)
              <= lax.broadcasted_iota(jnp.int32, (blk, blk), 1))
    bidx = lax.broadcasted_iota(jnp.int32, (n_blocks, blk), 0)
    for i in range(n_blocks):
        q_t = q_t_ref[:, i * blk:(i + 1) * blk]
        n_keys = (i + 1) * blk
        s = jnp.dot(k_ref[0:n_keys, :], q_t, preferred_element_type=F32)
        s_own = jnp.where(causal, s[i * blk:, :], NEG_BIG)
        m = jnp.max(s_own, axis=0, keepdims=True)
        sel = []
        if i > 0:
            gate = (jnp.dot(km_hi, q_t, preferred_element_type=F32)
                    + jnp.dot(km_lo, q_t, preferred_element_type=F32))
            gate = jnp.where(bidx < i, gate, -jnp.inf)
            rank = jnp.zeros(gate.shape, F32)
            for j in range(i):
                g_j = gate[j:j + 1, :]
                beats = (g_j > gate) | ((g_j == gate) & (j < bidx))
                rank = rank + jnp.where(beats, 1.0, 0.0)
            chosen = jnp.where((bidx < i) & (rank < MOBA_TOPK), 1.0, 0.0)
            for j in range(i):
                sel_j = chosen[j:j + 1, :] > 0.5
                sel.append(sel_j)
                bm = jnp.max(s[j * blk:(j + 1) * blk, :], axis=0, keepdims=True)
                m = jnp.maximum(m, jnp.where(sel_j, bm, NEG_BIG))
        p_own = jnp.exp2((s_own - m) * c)
        l = jnp.sum(p_own, axis=0, keepdims=True)
        parts = []
        for j in range(i):
            shift = jnp.where(sel[j], m, -NEG_BIG)
            p_j = jnp.exp2((s[j * blk:(j + 1) * blk, :] - shift) * c)
            l = l + jnp.sum(p_j, axis=0, keepdims=True)
            parts.append(p_j.astype(BF16))
        parts.append(p_own.astype(BF16))
        p = jnp.concatenate(parts, axis=0) if i > 0 else parts[0]
        acc = jnp.dot(v_t_ref[:, 0:n_keys], p, preferred_element_type=F32)
        o_ref[i * blk:(i + 1) * blk, :] = jnp.transpose(acc * (1.0 / l)).astype(BF16)


def _moba_call(q_t, k, v_t, kmean):
    bsz, hd, seq = q_t.shape
    n_b = seq // MOBA_BLOCK
    return pl.pallas_call(
        _moba_kernel,
        grid=(bsz, N_HEADS),
        in_specs=[pl.BlockSpec((None, HEAD_DIM, seq), lambda b, h: (b, h, 0)),
                  pl.BlockSpec((None, seq, HEAD_DIM), lambda b, h: (b, 0, h)),
                  pl.BlockSpec((None, HEAD_DIM, seq), lambda b, h: (b, h, 0)),
                  pl.BlockSpec((None, n_b, HEAD_DIM), lambda b, h: (b, 0, h))],
        out_specs=pl.BlockSpec((None, seq, HEAD_DIM), lambda b, h: (b, 0, h)),
        out_shape=jax.ShapeDtypeStruct((bsz, seq, hd), BF16),
        compiler_params=_params("arbitrary", "arbitrary"),
        name="moba_attention",
    )(q_t, k, v_t, kmean)


def _attn_out_kernel(x_ref, o_ref, cond_ref, w_out_ref, lng_ref, lnb_ref, wr_ref, br_ref,
                     x_out_ref, hx_ref, meta_t_ref):
    y = jnp.dot(o_ref[...], w_out_ref[...], preferred_element_type=F32)
    _sublayer_epilogue(x_ref[...], y, cond_ref, lng_ref, lnb_ref, wr_ref, br_ref, x_out_ref, hx_ref, meta_t_ref)


def _attn_out_call(x2d, o2d, cond, seq, w_out, lng, lnb, wr, br):
    n = x2d.shape[0]
    n_t = seq // ROW_TILE
    row = lambda i: (i, 0)
    small = [w_out, lng, lnb, wr, br]
    return pl.pallas_call(
        _attn_out_kernel,
        grid=(n // ROW_TILE,),
        in_specs=[pl.BlockSpec((ROW_TILE, D_MODEL), row),
                  pl.BlockSpec((ROW_TILE, D_MODEL), row),
                  pl.BlockSpec((None, 6, D_MODEL), lambda i: (i // n_t, 0, 0))]
                 + [_full(a.shape) for a in small],
        out_specs=(pl.BlockSpec((ROW_TILE, D_MODEL), row),
                   pl.BlockSpec((ROW_TILE, HX_W), row),
                   pl.BlockSpec((SUBLANES, ROW_TILE), lambda i: (0, i))),
        out_shape=_epilogue_out(n),
        compiler_params=_params("arbitrary"),
        name="attn_out",
    )(x2d, o2d, cond, *small)


def _router_weights(w_grp, b_grp, w_er, b_er):
    d = w_grp.shape[0]
    w_exp = jnp.transpose(w_er, (1, 0, 2)).reshape(d, N_EXPERTS)
    pad = LANES - N_GROUPS - N_EXPERTS
    w_cat = jnp.concatenate([w_grp, w_exp, jnp.zeros((d, ROUTER_LO_LANE - N_GROUPS - N_EXPERTS), F32)], axis=1)
    w_hi = w_cat.astype(BF16)
    w_lo = (w_cat - w_hi.astype(F32)).astype(BF16)
    wr = jnp.concatenate([w_hi, w_lo, jnp.zeros((d, LANES - 2 * ROUTER_LO_LANE), BF16)], axis=1)
    br = jnp.concatenate([b_grp, b_er.reshape(N_EXPERTS), jnp.zeros((pad,), F32)]).reshape(1, LANES)
    return wr, br


def kernel(x, c, ada_w, ada_b, ln_mix_g, ln_mix_b, ln_ffn_g, ln_ffn_b, ev_w_in, ev_sgu_ln_g, ev_sgu_ln_b,
           ev_w_s, ev_b_s, ev_w_dw, ev_b_dw, ev_conv_ln_g, ev_conv_ln_b, ev_w_out, od_w_qkv, od_w_out,
           moe_w_grp, moe_b_grp, moe_w_er, moe_b_er, moe_w_gate, moe_w_up, moe_w_down):
    bsz, seq, d = x.shape
    n = bsz * seq
    depth = ada_w.shape[0]
    cond_all = _cond_call(c, ada_w, ada_b).reshape(depth, bsz, 6, d)
    row = lambda v: v.reshape(1, -1)
    xc = x.reshape(n, d)
    for i in range(depth):
        cond = cond_all[i]
        j = i // 2
        wr, br = _router_weights(moe_w_grp[i], moe_b_grp[i], moe_w_er[i], moe_b_er[i])
        lng, lnb = row(ln_mix_g[i]), row(ln_mix_b[i])
        if i % 2 == 0:
            gw = A_WIDTH // A_GROUPS
            bs_full = jnp.repeat(ev_b_s[j].T, gw, axis=1)
            w_dw = jnp.concatenate([ev_w_dw[j], jnp.zeros((1, B_WIDTH), F32)], axis=0)
            x1, hx, meta_t = _even_call(
                xc, cond, bsz, seq, ev_w_in[j].astype(BF16), row(ev_sgu_ln_g[j]), row(ev_sgu_ln_b[j]),
                ev_w_s[j], bs_full, w_dw, row(ev_b_dw[j]), row(ev_conv_ln_g[j]), row(ev_conv_ln_b[j]),
                ev_w_out[j].astype(BF16), lng, lnb, wr, br)
        else:
            hd = N_HEADS * HEAD_DIM
            w_qkv = od_w_qkv[j].astype(BF16)
            q_t, k, v_t, kmean8 = _qkv_call(xc, cond, bsz, seq, w_qkv[:, :hd].T, w_qkv[:, hd:2 * hd],
                                            w_qkv[:, 2 * hd:].T)
            o = _moba_call(q_t, k, v_t, kmean8[:, :, 0, :])
            x1, hx, meta_t = _attn_out_call(xc, o.reshape(n, hd), cond, seq, od_w_out[j].astype(BF16),
                                            lng, lnb, wr, br)
        xc = _moe_sublayer(x1, hx, meta_t, cond, bsz, seq, row(ln_ffn_g[i]), row(ln_ffn_b[i]),
                           moe_w_gate[i], moe_w_up[i], moe_w_down[i])
    return xc.reshape(bsz, seq, d)
```

```python
import functools

import jax
import jax.numpy as jnp
import numpy as np
from jax import lax
from jax.experimental import pallas as pl
from jax.experimental.pallas import tpu as pltpu

D_MODEL = 1024
A_WIDTH = 512
A_GROUPS = 4
A_CHUNK = 128
B_WIDTH = 512
CONV_WIDTH = 31
N_HEADS = 8
HEAD_DIM = D_MODEL // N_HEADS
MOBA_BLOCK = 256
MOBA_TOPK = 3
N_GROUPS = 4
EXPERTS_PER_GROUP = 4
N_EXPERTS = N_GROUPS * EXPERTS_PER_GROUP
D_EXPERT = 512
DEPTH = 2
DN_ALPHA = (2.0 * DEPTH) ** 0.25
LN_EPS = 1e-5

LANES = 128
SUBLANES = 8
ROW_TILE = 256
MOE_TILE = 256
CONV_HALO = 32
PAIRS_PER_GROUP = EXPERTS_PER_GROUP * (EXPERTS_PER_GROUP - 1) // 2
N_BUCKETS = N_GROUPS * PAIRS_PER_GROUP
BUCKET_ROWS = 32
ROUTER_LO_LANE = 32
META_W = LANES
HX_W = D_MODEL + META_W
VMEM_LIMIT = 56 * 1024 * 1024
NEG_BIG = -1e30
LOG2E = 1.4426950408889634
F32 = jnp.float32
BF16 = jnp.bfloat16

_PAIR_LO = np.array([0, 0, 0, 1, 1, 2], np.int32)
_PAIR_HI = np.array([1, 2, 3, 2, 3, 3], np.int32)
_BUCKET_LO = np.concatenate([g * EXPERTS_PER_GROUP + _PAIR_LO for g in range(N_GROUPS)])
_BUCKET_HI = np.concatenate([g * EXPERTS_PER_GROUP + _PAIR_HI for g in range(N_GROUPS)])


def _params(*sem):
    return pltpu.CompilerParams(dimension_semantics=sem, vmem_limit_bytes=VMEM_LIMIT)


def _layer_norm(x, g, b):
    mu = jnp.mean(x, axis=-1, keepdims=True)
    xc = x - mu
    var = jnp.mean(xc * xc, axis=-1, keepdims=True)
    return xc * lax.rsqrt(var + LN_EPS) * g + b


def _bdot(a, b):
    return jnp.dot(a.astype(BF16), b.astype(BF16), preferred_element_type=F32)


def _cond_kernel(c_ref, w_ref, b_ref, o_ref):
    c = c_ref[...]
    c_act = c * (1.0 / (1.0 + jnp.exp(-c)))
    o_ref[...] = jnp.dot(c_act, w_ref[...], precision=lax.Precision.HIGHEST,
                         preferred_element_type=F32) + b_ref[...]


def _cond_call(c, ada_w, ada_b):
    depth, d, d6 = ada_w.shape
    bsz = c.shape[0]
    n_col = d6 // d
    return pl.pallas_call(
        _cond_kernel,
        grid=(depth, n_col),
        in_specs=[pl.BlockSpec((bsz, d), lambda i, j: (0, 0)),
                  pl.BlockSpec((None, d, d), lambda i, j: (i, 0, j)),
                  pl.BlockSpec((None, 1, d), lambda i, j: (i, 0, j))],
        out_specs=pl.BlockSpec((None, bsz, d), lambda i, j: (i, 0, j)),
        out_shape=jax.ShapeDtypeStruct((depth, bsz, d6), F32),
        compiler_params=_params("arbitrary", "arbitrary"),
        name="cond",
    )(c, ada_w, ada_b.reshape(depth, 1, d6))


def _first_max(rows):
    best = functools.reduce(jnp.maximum, rows)
    idx = jnp.full(best.shape, len(rows) - 1, jnp.int32)
    for k in range(len(rows) - 2, -1, -1):
        idx = jnp.where(rows[k] == best, k, idx)
    return best, idx


def _route(lt):
    row = lambda k: lt[k:k + 1, :]
    g_rows = [row(k) for k in range(N_GROUPS)]
    gmax, gsel = _first_max(g_rows)
    g_w = 1.0 / functools.reduce(lambda a, b: a + b, [jnp.exp(g - gmax) for g in g_rows])
    e_rows = []
    for k in range(EXPERTS_PER_GROUP):
        e = row(N_GROUPS + (N_GROUPS - 1) * EXPERTS_PER_GROUP + k)
        for g in range(N_GROUPS - 2, -1, -1):
            e = jnp.where(gsel == g, row(N_GROUPS + g * EXPERTS_PER_GROUP + k), e)
        e_rows.append(e)
    v1, i1 = _first_max(e_rows)
    v2, i2 = _first_max([jnp.where(i1 == k, -jnp.inf, e) for k, e in enumerate(e_rows)])
    t = jnp.exp(v2 - v1)
    w1 = g_w / (1.0 + t)
    w2 = g_w * t / (1.0 + t)
    first_lower = i1 < i2
    a = jnp.minimum(i1, i2)
    b = jnp.maximum(i1, i2)
    pair = ((a * (2 * EXPERTS_PER_GROUP - 1 - a)) >> 1) + (b - a - 1)
    bucket = gsel * PAIRS_PER_GROUP + pair
    w_lo = jnp.where(first_lower, w1, w2)
    w_hi = jnp.where(first_lower, w2, w1)
    return bucket.astype(F32), w_lo, w_hi


def _sublayer_epilogue(x, y, cond_ref, lng_ref, lnb_ref, wr_ref, br_ref, x_out_ref, hx_ref, meta_t_ref):
    g_m = cond_ref[2:3, :]
    sh_f = cond_ref[3:4, :]
    sc_f = cond_ref[4:5, :]
    x1 = _layer_norm(DN_ALPHA * x + (1.0 + g_m) * y, lng_ref[...], lnb_ref[...])
    x_out_ref[...] = x1
    h2 = x1 * (1.0 + sc_f) + sh_f
    h_hi = h2.astype(BF16)
    h_lo = (h2 - h_hi.astype(F32)).astype(BF16)
    both = (jnp.dot(h_hi, wr_ref[...], preferred_element_type=F32)
            + jnp.dot(h_lo, wr_ref[...], preferred_element_type=F32))
    logits = both + pltpu.roll(both, LANES - ROUTER_LO_LANE, axis=1) + br_ref[...]
    bucket, w_lo, w_hi = _route(jnp.transpose(logits))
    rows = logits.shape[0]
    sub = lax.broadcasted_iota(jnp.int32, (SUBLANES, rows), 0)
    meta_t = jnp.where(sub == 0, bucket, jnp.where(sub == 1, w_lo, jnp.where(sub == 2, w_hi, 0.0)))
    meta_t_ref[...] = meta_t
    hx_ref[:, :D_MODEL] = h2
    hx_ref[:, D_MODEL:] = jnp.transpose(
        jnp.concatenate([meta_t, jnp.zeros((META_W - SUBLANES, rows), F32)], axis=0))


def _epilogue_out(n):
    shapes = (jax.ShapeDtypeStruct((n, D_MODEL), F32),
              jax.ShapeDtypeStruct((n, HX_W), F32),
              jax.ShapeDtypeStruct((SUBLANES, n), F32))
    return shapes


def _gelu(x):
    return 0.5 * x * (1.0 + lax.erf(x * (2.0 ** -0.5)))


def _even_kernel(x_ref, cond_ref, w_in_ref, sgu_g_ref, sgu_b_ref, w_s_ref, bs_ref, w_dw_ref, b_dw_ref,
                 cg_ref, cb_ref, w_out_ref, lng_ref, lnb_ref, wr_ref, br_ref,
                 x_out_ref, hx_ref, meta_t_ref, hist_ref, shift_ref):
    rows = x_ref.shape[0]
    t = pl.program_id(1)

    @pl.when(t == 0)
    def _():
        hist_ref[0:CONV_HALO, :] = jnp.zeros((CONV_HALO, B_WIDTH), F32)

    x = x_ref[...]
    h = x * (1.0 + cond_ref[1:2, :]) + cond_ref[0:1, :]
    z = _bdot(h, w_in_ref[...])

    u = _gelu(z[:, :A_WIDTH])
    v = _layer_norm(_gelu(z[:, A_WIDTH:2 * A_WIDTH]), sgu_g_ref[...], sgu_b_ref[...])
    r_i = lax.broadcasted_iota(jnp.int32, (A_CHUNK, A_CHUNK), 0)
    c_i = lax.broadcasted_iota(jnp.int32, (A_CHUNK, A_CHUNK), 1)
    gw = A_WIDTH // A_GROUPS
    s_rows = []
    for ck in range(rows // A_CHUNK):
        s_cols = []
        for g in range(A_GROUPS):
            w_g = jnp.where(c_i <= r_i, w_s_ref[g], 0.0)
            s_cols.append(_bdot(w_g, v[ck * A_CHUNK:(ck + 1) * A_CHUNK, g * gw:(g + 1) * gw]))
        s_rows.append(jnp.concatenate(s_cols, axis=1) + bs_ref[...])
    y_a = u * jnp.concatenate(s_rows, axis=0)

    zb = z[:, 2 * A_WIDTH:]
    gate = zb[:, B_WIDTH:]
    a = zb[:, :B_WIDTH] * (1.0 / (1.0 + jnp.exp(-gate)))
    hist_ref[CONV_HALO:CONV_HALO + rows, :] = a
    first = CONV_HALO - (CONV_WIDTH - 1)
    acc = jnp.zeros((rows, B_WIDTH), F32) + b_dw_ref[...]
    for r in range(SUBLANES):
        taps = [k for k in range(CONV_WIDTH) if (first + k) % SUBLANES == r]
        if not taps:
            continue
        span = max(first + k for k in taps) - r + rows
        if r == 0:
            shifted = hist_ref
        else:
            shifted = shift_ref.at[r - 1]
            shifted[0:span, :] = hist_ref[r:r + span, :]
        for k in taps:
            lo = first + k - r
            acc = acc + w_dw_ref[k:k + 1, :] * shifted[lo:lo + rows, :]
    hist_ref[0:CONV_HALO, :] = hist_ref[rows:rows + CONV_HALO, :]
    yb = _layer_norm(acc, cg_ref[...], cb_ref[...])
    y_b = yb * (1.0 / (1.0 + jnp.exp(-yb)))

    y = _bdot(jnp.concatenate([y_a, y_b], axis=1), w_out_ref[...])
    _sublayer_epilogue(x, y, cond_ref, lng_ref, lnb_ref, wr_ref, br_ref, x_out_ref, hx_ref, meta_t_ref)


def _full(shape):
    nd = len(shape)
    return pl.BlockSpec(shape, lambda *_: (0,) * nd)


def _even_call(x2d, cond, bsz, seq, w_in, sgu_g, sgu_b, w_s, bs_full, w_dw, b_dw, cg, cb, w_out,
               lng, lnb, wr, br):
    n = bsz * seq
    n_t = seq // ROW_TILE
    row = lambda b, t: (b * n_t + t, 0)
    small = [sgu_g, sgu_b, w_s, bs_full, w_dw, b_dw, cg, cb, w_out, lng, lnb, wr, br]
    return pl.pallas_call(
        _even_kernel,
        grid=(bsz, n_t),
        in_specs=[pl.BlockSpec((ROW_TILE, D_MODEL), row),
                  pl.BlockSpec((None, 6, D_MODEL), lambda b, t: (b, 0, 0)),
                  _full(w_in.shape)] + [_full(a.shape) for a in small],
        out_specs=(pl.BlockSpec((ROW_TILE, D_MODEL), row),
                   pl.BlockSpec((ROW_TILE, HX_W), row),
                   pl.BlockSpec((SUBLANES, ROW_TILE), lambda b, t: (0, b * n_t + t))),
        out_shape=_epilogue_out(n),
        scratch_shapes=[pltpu.VMEM((CONV_HALO + ROW_TILE, B_WIDTH), F32),
                        pltpu.VMEM((SUBLANES - 1, CONV_HALO + ROW_TILE, B_WIDTH), F32)],
        compiler_params=_params("arbitrary", "arbitrary"),
        name="even_mixer",
    )(x2d, cond, w_in, *small)


RANK_SUB = 256
RANK_ROWS = 2048


def _rank_kernel(meta_t_ref, pos_ref, counts_ref, run_ref):
    phase = pl.program_id(0)
    i = pl.program_id(1)
    rows = meta_t_ref.shape[1]

    @pl.when((phase == 0) & (i == 0))
    def _():
        run_ref[...] = jnp.zeros_like(run_ref)

    @pl.when((phase == 1) & (i == 0))
    def _():
        counts = run_ref[...]
        counts_ref[...] = counts
        padded = jnp.floor((counts + (MOE_TILE - 1)) * (1.0 / MOE_TILE)) * MOE_TILE
        sub_l = lax.broadcasted_iota(jnp.int32, counts.shape, 0)
        off = jnp.zeros_like(counts)
        for b in range(N_BUCKETS):
            off = off + jnp.where(sub_l > b, padded[b:b + 1, :], 0.0)
        run_ref[...] = off

    r_i = lax.broadcasted_iota(jnp.int32, (RANK_SUB, RANK_SUB), 0)
    c_i = lax.broadcasted_iota(jnp.int32, (RANK_SUB, RANK_SUB), 1)
    tri = jnp.where(r_i <= c_i, 1.0, 0.0).astype(BF16)
    sub = lax.broadcasted_iota(jnp.int32, (BUCKET_ROWS, RANK_SUB), 0)
    run = run_ref[...]
    for t in range(rows // RANK_SUB):
        bucket = meta_t_ref[0:1, t * RANK_SUB:(t + 1) * RANK_SUB].astype(jnp.int32)
        onehot = jnp.where(sub == bucket, 1.0, 0.0)
        cum = jnp.dot(onehot.astype(BF16), tri, preferred_element_type=F32)
        pos = jnp.sum(onehot * (cum + run[:, 0:1]), axis=0, keepdims=True) - 1.0
        pos_ref[:, t * RANK_SUB:(t + 1) * RANK_SUB] = pos.astype(jnp.int32)
        run = run + cum[:, RANK_SUB - 1:RANK_SUB]
    run_ref[...] = run


def _rank_call(meta_t):
    n = meta_t.shape[1]
    n_t = n // RANK_ROWS
    return pl.pallas_call(
        _rank_kernel,
        grid=(2, n_t),
        in_specs=[pl.BlockSpec((SUBLANES, RANK_ROWS), lambda p, i: (0, i))],
        out_specs=(pl.BlockSpec((1, RANK_ROWS), lambda p, i: (0, i * p)),
                   pl.BlockSpec((BUCKET_ROWS, LANES), lambda p, i: (0, 0))),
        out_shape=(jax.ShapeDtypeStruct((1, n), jnp.int32),
                   jax.ShapeDtypeStruct((BUCKET_ROWS, LANES), F32)),
        scratch_shapes=[pltpu.VMEM((BUCKET_ROWS, LANES), F32)],
        compiler_params=_params("arbitrary", "arbitrary"),
        name="moe_rank",
    )(meta_t)


DISPATCH_ROWS = 1024


def _dispatch_kernel(pos_ref, pad_start_ref, pad_count_ref, nact_ref, hx_ref, xs_ref, zero_ref, sem, zsem):
    rows = hx_ref.shape[0]
    step = pl.program_id(0)
    base = step * rows
    n_tiles = xs_ref.shape[0] // MOE_TILE

    @pl.when(step == 0)
    def _():
        zero_ref[...] = jnp.zeros_like(zero_ref)

        def pad_row(b, r):
            return pltpu.make_async_copy(zero_ref.at[pl.ds(0, 1), :],
                                         xs_ref.at[pl.ds(pad_start_ref[b] + r, 1), :], zsem)

        def tail_tile(t):
            start = pl.multiple_of(t * MOE_TILE, MOE_TILE)
            return pltpu.make_async_copy(zero_ref, xs_ref.at[pl.ds(start, MOE_TILE), :], zsem)

        def for_all(act):
            for b in range(N_BUCKETS):
                lax.fori_loop(0, pad_count_ref[b], lambda r, c, b=b: (act(pad_row(b, r)), c)[1], 0)
            lax.fori_loop(nact_ref[0], n_tiles, lambda t, c: (act(tail_tile(t)), c)[1], 0)

        for_all(lambda cp: cp.start())
        for_all(lambda cp: cp.wait())

    for r in range(rows):
        pltpu.make_async_copy(hx_ref.at[pl.ds(r, 1), :], xs_ref.at[pl.ds(pos_ref[base + r], 1), :],
                              sem).start()
    pltpu.make_async_copy(hx_ref, xs_ref.at[pl.ds(0, rows), :], sem).wait()


def _dispatch_call(pos, pad_start, pad_count, nact, hx, n_sorted):
    n = hx.shape[0]
    return pl.pallas_call(
        _dispatch_kernel,
        grid_spec=pltpu.PrefetchScalarGridSpec(
            num_scalar_prefetch=4,
            grid=(n // DISPATCH_ROWS,),
            in_specs=[pl.BlockSpec((DISPATCH_ROWS, HX_W), lambda i, *_: (i, 0))],
            out_specs=pl.BlockSpec(memory_space=pl.ANY),
            scratch_shapes=[pltpu.VMEM((MOE_TILE, HX_W), F32),
                            pltpu.SemaphoreType.DMA(()), pltpu.SemaphoreType.DMA(())]),
        out_shape=jax.ShapeDtypeStruct((n_sorted, HX_W), F32),
        compiler_params=_params("arbitrary"),
        name="moe_dispatch",
    )(pos, pad_start, pad_count, nact, hx)


def _moe_kernel(lo_ref, hi_ref, nact_ref, xs_ref, wgu_lo_ref, wd_lo_ref, wgu_hi_ref, wd_hi_ref, ys_ref):
    del lo_ref, hi_ref

    @pl.when(pl.program_id(0) < nact_ref[0])
    def _():
        x = xs_ref[:, :D_MODEL].astype(BF16)
        w_lo = xs_ref[:, D_MODEL + 1:D_MODEL + 2]
        w_hi = xs_ref[:, D_MODEL + 2:D_MODEL + 3]

        def expert(wgu_ref, wd_ref):
            gu = jnp.dot(x, wgu_ref[...], preferred_element_type=F32)
            gate = gu[:, :D_EXPERT]
            hid = gate * (1.0 / (1.0 + jnp.exp(-gate))) * gu[:, D_EXPERT:]
            return jnp.dot(hid.astype(BF16), wd_ref[...], preferred_element_type=F32)

        ys_ref[...] = w_lo * expert(wgu_lo_ref, wd_lo_ref) + w_hi * expert(wgu_hi_ref, wd_hi_ref)

    @pl.when(pl.program_id(0) >= nact_ref[0])
    def _():
        ys_ref[...] = jnp.zeros_like(ys_ref)


def _moe_call(tile_lo, tile_hi, nact, xs, wgu, wd):
    n_sorted = xs.shape[0]
    n_tiles = n_sorted // MOE_TILE

    def tile(t, lo, hi, na):
        return (jnp.minimum(t, na[0] - 1), 0)

    def w_lo(t, lo, hi, na):
        return (lo[jnp.minimum(t, na[0] - 1)], 0, 0)

    def w_hi(t, lo, hi, na):
        return (hi[jnp.minimum(t, na[0] - 1)], 0, 0)

    return pl.pallas_call(
        _moe_kernel,
        grid_spec=pltpu.PrefetchScalarGridSpec(
            num_scalar_prefetch=3,
            grid=(n_tiles,),
            in_specs=[pl.BlockSpec((MOE_TILE, HX_W), tile),
                      pl.BlockSpec((None, D_MODEL, 2 * D_EXPERT), w_lo),
                      pl.BlockSpec((None, D_EXPERT, D_MODEL), w_lo),
                      pl.BlockSpec((None, D_MODEL, 2 * D_EXPERT), w_hi),
                      pl.BlockSpec((None, D_EXPERT, D_MODEL), w_hi)],
            out_specs=pl.BlockSpec((MOE_TILE, D_MODEL), lambda t, lo, hi, na: (t, 0))),
        out_shape=jax.ShapeDtypeStruct((n_sorted, D_MODEL), F32),
        compiler_params=_params("arbitrary"),
        name="moe_experts",
    )(tile_lo, tile_hi, nact, xs, wgu, wd, wgu, wd)


def _combine_kernel(pos_ref, x_ref, cond_ref, lng_ref, lnb_ref, ys_ref, o_ref, ybuf, sems):
    i = pl.program_id(0)
    n_steps = pl.num_programs(0)
    rows = x_ref.shape[0]

    def row_copy(step, slot, r):
        src = ys_ref.at[pl.ds(pos_ref[step * rows + r], 1), :]
        return pltpu.make_async_copy(src, ybuf.at[slot, pl.ds(r, 1), :], sems.at[slot])

    def issue(step, slot):
        for r in range(rows):
            row_copy(step, slot, r).start()

    @pl.when(i == 0)
    def _():
        issue(0, 0)

    @pl.when(i + 1 < n_steps)
    def _():
        issue(i + 1, (i + 1) % 2)

    slot = i % 2

    pltpu.make_async_copy(ys_ref.at[pl.ds(0, rows), :], ybuf.at[slot], sems.at[slot]).wait()

    y = ybuf[slot]
    g_f = cond_ref[5:6, :]
    o_ref[...] = _layer_norm(DN_ALPHA * x_ref[...] + (1.0 + g_f) * y, lng_ref[...], lnb_ref[...])


def _combine_call(pos, x1, cond, bsz, seq, lng, lnb, ys):
    n = x1.shape[0]
    n_t = seq // ROW_TILE
    return pl.pallas_call(
        _combine_kernel,
        grid_spec=pltpu.PrefetchScalarGridSpec(
            num_scalar_prefetch=1,
            grid=(n // ROW_TILE,),
            in_specs=[pl.BlockSpec((ROW_TILE, D_MODEL), lambda i, p: (i, 0)),
                      pl.BlockSpec((None, 6, D_MODEL), lambda i, p: (i // n_t, 0, 0)),
                      pl.BlockSpec((1, D_MODEL), lambda i, p: (0, 0)),
                      pl.BlockSpec((1, D_MODEL), lambda i, p: (0, 0)),
                      pl.BlockSpec(memory_space=pl.ANY)],
            out_specs=pl.BlockSpec((ROW_TILE, D_MODEL), lambda i, p: (i, 0)),
            scratch_shapes=[pltpu.VMEM((2, ROW_TILE, D_MODEL), F32),
                            pltpu.SemaphoreType.DMA((2,))]),
        out_shape=jax.ShapeDtypeStruct((n, D_MODEL), F32),
        compiler_params=_params("arbitrary"),
        name="moe_combine",
    )(pos, x1, cond, lng, lnb, ys)


def _moe_sublayer(x1, hx, meta_t, cond, bsz, seq, lng, lnb, w_gate, w_up, w_down):
    n = x1.shape[0]
    n_sorted = n + N_BUCKETS * MOE_TILE
    pos2d, counts = _rank_call(meta_t)
    pos = pos2d.reshape(n)
    cnt = counts[:N_BUCKETS, 0].astype(jnp.int32)
    padded = ((cnt + MOE_TILE - 1) // MOE_TILE) * MOE_TILE
    ends = jnp.cumsum(padded)
    starts = jnp.arange(n_sorted // MOE_TILE, dtype=jnp.int32) * MOE_TILE
    tile_bucket = jnp.minimum(jnp.sum((ends[None, :] <= starts[:, None]).astype(jnp.int32), axis=1),
                              N_BUCKETS - 1)
    tile_lo = jnp.asarray(_BUCKET_LO)[tile_bucket]
    tile_hi = jnp.asarray(_BUCKET_HI)[tile_bucket]
    nact = (ends[-1:] // MOE_TILE).astype(jnp.int32)
    xs = _dispatch_call(pos, ends - padded + cnt, padded - cnt, nact, hx, n_sorted)
    wgu = jnp.concatenate([w_gate, w_up], axis=-1).astype(BF16)
    ys = _moe_call(tile_lo, tile_hi, nact, xs, wgu, w_down.astype(BF16))
    return _combine_call(pos, x1, cond, bsz, seq, lng, lnb, ys)


def _qkv_kernel(x_ref, cond_ref, wq_t_ref, wk_ref, wv_t_ref, q_t_ref, k_ref, v_t_ref, kmean_ref):
    h = (x_ref[...] * (1.0 + cond_ref[1:2, :]) + cond_ref[0:1, :]).astype(BF16)
    nt = (((1,), (1,)), ((), ()))
    q_t_ref[...] = lax.dot_general(wq_t_ref[...], h, nt, preferred_element_type=F32).astype(BF16)
    v_t_ref[...] = lax.dot_general(wv_t_ref[...], h, nt, preferred_element_type=F32).astype(BF16)
    k = jnp.dot(h, wk_ref[...], preferred_element_type=F32)
    k_ref[...] = k.astype(BF16)
    kmean_ref[...] = jnp.broadcast_to(jnp.mean(k, axis=0, keepdims=True), kmean_ref.shape)


def _qkv_call(x2d, cond, bsz, seq, wq_t, wk, wv_t):
    n_t = seq // MOBA_BLOCK
    hd = N_HEADS * HEAD_DIM
    return pl.pallas_call(
        _qkv_kernel,
        grid=(bsz, n_t),
        in_specs=[pl.BlockSpec((MOBA_BLOCK, D_MODEL), lambda b, t: (b * n_t + t, 0)),
                  pl.BlockSpec((None, 6, D_MODEL), lambda b, t: (b, 0, 0)),
                  _full(wq_t.shape), _full(wk.shape), _full(wv_t.shape)],
        out_specs=(pl.BlockSpec((None, hd, MOBA_BLOCK), lambda b, t: (b, 0, t)),
                   pl.BlockSpec((None, MOBA_BLOCK, hd), lambda b, t: (b, t, 0)),
                   pl.BlockSpec((None, hd, MOBA_BLOCK), lambda b, t: (b, 0, t)),
                   pl.BlockSpec((None, None, SUBLANES, hd), lambda b, t: (b, t, 0, 0))),
        out_shape=(jax.ShapeDtypeStruct((bsz, hd, seq), BF16),
                   jax.ShapeDtypeStruct((bsz, seq, hd), BF16),
                   jax.ShapeDtypeStruct((bsz, hd, seq), BF16),
                   jax.ShapeDtypeStruct((bsz, n_t, SUBLANES, hd), F32)),
        compiler_params=_params("arbitrary", "arbitrary"),
        name="qkv_proj",
    )(x2d, cond, wq_t, wk, wv_t)


def _moba_kernel(q_t_ref, k_ref, v_t_ref, kmean_ref, o_ref):
    n_blocks = kmean_ref.shape[0]
    blk = MOBA_BLOCK
    c = HEAD_DIM ** -0.5 * LOG2E
    km = kmean_ref[...]
    km_hi = km.astype(BF16)
    km_lo = (km - km_hi.astype(F32)).astype(BF16)
    causal = (lax.broadcasted_iota(jnp.int32, (blk, blk), 0)
              <= lax.broadcasted_iota(jnp.int32, (blk, blk), 1))
    bidx = lax.broadcasted_iota(jnp.int32, (n_blocks, blk), 0)

    def scores(i):
        q_t = q_t_ref[:, i * blk:(i + 1) * blk]
        return q_t, [jnp.dot(k_ref[j * blk:(j + 1) * blk, :], q_t, preferred_element_type=F32)
                     for j in range(i + 1)]

    def attend(i, q_t, s):
        s_own = jnp.where(causal, s[i], NEG_BIG)
        m = jnp.max(s_own, axis=0, keepdims=True)
        sel = []
        if i > 0:
            gate = (jnp.dot(km_hi, q_t, preferred_element_type=F32)
                    + jnp.dot(km_lo, q_t, preferred_element_type=F32))
            gate = jnp.where(bidx < i, gate, -jnp.inf)
            rank = jnp.zeros(gate.shape, F32)
            for j in range(i):
                g_j = gate[j:j + 1, :]
                beats = (g_j > gate) | ((g_j == gate) & (j < bidx))
                rank = rank + jnp.where(beats, 1.0, 0.0)
            chosen = jnp.where((bidx < i) & (rank < MOBA_TOPK), 1.0, 0.0)
            for j in range(i):
                sel_j = chosen[j:j + 1, :] > 0.5
                sel.append(sel_j)
                bm = jnp.max(s[j], axis=0, keepdims=True)
                m = jnp.maximum(m, jnp.where(sel_j, bm, NEG_BIG))
        p_own = jnp.exp2((s_own - m) * c)
        l = jnp.sum(p_own, axis=0, keepdims=True)
        acc = jnp.dot(v_t_ref[:, i * blk:(i + 1) * blk], p_own.astype(BF16), preferred_element_type=F32)
        for j in range(i):
            shift = jnp.where(sel[j], m, -NEG_BIG)
            p_j = jnp.exp2((s[j] - shift) * c)
            l = l + jnp.sum(p_j, axis=0, keepdims=True)
            acc = acc + jnp.dot(v_t_ref[:, j * blk:(j + 1) * blk], p_j.astype(BF16),
                                preferred_element_type=F32)
        o_ref[i * blk:(i + 1) * blk, :] = jnp.transpose(acc * (1.0 / l)).astype(BF16)

    nxt = scores(0)
    for i in range(n_blocks):
        cur = nxt
        if i + 1 < n_blocks:
            nxt = scores(i + 1)
        attend(i, *cur)


def _moba_call(q_t, k, v_t, kmean):
    bsz, hd, seq = q_t.shape
    n_b = seq // MOBA_BLOCK
    return pl.pallas_call(
        _moba_kernel,
        grid=(bsz, N_HEADS),
        in_specs=[pl.BlockSpec((None, HEAD_DIM, seq), lambda b, h: (b, h, 0)),
                  pl.BlockSpec((None, seq, HEAD_DIM), lambda b, h: (b, 0, h)),
                  pl.BlockSpec((None, HEAD_DIM, seq), lambda b, h: (b, h, 0)),
                  pl.BlockSpec((None, n_b, HEAD_DIM), lambda b, h: (b, 0, h))],
        out_specs=pl.BlockSpec((None, seq, HEAD_DIM), lambda b, h: (b, 0, h)),
        out_shape=jax.ShapeDtypeStruct((bsz, seq, hd), BF16),
        compiler_params=_params("arbitrary", "arbitrary"),
        name="moba_attention",
    )(q_t, k, v_t, kmean)


def _attn_out_kernel(x_ref, o_ref, cond_ref, w_out_ref, lng_ref, lnb_ref, wr_ref, br_ref,
                     x_out_ref, hx_ref, meta_t_ref):
    y = jnp.dot(o_ref[...], w_out_ref[...], preferred_element_type=F32)
    _sublayer_epilogue(x_ref[...], y, cond_ref, lng_ref, lnb_ref, wr_ref, br_ref, x_out_ref, hx_ref, meta_t_ref)


def _attn_out_call(x2d, o2d, cond, seq, w_out, lng, lnb, wr, br):
    n = x2d.shape[0]
    n_t = seq // ROW_TILE
    row = lambda i: (i, 0)
    small = [w_out, lng, lnb, wr, br]
    return pl.pallas_call(
        _attn_out_kernel,
        grid=(n // ROW_TILE,),
        in_specs=[pl.BlockSpec((ROW_TILE, D_MODEL), row),
                  pl.BlockSpec((ROW_TILE, D_MODEL), row),
                  pl.BlockSpec((None, 6, D_MODEL), lambda i: (i // n_t, 0, 0))]
                 + [_full(a.shape) for a in small],
        out_specs=(pl.BlockSpec((ROW_TILE, D_MODEL), row),
                   pl.BlockSpec((ROW_TILE, HX_W), row),
                   pl.BlockSpec((SUBLANES, ROW_TILE), lambda i: (0, i))),
        out_shape=_epilogue_out(n),
        compiler_params=_params("arbitrary"),
        name="attn_out",
    )(x2d, o2d, cond, *small)


def _router_weights(w_grp, b_grp, w_er, b_er):
    d = w_grp.shape[0]
    w_exp = jnp.transpose(w_er, (1, 0, 2)).reshape(d, N_EXPERTS)
    pad = LANES - N_GROUPS - N_EXPERTS
    w_cat = jnp.concatenate([w_grp, w_exp, jnp.zeros((d, ROUTER_LO_LANE - N_GROUPS - N_EXPERTS), F32)], axis=1)
    w_hi = w_cat.astype(BF16)
    w_lo = (w_cat - w_hi.astype(F32)).astype(BF16)
    wr = jnp.concatenate([w_hi, w_lo, jnp.zeros((d, LANES - 2 * ROUTER_LO_LANE), BF16)], axis=1)
    br = jnp.concatenate([b_grp, b_er.reshape(N_EXPERTS), jnp.zeros((pad,), F32)]).reshape(1, LANES)
    return wr, br


def kernel(x, c, ada_w, ada_b, ln_mix_g, ln_mix_b, ln_ffn_g, ln_ffn_b, ev_w_in, ev_sgu_ln_g, ev_sgu_ln_b,
           ev_w_s, ev_b_s, ev_w_dw, ev_b_dw, ev_conv_ln_g, ev_conv_ln_b, ev_w_out, od_w_qkv, od_w_out,
           moe_w_grp, moe_b_grp, moe_w_er, moe_b_er, moe_w_gate, moe_w_up, moe_w_down):
    bsz, seq, d = x.shape
    n = bsz * seq
    depth = ada_w.shape[0]
    cond_all = _cond_call(c, ada_w, ada_b).reshape(depth, bsz, 6, d)
    row = lambda v: v.reshape(1, -1)
    xc = x.reshape(n, d)
    for i in range(depth):
        cond = cond_all[i]
        j = i // 2
        wr, br = _router_weights(moe_w_grp[i], moe_b_grp[i], moe_w_er[i], moe_b_er[i])
        lng, lnb = row(ln_mix_g[i]), row(ln_mix_b[i])
        if i % 2 == 0:
            gw = A_WIDTH // A_GROUPS
            bs_full = jnp.repeat(ev_b_s[j].T, gw, axis=1)
            w_dw = jnp.concatenate([ev_w_dw[j], jnp.zeros((1, B_WIDTH), F32)], axis=0)
            x1, hx, meta_t = _even_call(
                xc, cond, bsz, seq, ev_w_in[j].astype(BF16), row(ev_sgu_ln_g[j]), row(ev_sgu_ln_b[j]),
                ev_w_s[j], bs_full, w_dw, row(ev_b_dw[j]), row(ev_conv_ln_g[j]), row(ev_conv_ln_b[j]),
                ev_w_out[j].astype(BF16), lng, lnb, wr, br)
        else:
            hd = N_HEADS * HEAD_DIM
            w_qkv = od_w_qkv[j].astype(BF16)
            q_t, k, v_t, kmean8 = _qkv_call(xc, cond, bsz, seq, w_qkv[:, :hd].T, w_qkv[:, hd:2 * hd],
                                            w_qkv[:, 2 * hd:].T)
            o = _moba_call(q_t, k, v_t, kmean8[:, :, 0, :])
            x1, hx, meta_t = _attn_out_call(xc, o.reshape(n, hd), cond, seq, od_w_out[j].astype(BF16),
                                            lng, lnb, wr, br)
        xc = _moe_sublayer(x1, hx, meta_t, cond, bsz, seq, row(ln_ffn_g[i]), row(ln_ffn_b[i]),
                           moe_w_gate[i], moe_w_up[i], moe_w_down[i])
    return xc.reshape(bsz, seq, d)
```

```python
import functools

import jax
import jax.numpy as jnp
import numpy as np
from jax import lax
from jax.experimental import pallas as pl
from jax.experimental.pallas import tpu as pltpu

D_MODEL = 1024
A_WIDTH = 512
A_GROUPS = 4
A_CHUNK = 128
B_WIDTH = 512
CONV_WIDTH = 31
N_HEADS = 8
HEAD_DIM = D_MODEL // N_HEADS
MOBA_BLOCK = 256
MOBA_TOPK = 3
N_GROUPS = 4
EXPERTS_PER_GROUP = 4
N_EXPERTS = N_GROUPS * EXPERTS_PER_GROUP
D_EXPERT = 512
DEPTH = 2
DN_ALPHA = (2.0 * DEPTH) ** 0.25
LN_EPS = 1e-5

LANES = 128
SUBLANES = 8
ROW_TILE = 256
MOE_TILE = 256
CONV_HALO = 32
PAIRS_PER_GROUP = EXPERTS_PER_GROUP * (EXPERTS_PER_GROUP - 1) // 2
N_BUCKETS = N_GROUPS * PAIRS_PER_GROUP
BUCKET_ROWS = 32
ROUTER_LO_LANE = 32
META_W = LANES
HX_W = D_MODEL + META_W
VMEM_LIMIT = 56 * 1024 * 1024
NEG_BIG = -1e30
LOG2E = 1.4426950408889634
F32 = jnp.float32
BF16 = jnp.bfloat16

_PAIR_LO = np.array([0, 0, 0, 1, 1, 2], np.int32)
_PAIR_HI = np.array([1, 2, 3, 2, 3, 3], np.int32)
_BUCKET_LO = np.concatenate([g * EXPERTS_PER_GROUP + _PAIR_LO for g in range(N_GROUPS)])
_BUCKET_HI = np.concatenate([g * EXPERTS_PER_GROUP + _PAIR_HI for g in range(N_GROUPS)])


def _params(*sem):
    return pltpu.CompilerParams(dimension_semantics=sem, vmem_limit_bytes=VMEM_LIMIT)


def _layer_norm(x, g, b):
    mu = jnp.mean(x, axis=-1, keepdims=True)
    xc = x - mu
    var = jnp.mean(xc * xc, axis=-1, keepdims=True)
    return xc * lax.rsqrt(var + LN_EPS) * g + b


def _bdot(a, b):
    return jnp.dot(a.astype(BF16), b.astype(BF16), preferred_element_type=F32)


def _cond_kernel(c_ref, w_ref, b_ref, o_ref):
    c = c_ref[...]
    c_act = c * (1.0 / (1.0 + jnp.exp(-c)))
    o_ref[...] = jnp.dot(c_act, w_ref[...], precision=lax.Precision.HIGHEST,
                         preferred_element_type=F32) + b_ref[...]


def _cond_call(c, ada_w, ada_b):
    depth, d, d6 = ada_w.shape
    bsz = c.shape[0]
    n_col = d6 // d
    return pl.pallas_call(
        _cond_kernel,
        grid=(depth, n_col),
        in_specs=[pl.BlockSpec((bsz, d), lambda i, j: (0, 0)),
                  pl.BlockSpec((None, d, d), lambda i, j: (i, 0, j)),
                  pl.BlockSpec((None, 1, d), lambda i, j: (i, 0, j))],
        out_specs=pl.BlockSpec((None, bsz, d), lambda i, j: (i, 0, j)),
        out_shape=jax.ShapeDtypeStruct((depth, bsz, d6), F32),
        compiler_params=_params("arbitrary", "arbitrary"),
        name="cond",
    )(c, ada_w, ada_b.reshape(depth, 1, d6))


def _first_max(rows):
    best = functools.reduce(jnp.maximum, rows)
    idx = jnp.full(best.shape, len(rows) - 1, jnp.int32)
    for k in range(len(rows) - 2, -1, -1):
        idx = jnp.where(rows[k] == best, k, idx)
    return best, idx


def _route(lt):
    row = lambda k: lt[k:k + 1, :]
    g_rows = [row(k) for k in range(N_GROUPS)]
    gmax, gsel = _first_max(g_rows)
    g_w = 1.0 / functools.reduce(lambda a, b: a + b, [jnp.exp(g - gmax) for g in g_rows])
    e_rows = []
    for k in range(EXPERTS_PER_GROUP):
        e = row(N_GROUPS + (N_GROUPS - 1) * EXPERTS_PER_GROUP + k)
        for g in range(N_GROUPS - 2, -1, -1):
            e = jnp.where(gsel == g, row(N_GROUPS + g * EXPERTS_PER_GROUP + k), e)
        e_rows.append(e)
    v1, i1 = _first_max(e_rows)
    v2, i2 = _first_max([jnp.where(i1 == k, -jnp.inf, e) for k, e in enumerate(e_rows)])
    t = jnp.exp(v2 - v1)
    w1 = g_w / (1.0 + t)
    w2 = g_w * t / (1.0 + t)
    first_lower = i1 < i2
    a = jnp.minimum(i1, i2)
    b = jnp.maximum(i1, i2)
    pair = ((a * (2 * EXPERTS_PER_GROUP - 1 - a)) >> 1) + (b - a - 1)
    bucket = gsel * PAIRS_PER_GROUP + pair
    w_lo = jnp.where(first_lower, w1, w2)
    w_hi = jnp.where(first_lower, w2, w1)
    return bucket.astype(F32), w_lo, w_hi


def _sublayer_epilogue(x, y, cond_ref, lng_ref, lnb_ref, wr_ref, br_ref, x_out_ref, hx_ref, meta_t_ref):
    g_m = cond_ref[2:3, :]
    sh_f = cond_ref[3:4, :]
    sc_f = cond_ref[4:5, :]
    x1 = _layer_norm(DN_ALPHA * x + (1.0 + g_m) * y, lng_ref[...], lnb_ref[...])
    x_out_ref[...] = x1
    h2 = x1 * (1.0 + sc_f) + sh_f
    h_hi = h2.astype(BF16)
    h_lo = (h2 - h_hi.astype(F32)).astype(BF16)
    both = (jnp.dot(h_hi, wr_ref[...], preferred_element_type=F32)
            + jnp.dot(h_lo, wr_ref[...], preferred_element_type=F32))
    logits = both + pltpu.roll(both, LANES - ROUTER_LO_LANE, axis=1) + br_ref[...]
    bucket, w_lo, w_hi = _route(jnp.transpose(logits))
    rows = logits.shape[0]
    sub = lax.broadcasted_iota(jnp.int32, (SUBLANES, rows), 0)
    meta_t = jnp.where(sub == 0, bucket, jnp.where(sub == 1, w_lo, jnp.where(sub == 2, w_hi, 0.0)))
    meta_t_ref[...] = meta_t
    hx_ref[:, :D_MODEL] = h2
    hx_ref[:, D_MODEL:] = jnp.transpose(
        jnp.concatenate([meta_t, jnp.zeros((META_W - SUBLANES, rows), F32)], axis=0))


def _epilogue_out(n):
    shapes = (jax.ShapeDtypeStruct((n, D_MODEL), F32),
              jax.ShapeDtypeStruct((n, HX_W), F32),
              jax.ShapeDtypeStruct((SUBLANES, n), F32))
    return shapes


def _gelu(x):
    return 0.5 * x * (1.0 + lax.erf(x * (2.0 ** -0.5)))


def _even_kernel(x_ref, cond_ref, w_in_ref, sgu_g_ref, sgu_b_ref, w_s_ref, bs_ref, w_dw_ref, b_dw_ref,
                 cg_ref, cb_ref, w_out_ref, lng_ref, lnb_ref, wr_ref, br_ref,
                 x_out_ref, hx_ref, meta_t_ref, hist_ref, shift_ref):
    rows = x_ref.shape[0]
    t = pl.program_id(1)

    @pl.when(t == 0)
    def _():
        hist_ref[0:CONV_HALO, :] = jnp.zeros((CONV_HALO, B_WIDTH), F32)

    x = x_ref[...]
    h = x * (1.0 + cond_ref[1:2, :]) + cond_ref[0:1, :]
    z = _bdot(h, w_in_ref[...])

    u = _gelu(z[:, :A_WIDTH])
    v = _layer_norm(_gelu(z[:, A_WIDTH:2 * A_WIDTH]), sgu_g_ref[...], sgu_b_ref[...])
    r_i = lax.broadcasted_iota(jnp.int32, (A_CHUNK, A_CHUNK), 0)
    c_i = lax.broadcasted_iota(jnp.int32, (A_CHUNK, A_CHUNK), 1)
    gw = A_WIDTH // A_GROUPS
    s_rows = []
    for ck in range(rows // A_CHUNK):
        s_cols = []
        for g in range(A_GROUPS):
            w_g = jnp.where(c_i <= r_i, w_s_ref[g], 0.0)
            s_cols.append(_bdot(w_g, v[ck * A_CHUNK:(ck + 1) * A_CHUNK, g * gw:(g + 1) * gw]))
        s_rows.append(jnp.concatenate(s_cols, axis=1) + bs_ref[...])
    y_a = u * jnp.concatenate(s_rows, axis=0)

    zb = z[:, 2 * A_WIDTH:]
    gate = zb[:, B_WIDTH:]
    a = zb[:, :B_WIDTH] * (1.0 / (1.0 + jnp.exp(-gate)))
    hist_ref[CONV_HALO:CONV_HALO + rows, :] = a
    first = CONV_HALO - (CONV_WIDTH - 1)
    acc = jnp.zeros((rows, B_WIDTH), F32) + b_dw_ref[...]
    for r in range(SUBLANES):
        taps = [k for k in range(CONV_WIDTH) if (first + k) % SUBLANES == r]
        if not taps:
            continue
        span = max(first + k for k in taps) - r + rows
        if r == 0:
            shifted = hist_ref
        else:
            shifted = shift_ref.at[r - 1]
            shifted[0:span, :] = hist_ref[r:r + span, :]
        for k in taps:
            lo = first + k - r
            acc = acc + w_dw_ref[k:k + 1, :] * shifted[lo:lo + rows, :]
    hist_ref[0:CONV_HALO, :] = hist_ref[rows:rows + CONV_HALO, :]
    yb = _layer_norm(acc, cg_ref[...], cb_ref[...])
    y_b = yb * (1.0 / (1.0 + jnp.exp(-yb)))

    y = _bdot(jnp.concatenate([y_a, y_b], axis=1), w_out_ref[...])
    _sublayer_epilogue(x, y, cond_ref, lng_ref, lnb_ref, wr_ref, br_ref, x_out_ref, hx_ref, meta_t_ref)


def _full(shape):
    nd = len(shape)
    return pl.BlockSpec(shape, lambda *_: (0,) * nd)


def _even_call(x2d, cond, bsz, seq, w_in, sgu_g, sgu_b, w_s, bs_full, w_dw, b_dw, cg, cb, w_out,
               lng, lnb, wr, br):
    n = bsz * seq
    n_t = seq // ROW_TILE
    row = lambda b, t: (b * n_t + t, 0)
    small = [sgu_g, sgu_b, w_s, bs_full, w_dw, b_dw, cg, cb, w_out, lng, lnb, wr, br]
    return pl.pallas_call(
        _even_kernel,
        grid=(bsz, n_t),
        in_specs=[pl.BlockSpec((ROW_TILE, D_MODEL), row),
                  pl.BlockSpec((None, 6, D_MODEL), lambda b, t: (b, 0, 0)),
                  _full(w_in.shape)] + [_full(a.shape) for a in small],
        out_specs=(pl.BlockSpec((ROW_TILE, D_MODEL), row),
                   pl.BlockSpec((ROW_TILE, HX_W), row),
                   pl.BlockSpec((SUBLANES, ROW_TILE), lambda b, t: (0, b * n_t + t))),
        out_shape=_epilogue_out(n),
        scratch_shapes=[pltpu.VMEM((CONV_HALO + ROW_TILE, B_WIDTH), F32),
                        pltpu.VMEM((SUBLANES - 1, CONV_HALO + ROW_TILE, B_WIDTH), F32)],
        compiler_params=_params("arbitrary", "arbitrary"),
        name="even_mixer",
    )(x2d, cond, w_in, *small)


RANK_SUB = 256
RANK_ROWS = 2048


def _rank_kernel(meta_t_ref, pos_ref, counts_ref, run_ref):
    phase = pl.program_id(0)
    i = pl.program_id(1)
    rows = meta_t_ref.shape[1]

    @pl.when((phase == 0) & (i == 0))
    def _():
        run_ref[...] = jnp.zeros_like(run_ref)

    @pl.when((phase == 1) & (i == 0))
    def _():
        counts = run_ref[...]
        counts_ref[...] = counts
        padded = jnp.floor((counts + (MOE_TILE - 1)) * (1.0 / MOE_TILE)) * MOE_TILE
        sub_l = lax.broadcasted_iota(jnp.int32, counts.shape, 0)
        off = jnp.zeros_like(counts)
        for b in range(N_BUCKETS):
            off = off + jnp.where(sub_l > b, padded[b:b + 1, :], 0.0)
        run_ref[...] = off

    r_i = lax.broadcasted_iota(jnp.int32, (RANK_SUB, RANK_SUB), 0)
    c_i = lax.broadcasted_iota(jnp.int32, (RANK_SUB, RANK_SUB), 1)
    tri = jnp.where(r_i <= c_i, 1.0, 0.0).astype(BF16)
    sub = lax.broadcasted_iota(jnp.int32, (BUCKET_ROWS, RANK_SUB), 0)
    run = run_ref[...]
    for t in range(rows // RANK_SUB):
        bucket = meta_t_ref[0:1, t * RANK_SUB:(t + 1) * RANK_SUB].astype(jnp.int32)
        onehot = jnp.where(sub == bucket, 1.0, 0.0)
        cum = jnp.dot(onehot.astype(BF16), tri, preferred_element_type=F32)
        pos = jnp.sum(onehot * (cum + run[:, 0:1]), axis=0, keepdims=True) - 1.0
        pos_ref[:, t * RANK_SUB:(t + 1) * RANK_SUB] = pos.astype(jnp.int32)
        run = run + cum[:, RANK_SUB - 1:RANK_SUB]
    run_ref[...] = run


def _rank_call(meta_t):
    n = meta_t.shape[1]
    n_t = n // RANK_ROWS
    return pl.pallas_call(
        _rank_kernel,
        grid=(2, n_t),
        in_specs=[pl.BlockSpec((SUBLANES, RANK_ROWS), lambda p, i: (0, i))],
        out_specs=(pl.BlockSpec((1, RANK_ROWS), lambda p, i: (0, i * p)),
                   pl.BlockSpec((BUCKET_ROWS, LANES), lambda p, i: (0, 0))),
        out_shape=(jax.ShapeDtypeStruct((1, n), jnp.int32),
                   jax.ShapeDtypeStruct((BUCKET_ROWS, LANES), F32)),
        scratch_shapes=[pltpu.VMEM((BUCKET_ROWS, LANES), F32)],
        compiler_params=_params("arbitrary", "arbitrary"),
        name="moe_rank",
    )(meta_t)


DISPATCH_ROWS = 1024


def _dispatch_kernel(pos_ref, pad_start_ref, pad_count_ref, nact_ref, hx_ref, xs_ref, zero_ref, sem, zsem):
    rows = hx_ref.shape[0]
    step = pl.program_id(0)
    base = step * rows
    n_tiles = xs_ref.shape[0] // MOE_TILE

    @pl.when(step == 0)
    def _():
        zero_ref[...] = jnp.zeros_like(zero_ref)

        def pad_row(b, r):
            return pltpu.make_async_copy(zero_ref.at[pl.ds(0, 1), :],
                                         xs_ref.at[pl.ds(pad_start_ref[b] + r, 1), :], zsem)

        def tail_tile(t):
            start = pl.multiple_of(t * MOE_TILE, MOE_TILE)
            return pltpu.make_async_copy(zero_ref, xs_ref.at[pl.ds(start, MOE_TILE), :], zsem)

        def for_all(act):
            for b in range(N_BUCKETS):
                lax.fori_loop(0, pad_count_ref[b], lambda r, c, b=b: (act(pad_row(b, r)), c)[1], 0)
            lax.fori_loop(nact_ref[0], n_tiles, lambda t, c: (act(tail_tile(t)), c)[1], 0)

        for_all(lambda cp: cp.start())
        for_all(lambda cp: cp.wait())

    for r in range(rows):
        pltpu.make_async_copy(hx_ref.at[pl.ds(r, 1), :], xs_ref.at[pl.ds(pos_ref[base + r], 1), :],
                              sem).start()
    pltpu.make_async_copy(hx_ref, xs_ref.at[pl.ds(0, rows), :], sem).wait()


def _dispatch_call(pos, pad_start, pad_count, nact, hx, n_sorted):
    n = hx.shape[0]
    return pl.pallas_call(
        _dispatch_kernel,
        grid_spec=pltpu.PrefetchScalarGridSpec(
            num_scalar_prefetch=4,
            grid=(n // DISPATCH_ROWS,),
            in_specs=[pl.BlockSpec((DISPATCH_ROWS, HX_W), lambda i, *_: (i, 0))],
            out_specs=pl.BlockSpec(memory_space=pl.ANY),
            scratch_shapes=[pltpu.VMEM((MOE_TILE, HX_W), F32),
                            pltpu.SemaphoreType.DMA(()), pltpu.SemaphoreType.DMA(())]),
        out_shape=jax.ShapeDtypeStruct((n_sorted, HX_W), F32),
        compiler_params=_params("arbitrary"),
        name="moe_dispatch",
    )(pos, pad_start, pad_count, nact, hx)


def _moe_kernel(lo_ref, hi_ref, nact_ref, xs_ref, wg_lo_ref, wu_lo_ref, wd_lo_ref,
                wg_hi_ref, wu_hi_ref, wd_hi_ref, ys_ref):
    del lo_ref, hi_ref

    @pl.when(pl.program_id(0) < nact_ref[0])
    def _():
        x = xs_ref[:, :D_MODEL].astype(BF16)
        w_lo = xs_ref[:, D_MODEL + 1:D_MODEL + 2]
        w_hi = xs_ref[:, D_MODEL + 2:D_MODEL + 3]

        def expert(wg_ref, wu_ref, wd_ref):
            gate = jnp.dot(x, wg_ref[...].astype(BF16), preferred_element_type=F32)
            up = jnp.dot(x, wu_ref[...].astype(BF16), preferred_element_type=F32)
            hid = gate * (1.0 / (1.0 + jnp.exp(-gate))) * up
            return jnp.dot(hid.astype(BF16), wd_ref[...].astype(BF16), preferred_element_type=F32)

        ys_ref[...] = (w_lo * expert(wg_lo_ref, wu_lo_ref, wd_lo_ref)
                       + w_hi * expert(wg_hi_ref, wu_hi_ref, wd_hi_ref))

    @pl.when(pl.program_id(0) >= nact_ref[0])
    def _():
        ys_ref[...] = jnp.zeros_like(ys_ref)


def _moe_call(tile_lo, tile_hi, nact, xs, layer, w_gate, w_up, w_down):
    n_sorted = xs.shape[0]
    n_tiles = n_sorted // MOE_TILE

    def tile(t, lo, hi, na):
        return (jnp.minimum(t, na[0] - 1), 0)

    def w_lo(t, lo, hi, na):
        return (layer, lo[jnp.minimum(t, na[0] - 1)], 0, 0)

    def w_hi(t, lo, hi, na):
        return (layer, hi[jnp.minimum(t, na[0] - 1)], 0, 0)

    up_spec = lambda index: pl.BlockSpec((None, None, D_MODEL, D_EXPERT), index)
    down_spec = lambda index: pl.BlockSpec((None, None, D_EXPERT, D_MODEL), index)
    return pl.pallas_call(
        _moe_kernel,
        grid_spec=pltpu.PrefetchScalarGridSpec(
            num_scalar_prefetch=3,
            grid=(n_tiles,),
            in_specs=[pl.BlockSpec((MOE_TILE, HX_W), tile),
                      up_spec(w_lo), up_spec(w_lo), down_spec(w_lo),
                      up_spec(w_hi), up_spec(w_hi), down_spec(w_hi)],
            out_specs=pl.BlockSpec((MOE_TILE, D_MODEL), lambda t, lo, hi, na: (t, 0))),
        out_shape=jax.ShapeDtypeStruct((n_sorted, D_MODEL), F32),
        compiler_params=_params("arbitrary"),
        name="moe_experts",
    )(tile_lo, tile_hi, nact, xs, w_gate, w_up, w_down, w_gate, w_up, w_down)


def _gathered_rows(pos_ref, ys_ref, ybuf, sems, i, n_steps):
    rows = ybuf.shape[1]

    def issue(step, slot):
        for r in range(rows):
            pltpu.make_async_copy(ys_ref.at[pl.ds(pos_ref[step * rows + r], 1), :],
                                  ybuf.at[slot, pl.ds(r, 1), :], sems.at[slot]).start()

    def wait(slot):
        pltpu.make_async_copy(ys_ref.at[pl.ds(0, rows), :], ybuf.at[slot], sems.at[slot]).wait()

    @pl.when(i == 0)
    def _():
        issue(0, 0)

    slot = i % 2

    def wait_current():
        wait(slot)
        return ybuf[slot]

    def issue_next():
        issue(jnp.minimum(i + 1, n_steps - 1), 1 - slot)

    def drain():
        @pl.when(i == n_steps - 1)
        def _():
            wait(1 - slot)

    return wait_current, issue_next, drain


def _combine_kernel(pos_ref, x_ref, cond_ref, lng_ref, lnb_ref, ys_ref, o_ref, ybuf, sems):
    wait_current, issue_next, drain = _gathered_rows(pos_ref, ys_ref, ybuf, sems, pl.program_id(0),
                                                     pl.num_programs(0))
    g_f = cond_ref[5:6, :]
    pre = DN_ALPHA * x_ref[...] + (1.0 + g_f) * wait_current()
    issue_next()
    o_ref[...] = _layer_norm(pre, lng_ref[...], lnb_ref[...])
    drain()


def _combine_call(pos, x1, cond, seq, lng, lnb, ys):
    n = x1.shape[0]
    n_t = seq // ROW_TILE
    return pl.pallas_call(
        _combine_kernel,
        grid_spec=pltpu.PrefetchScalarGridSpec(
            num_scalar_prefetch=1,
            grid=(n // ROW_TILE,),
            in_specs=[pl.BlockSpec((ROW_TILE, D_MODEL), lambda i, p: (i, 0)),
                      pl.BlockSpec((None, 6, D_MODEL), lambda i, p: (i // n_t, 0, 0)),
                      pl.BlockSpec((1, D_MODEL), lambda i, p: (0, 0)),
                      pl.BlockSpec((1, D_MODEL), lambda i, p: (0, 0)),
                      pl.BlockSpec(memory_space=pl.ANY)],
            out_specs=pl.BlockSpec((ROW_TILE, D_MODEL), lambda i, p: (i, 0)),
            scratch_shapes=[pltpu.VMEM((2, ROW_TILE, D_MODEL), F32),
                            pltpu.SemaphoreType.DMA((2,))]),
        out_shape=jax.ShapeDtypeStruct((n, D_MODEL), F32),
        compiler_params=_params("arbitrary"),
        name="moe_combine",
    )(pos, x1, cond, lng, lnb, ys)


def _moe_sublayer(hx, meta_t, layer, w_gate, w_up, w_down):
    n = hx.shape[0]
    n_sorted = n + N_BUCKETS * MOE_TILE
    pos2d, counts = _rank_call(meta_t)
    pos = pos2d.reshape(n)
    cnt = counts[:N_BUCKETS, 0].astype(jnp.int32)
    padded = ((cnt + MOE_TILE - 1) // MOE_TILE) * MOE_TILE
    ends = jnp.cumsum(padded)
    starts = jnp.arange(n_sorted // MOE_TILE, dtype=jnp.int32) * MOE_TILE
    tile_bucket = jnp.minimum(jnp.sum((ends[None, :] <= starts[:, None]).astype(jnp.int32), axis=1),
                              N_BUCKETS - 1)
    tile_lo = jnp.asarray(_BUCKET_LO)[tile_bucket]
    tile_hi = jnp.asarray(_BUCKET_HI)[tile_bucket]
    nact = (ends[-1:] // MOE_TILE).astype(jnp.int32)
    xs = _dispatch_call(pos, ends - padded + cnt, padded - cnt, nact, hx, n_sorted)
    ys = _moe_call(tile_lo, tile_hi, nact, xs, layer, w_gate, w_up, w_down)
    return pos, ys


def _qkv_kernel(pos_ref, x1_ref, cond_prev_ref, cond_ref, lng_ref, lnb_ref, ys_ref, wq_t_ref, wk_ref, wv_t_ref,
                x_out_ref, q_t_ref, k_ref, v_t_ref, kmean_ref, ybuf, sems):
    wait_current, issue_next, drain = _gathered_rows(pos_ref, ys_ref, ybuf, sems, pl.program_id(0),
                                                     pl.num_programs(0))
    g_f = cond_prev_ref[5:6, :]
    pre = DN_ALPHA * x1_ref[...] + (1.0 + g_f) * wait_current()
    issue_next()
    x = _layer_norm(pre, lng_ref[...], lnb_ref[...])
    x_out_ref[...] = x
    h = (x * (1.0 + cond_ref[1:2, :]) + cond_ref[0:1, :]).astype(BF16)
    nt = (((1,), (1,)), ((), ()))
    q_t_ref[...] = lax.dot_general(wq_t_ref[...], h, nt, preferred_element_type=F32).astype(BF16)
    v_t_ref[...] = lax.dot_general(wv_t_ref[...], h, nt, preferred_element_type=F32).astype(BF16)
    k = jnp.dot(h, wk_ref[...], preferred_element_type=F32)
    k_ref[...] = k.astype(BF16)
    kmean_ref[...] = jnp.broadcast_to(jnp.mean(k, axis=0, keepdims=True), kmean_ref.shape)
    drain()


def _qkv_call(pos, x1, cond_prev, cond, lng, lnb, ys, bsz, seq, wq_t, wk, wv_t):
    n_t = seq // MOBA_BLOCK
    hd = N_HEADS * HEAD_DIM
    n = bsz * seq
    per_batch = lambda i, p: (i // n_t, 0, 0)
    return pl.pallas_call(
        _qkv_kernel,
        grid_spec=pltpu.PrefetchScalarGridSpec(
            num_scalar_prefetch=1,
            grid=(n // MOBA_BLOCK,),
            in_specs=[pl.BlockSpec((MOBA_BLOCK, D_MODEL), lambda i, p: (i, 0)),
                      pl.BlockSpec((None, 6, D_MODEL), per_batch),
                      pl.BlockSpec((None, 6, D_MODEL), per_batch),
                      pl.BlockSpec((1, D_MODEL), lambda i, p: (0, 0)),
                      pl.BlockSpec((1, D_MODEL), lambda i, p: (0, 0)),
                      pl.BlockSpec(memory_space=pl.ANY),
                      _full(wq_t.shape), _full(wk.shape), _full(wv_t.shape)],
            out_specs=(pl.BlockSpec((MOBA_BLOCK, D_MODEL), lambda i, p: (i, 0)),
                       pl.BlockSpec((None, hd, MOBA_BLOCK), lambda i, p: (i // n_t, 0, i % n_t)),
                       pl.BlockSpec((None, MOBA_BLOCK, hd), lambda i, p: (i // n_t, i % n_t, 0)),
                       pl.BlockSpec((None, hd, MOBA_BLOCK), lambda i, p: (i // n_t, 0, i % n_t)),
                       pl.BlockSpec((None, None, SUBLANES, hd), lambda i, p: (i // n_t, i % n_t, 0, 0))),
            scratch_shapes=[pltpu.VMEM((2, MOBA_BLOCK, D_MODEL), F32),
                            pltpu.SemaphoreType.DMA((2,))]),
        out_shape=(jax.ShapeDtypeStruct((n, D_MODEL), F32),
                   jax.ShapeDtypeStruct((bsz, hd, seq), BF16),
                   jax.ShapeDtypeStruct((bsz, seq, hd), BF16),
                   jax.ShapeDtypeStruct((bsz, hd, seq), BF16),
                   jax.ShapeDtypeStruct((bsz, n_t, SUBLANES, hd), F32)),
        compiler_params=_params("arbitrary"),
        name="qkv_proj",
    )(pos, x1, cond_prev, cond, lng, lnb, ys, wq_t, wk, wv_t)


def _moba_kernel(q_t_ref, k_ref, v_t_ref, kmean_ref, o_ref):
    n_blocks = kmean_ref.shape[0]
    blk = MOBA_BLOCK
    c = HEAD_DIM ** -0.5 * LOG2E
    km = kmean_ref[...]
    km_hi = km.astype(BF16)
    km_lo = (km - km_hi.astype(F32)).astype(BF16)
    causal = (lax.broadcasted_iota(jnp.int32, (blk, blk), 0)
              <= lax.broadcasted_iota(jnp.int32, (blk, blk), 1))
    bidx = lax.broadcasted_iota(jnp.int32, (n_blocks, blk), 0)

    def scores(i):
        q_t = q_t_ref[:, i * blk:(i + 1) * blk]
        return q_t, [jnp.dot(k_ref[j * blk:(j + 1) * blk, :], q_t, preferred_element_type=F32)
                     for j in range(i + 1)]

    def attend(i, q_t, s):
        s_own = jnp.where(causal, s[i], NEG_BIG)
        m = jnp.max(s_own, axis=0, keepdims=True)
        sel = []
        if i > 0:
            gate = (jnp.dot(km_hi, q_t, preferred_element_type=F32)
                    + jnp.dot(km_lo, q_t, preferred_element_type=F32))
            gate = jnp.where(bidx < i, gate, -jnp.inf)
            rank = jnp.zeros(gate.shape, F32)
            for j in range(i):
                g_j = gate[j:j + 1, :]
                beats = (g_j > gate) | ((g_j == gate) & (j < bidx))
                rank = rank + jnp.where(beats, 1.0, 0.0)
            chosen = jnp.where((bidx < i) & (rank < MOBA_TOPK), 1.0, 0.0)
            for j in range(i):
                sel_j = chosen[j:j + 1, :] > 0.5
                sel.append(sel_j)
                bm = jnp.max(s[j], axis=0, keepdims=True)
                m = jnp.maximum(m, jnp.where(sel_j, bm, NEG_BIG))
        p_own = jnp.exp2((s_own - m) * c)
        l = jnp.sum(p_own, axis=0, keepdims=True)
        acc = jnp.dot(v_t_ref[:, i * blk:(i + 1) * blk], p_own.astype(BF16), preferred_element_type=F32)
        for j in range(i):
            shift = jnp.where(sel[j], m, -NEG_BIG)
            p_j = jnp.exp2((s[j] - shift) * c)
            l = l + jnp.sum(p_j, axis=0, keepdims=True)
            acc = acc + jnp.dot(v_t_ref[:, j * blk:(j + 1) * blk], p_j.astype(BF16),
                                preferred_element_type=F32)
        o_ref[i * blk:(i + 1) * blk, :] = jnp.transpose(acc * (1.0 / l)).astype(BF16)

    nxt = scores(0)
    for i in range(n_blocks):
        cur = nxt
        if i + 1 < n_blocks:
            nxt = scores(i + 1)
        attend(i, *cur)


def _moba_call(q_t, k, v_t, kmean):
    bsz, hd, seq = q_t.shape
    n_b = seq // MOBA_BLOCK
    return pl.pallas_call(
        _moba_kernel,
        grid=(bsz, N_HEADS),
        in_specs=[pl.BlockSpec((None, HEAD_DIM, seq), lambda b, h: (b, h, 0)),
                  pl.BlockSpec((None, seq, HEAD_DIM), lambda b, h: (b, 0, h)),
                  pl.BlockSpec((None, HEAD_DIM, seq), lambda b, h: (b, h, 0)),
                  pl.BlockSpec((None, n_b, HEAD_DIM), lambda b, h: (b, 0, h))],
        out_specs=pl.BlockSpec((None, seq, HEAD_DIM), lambda b, h: (b, 0, h)),
        out_shape=jax.ShapeDtypeStruct((bsz, seq, hd), BF16),
        compiler_params=_params("arbitrary", "arbitrary"),
        name="moba_attention",
    )(q_t, k, v_t, kmean)


def _attn_out_kernel(x_ref, o_ref, cond_ref, w_out_ref, lng_ref, lnb_ref, wr_ref, br_ref,
                     x_out_ref, hx_ref, meta_t_ref):
    y = jnp.dot(o_ref[...], w_out_ref[...], preferred_element_type=F32)
    _sublayer_epilogue(x_ref[...], y, cond_ref, lng_ref, lnb_ref, wr_ref, br_ref, x_out_ref, hx_ref, meta_t_ref)


def _attn_out_call(x2d, o2d, cond, seq, w_out, lng, lnb, wr, br):
    n = x2d.shape[0]
    n_t = seq // ROW_TILE
    row = lambda i: (i, 0)
    small = [w_out, lng, lnb, wr, br]
    return pl.pallas_call(
        _attn_out_kernel,
        grid=(n // ROW_TILE,),
        in_specs=[pl.BlockSpec((ROW_TILE, D_MODEL), row),
                  pl.BlockSpec((ROW_TILE, D_MODEL), row),
                  pl.BlockSpec((None, 6, D_MODEL), lambda i: (i // n_t, 0, 0))]
                 + [_full(a.shape) for a in small],
        out_specs=(pl.BlockSpec((ROW_TILE, D_MODEL), row),
                   pl.BlockSpec((ROW_TILE, HX_W), row),
                   pl.BlockSpec((SUBLANES, ROW_TILE), lambda i: (0, i))),
        out_shape=_epilogue_out(n),
        compiler_params=_params("arbitrary"),
        name="attn_out",
    )(x2d, o2d, cond, *small)


def _router_weights(w_grp, b_grp, w_er, b_er):
    d = w_grp.shape[0]
    w_exp = jnp.transpose(w_er, (1, 0, 2)).reshape(d, N_EXPERTS)
    pad = LANES - N_GROUPS - N_EXPERTS
    w_cat = jnp.concatenate([w_grp, w_exp, jnp.zeros((d, ROUTER_LO_LANE - N_GROUPS - N_EXPERTS), F32)], axis=1)
    w_hi = w_cat.astype(BF16)
    w_lo = (w_cat - w_hi.astype(F32)).astype(BF16)
    wr = jnp.concatenate([w_hi, w_lo, jnp.zeros((d, LANES - 2 * ROUTER_LO_LANE), BF16)], axis=1)
    br = jnp.concatenate([b_grp, b_er.reshape(N_EXPERTS), jnp.zeros((pad,), F32)]).reshape(1, LANES)
    return wr, br


def kernel(x, c, ada_w, ada_b, ln_mix_g, ln_mix_b, ln_ffn_g, ln_ffn_b, ev_w_in, ev_sgu_ln_g, ev_sgu_ln_b,
           ev_w_s, ev_b_s, ev_w_dw, ev_b_dw, ev_conv_ln_g, ev_conv_ln_b, ev_w_out, od_w_qkv, od_w_out,
           moe_w_grp, moe_b_grp, moe_w_er, moe_b_er, moe_w_gate, moe_w_up, moe_w_down):
    bsz, seq, d = x.shape
    n = bsz * seq
    depth = ada_w.shape[0]
    cond_all = _cond_call(c, ada_w, ada_b).reshape(depth, bsz, 6, d)
    row = lambda v: v.reshape(1, -1)
    xc = x.reshape(n, d)
    pending = None
    for i in range(depth):
        cond = cond_all[i]
        j = i // 2
        wr, br = _router_weights(moe_w_grp[i], moe_b_grp[i], moe_w_er[i], moe_b_er[i])
        lng, lnb = row(ln_mix_g[i]), row(ln_mix_b[i])
        if i % 2 == 0:
            if pending is not None:
                xc = _combine_call(pending[1], pending[0], pending[3], seq, pending[4], pending[5], pending[2])
            gw = A_WIDTH // A_GROUPS
            bs_full = jnp.repeat(ev_b_s[j].T, gw, axis=1)
            w_dw = jnp.concatenate([ev_w_dw[j], jnp.zeros((1, B_WIDTH), F32)], axis=0)
            x1, hx, meta_t = _even_call(
                xc, cond, bsz, seq, ev_w_in[j].astype(BF16), row(ev_sgu_ln_g[j]), row(ev_sgu_ln_b[j]),
                ev_w_s[j], bs_full, w_dw, row(ev_b_dw[j]), row(ev_conv_ln_g[j]), row(ev_conv_ln_b[j]),
                ev_w_out[j].astype(BF16), lng, lnb, wr, br)
        else:
            hd = N_HEADS * HEAD_DIM
            w_qkv = od_w_qkv[j].astype(BF16)
            xc, q_t, k, v_t, kmean8 = _qkv_call(
                pending[1], pending[0], pending[3], cond, pending[4], pending[5], pending[2], bsz, seq,
                w_qkv[:, :hd].T, w_qkv[:, hd:2 * hd], w_qkv[:, 2 * hd:].T)
            o = _moba_call(q_t, k, v_t, kmean8[:, :, 0, :])
            x1, hx, meta_t = _attn_out_call(xc, o.reshape(n, hd), cond, seq, od_w_out[j].astype(BF16),
                                            lng, lnb, wr, br)
        pos, ys = _moe_sublayer(hx, meta_t, i, moe_w_gate, moe_w_up, moe_w_down)
        pending = (x1, pos, ys, cond, row(ln_ffn_g[i]), row(ln_ffn_b[i]))
    xc = _combine_call(pending[1], pending[0], pending[3], seq, pending[4], pending[5], pending[2])
    return xc.reshape(bsz, seq, d)
```

```python
import functools

import jax
import jax.numpy as jnp
import numpy as np
from jax import lax
from jax.experimental import pallas as pl
from jax.experimental.pallas import tpu as pltpu

D_MODEL = 1024
A_WIDTH = 512
A_GROUPS = 4
A_CHUNK = 128
B_WIDTH = 512
CONV_WIDTH = 31
N_HEADS = 8
HEAD_DIM = D_MODEL // N_HEADS
MOBA_BLOCK = 256
MOBA_TOPK = 3
N_GROUPS = 4
EXPERTS_PER_GROUP = 4
N_EXPERTS = N_GROUPS * EXPERTS_PER_GROUP
D_EXPERT = 512
DEPTH = 2
DN_ALPHA = (2.0 * DEPTH) ** 0.25
LN_EPS = 1e-5

LANES = 128
SUBLANES = 8
ROW_TILE = 256
MOE_TILE = 256
CONV_HALO = 32
PAIRS_PER_GROUP = EXPERTS_PER_GROUP * (EXPERTS_PER_GROUP - 1) // 2
N_BUCKETS = N_GROUPS * PAIRS_PER_GROUP
BUCKET_ROWS = 32
ROUTER_LO_LANE = 32
META_W = LANES
HX_W = D_MODEL + META_W
VMEM_LIMIT = 56 * 1024 * 1024
NEG_BIG = -1e30
LOG2E = 1.4426950408889634
F32 = jnp.float32
BF16 = jnp.bfloat16

_PAIR_LO = np.array([0, 0, 0, 1, 1, 2], np.int32)
_PAIR_HI = np.array([1, 2, 3, 2, 3, 3], np.int32)
_BUCKET_LO = np.concatenate([g * EXPERTS_PER_GROUP + _PAIR_LO for g in range(N_GROUPS)])
_BUCKET_HI = np.concatenate([g * EXPERTS_PER_GROUP + _PAIR_HI for g in range(N_GROUPS)])


def _params(*sem):
    return pltpu.CompilerParams(dimension_semantics=sem, vmem_limit_bytes=VMEM_LIMIT)


def _layer_norm(x, g, b):
    mu = jnp.mean(x, axis=-1, keepdims=True)
    xc = x - mu
    var = jnp.mean(xc * xc, axis=-1, keepdims=True)
    return xc * lax.rsqrt(var + LN_EPS) * g + b


def _bdot(a, b):
    return jnp.dot(a.astype(BF16), b.astype(BF16), preferred_element_type=F32)


def _cond_kernel(c_ref, w_ref, b_ref, o_ref):
    c = c_ref[...]
    c_act = c * (1.0 / (1.0 + jnp.exp(-c)))
    o_ref[...] = jnp.dot(c_act, w_ref[...], precision=lax.Precision.HIGHEST,
                         preferred_element_type=F32) + b_ref[...]


def _cond_call(c, ada_w, ada_b):
    depth, d, d6 = ada_w.shape
    bsz = c.shape[0]
    n_col = d6 // d
    return pl.pallas_call(
        _cond_kernel,
        grid=(depth, n_col),
        in_specs=[pl.BlockSpec((bsz, d), lambda i, j: (0, 0)),
                  pl.BlockSpec((None, d, d), lambda i, j: (i, 0, j)),
                  pl.BlockSpec((None, 1, d), lambda i, j: (i, 0, j))],
        out_specs=pl.BlockSpec((None, bsz, d), lambda i, j: (i, 0, j)),
        out_shape=jax.ShapeDtypeStruct((depth, bsz, d6), F32),
        compiler_params=_params("arbitrary", "arbitrary"),
        name="cond",
    )(c, ada_w, ada_b.reshape(depth, 1, d6))


def _first_max(rows):
    best = functools.reduce(jnp.maximum, rows)
    idx = jnp.full(best.shape, len(rows) - 1, jnp.int32)
    for k in range(len(rows) - 2, -1, -1):
        idx = jnp.where(rows[k] == best, k, idx)
    return best, idx


def _route(lt):
    row = lambda k: lt[k:k + 1, :]
    g_rows = [row(k) for k in range(N_GROUPS)]
    gmax, gsel = _first_max(g_rows)
    g_w = 1.0 / functools.reduce(lambda a, b: a + b, [jnp.exp(g - gmax) for g in g_rows])
    e_rows = []
    for k in range(EXPERTS_PER_GROUP):
        e = row(N_GROUPS + (N_GROUPS - 1) * EXPERTS_PER_GROUP + k)
        for g in range(N_GROUPS - 2, -1, -1):
            e = jnp.where(gsel == g, row(N_GROUPS + g * EXPERTS_PER_GROUP + k), e)
        e_rows.append(e)
    v1, i1 = _first_max(e_rows)
    v2, i2 = _first_max([jnp.where(i1 == k, -jnp.inf, e) for k, e in enumerate(e_rows)])
    t = jnp.exp(v2 - v1)
    w1 = g_w / (1.0 + t)
    w2 = g_w * t / (1.0 + t)
    first_lower = i1 < i2
    a = jnp.minimum(i1, i2)
    b = jnp.maximum(i1, i2)
    pair = ((a * (2 * EXPERTS_PER_GROUP - 1 - a)) >> 1) + (b - a - 1)
    bucket = gsel * PAIRS_PER_GROUP + pair
    w_lo = jnp.where(first_lower, w1, w2)
    w_hi = jnp.where(first_lower, w2, w1)
    return bucket.astype(F32), w_lo, w_hi


def _sublayer_epilogue(x, y, cond_ref, lng_ref, lnb_ref, wr_ref, br_ref, x_out_ref, hx_ref, meta_t_ref):
    g_m = cond_ref[2:3, :]
    sh_f = cond_ref[3:4, :]
    sc_f = cond_ref[4:5, :]
    x1 = _layer_norm(DN_ALPHA * x + (1.0 + g_m) * y, lng_ref[...], lnb_ref[...])
    x_out_ref[...] = x1
    h2 = x1 * (1.0 + sc_f) + sh_f
    h_hi = h2.astype(BF16)
    h_lo = (h2 - h_hi.astype(F32)).astype(BF16)
    both = (jnp.dot(h_hi, wr_ref[...], preferred_element_type=F32)
            + jnp.dot(h_lo, wr_ref[...], preferred_element_type=F32))
    logits = both + pltpu.roll(both, LANES - ROUTER_LO_LANE, axis=1) + br_ref[...]
    bucket, w_lo, w_hi = _route(jnp.transpose(logits))
    rows = logits.shape[0]
    sub = lax.broadcasted_iota(jnp.int32, (SUBLANES, rows), 0)
    meta_t = jnp.where(sub == 0, bucket, jnp.where(sub == 1, w_lo, jnp.where(sub == 2, w_hi, 0.0)))
    meta_t_ref[...] = meta_t
    hx_ref[:, :D_MODEL] = h2
    hx_ref[:, D_MODEL:] = jnp.transpose(
        jnp.concatenate([meta_t, jnp.zeros((META_W - SUBLANES, rows), F32)], axis=0))


def _epilogue_out(n):
    shapes = (jax.ShapeDtypeStruct((n, D_MODEL), F32),
              jax.ShapeDtypeStruct((n, HX_W), F32),
              jax.ShapeDtypeStruct((SUBLANES, n), F32))
    return shapes


def _gelu(x):
    return 0.5 * x * (1.0 + lax.erf(x * (2.0 ** -0.5)))


def _even_kernel(x_ref, cond_ref, w_in_ref, sgu_g_ref, sgu_b_ref, w_s_ref, bs_ref, w_dw_ref, b_dw_ref,
                 cg_ref, cb_ref, w_out_ref, lng_ref, lnb_ref, wr_ref, br_ref,
                 x_out_ref, hx_ref, meta_t_ref, hist_ref, shift_ref):
    rows = x_ref.shape[0]
    t = pl.program_id(1)

    @pl.when(t == 0)
    def _():
        hist_ref[0:CONV_HALO, :] = jnp.zeros((CONV_HALO, B_WIDTH), F32)

    x = x_ref[...]
    h = x * (1.0 + cond_ref[1:2, :]) + cond_ref[0:1, :]
    z = _bdot(h, w_in_ref[...])

    u = _gelu(z[:, :A_WIDTH])
    v = _layer_norm(_gelu(z[:, A_WIDTH:2 * A_WIDTH]), sgu_g_ref[...], sgu_b_ref[...])
    r_i = lax.broadcasted_iota(jnp.int32, (A_CHUNK, A_CHUNK), 0)
    c_i = lax.broadcasted_iota(jnp.int32, (A_CHUNK, A_CHUNK), 1)
    gw = A_WIDTH // A_GROUPS
    s_rows = []
    for ck in range(rows // A_CHUNK):
        s_cols = []
        for g in range(A_GROUPS):
            w_g = jnp.where(c_i <= r_i, w_s_ref[g], 0.0)
            s_cols.append(_bdot(w_g, v[ck * A_CHUNK:(ck + 1) * A_CHUNK, g * gw:(g + 1) * gw]))
        s_rows.append(jnp.concatenate(s_cols, axis=1) + bs_ref[...])
    y_a = u * jnp.concatenate(s_rows, axis=0)

    zb = z[:, 2 * A_WIDTH:]
    gate = zb[:, B_WIDTH:]
    a = zb[:, :B_WIDTH] * (1.0 / (1.0 + jnp.exp(-gate)))
    hist_ref[CONV_HALO:CONV_HALO + rows, :] = a
    first = CONV_HALO - (CONV_WIDTH - 1)
    acc = jnp.zeros((rows, B_WIDTH), F32) + b_dw_ref[...]
    for r in range(SUBLANES):
        taps = [k for k in range(CONV_WIDTH) if (first + k) % SUBLANES == r]
        if not taps:
            continue
        span = max(first + k for k in taps) - r + rows
        if r == 0:
            shifted = hist_ref
        else:
            shifted = shift_ref.at[r - 1]
            shifted[0:span, :] = hist_ref[r:r + span, :]
        for k in taps:
            lo = first + k - r
            acc = acc + w_dw_ref[k:k + 1, :] * shifted[lo:lo + rows, :]
    hist_ref[0:CONV_HALO, :] = hist_ref[rows:rows + CONV_HALO, :]
    yb = _layer_norm(acc, cg_ref[...], cb_ref[...])
    y_b = yb * (1.0 / (1.0 + jnp.exp(-yb)))

    y = _bdot(jnp.concatenate([y_a, y_b], axis=1), w_out_ref[...])
    _sublayer_epilogue(x, y, cond_ref, lng_ref, lnb_ref, wr_ref, br_ref, x_out_ref, hx_ref, meta_t_ref)


def _full(shape):
    nd = len(shape)
    return pl.BlockSpec(shape, lambda *_: (0,) * nd)


def _even_call(x2d, cond, bsz, seq, w_in, sgu_g, sgu_b, w_s, bs_full, w_dw, b_dw, cg, cb, w_out,
               lng, lnb, wr, br):
    n = bsz * seq
    n_t = seq // ROW_TILE
    row = lambda b, t: (b * n_t + t, 0)
    small = [sgu_g, sgu_b, w_s, bs_full, w_dw, b_dw, cg, cb, w_out, lng, lnb, wr, br]
    return pl.pallas_call(
        _even_kernel,
        grid=(bsz, n_t),
        in_specs=[pl.BlockSpec((ROW_TILE, D_MODEL), row),
                  pl.BlockSpec((None, 6, D_MODEL), lambda b, t: (b, 0, 0)),
                  _full(w_in.shape)] + [_full(a.shape) for a in small],
        out_specs=(pl.BlockSpec((ROW_TILE, D_MODEL), row),
                   pl.BlockSpec((ROW_TILE, HX_W), row),
                   pl.BlockSpec((SUBLANES, ROW_TILE), lambda b, t: (0, b * n_t + t))),
        out_shape=_epilogue_out(n),
        scratch_shapes=[pltpu.VMEM((CONV_HALO + ROW_TILE, B_WIDTH), F32),
                        pltpu.VMEM((SUBLANES - 1, CONV_HALO + ROW_TILE, B_WIDTH), F32)],
        compiler_params=_params("arbitrary", "arbitrary"),
        name="even_mixer",
    )(x2d, cond, w_in, *small)


RANK_SUB = 256
RANK_ROWS = 2048


def _rank_kernel(meta_t_ref, pos_ref, counts_ref, run_ref):
    phase = pl.program_id(0)
    i = pl.program_id(1)
    rows = meta_t_ref.shape[1]

    @pl.when((phase == 0) & (i == 0))
    def _():
        run_ref[...] = jnp.zeros_like(run_ref)

    @pl.when((phase == 1) & (i == 0))
    def _():
        counts = run_ref[...]
        counts_ref[...] = counts
        padded = jnp.floor((counts + (MOE_TILE - 1)) * (1.0 / MOE_TILE)) * MOE_TILE
        sub_l = lax.broadcasted_iota(jnp.int32, counts.shape, 0)
        off = jnp.zeros_like(counts)
        for b in range(N_BUCKETS):
            off = off + jnp.where(sub_l > b, padded[b:b + 1, :], 0.0)
        run_ref[...] = off

    r_i = lax.broadcasted_iota(jnp.int32, (RANK_SUB, RANK_SUB), 0)
    c_i = lax.broadcasted_iota(jnp.int32, (RANK_SUB, RANK_SUB), 1)
    tri = jnp.where(r_i <= c_i, 1.0, 0.0).astype(BF16)
    sub = lax.broadcasted_iota(jnp.int32, (BUCKET_ROWS, RANK_SUB), 0)
    run = run_ref[...]
    for t in range(rows // RANK_SUB):
        bucket = meta_t_ref[0:1, t * RANK_SUB:(t + 1) * RANK_SUB].astype(jnp.int32)
        onehot = jnp.where(sub == bucket, 1.0, 0.0)
        cum = jnp.dot(onehot.astype(BF16), tri, preferred_element_type=F32)
        pos = jnp.sum(onehot * (cum + run[:, 0:1]), axis=0, keepdims=True) - 1.0
        pos_ref[:, t * RANK_SUB:(t + 1) * RANK_SUB] = pos.astype(jnp.int32)
        run = run + cum[:, RANK_SUB - 1:RANK_SUB]
    run_ref[...] = run


def _rank_call(meta_t):
    n = meta_t.shape[1]
    n_t = n // RANK_ROWS
    return pl.pallas_call(
        _rank_kernel,
        grid=(2, n_t),
        in_specs=[pl.BlockSpec((SUBLANES, RANK_ROWS), lambda p, i: (0, i))],
        out_specs=(pl.BlockSpec((1, RANK_ROWS), lambda p, i: (0, i * p)),
                   pl.BlockSpec((BUCKET_ROWS, LANES), lambda p, i: (0, 0))),
        out_shape=(jax.ShapeDtypeStruct((1, n), jnp.int32),
                   jax.ShapeDtypeStruct((BUCKET_ROWS, LANES), F32)),
        scratch_shapes=[pltpu.VMEM((BUCKET_ROWS, LANES), F32)],
        compiler_params=_params("arbitrary", "arbitrary"),
        name="moe_rank",
    )(meta_t)


def _gathered_rows(idx_ref, src_ref, buf, sems, i, n_steps):
    slots, rows = buf.shape[0], buf.shape[1]
    ahead = slots - 1

    def issue(step, slot):
        for r in range(rows):
            pltpu.make_async_copy(src_ref.at[pl.ds(idx_ref[step * rows + r], 1), :],
                                  buf.at[slot, pl.ds(r, 1), :], sems.at[slot]).start()

    def wait(slot):
        pltpu.make_async_copy(src_ref.at[pl.ds(0, rows), :], buf.at[slot], sems.at[slot]).wait()

    @pl.when(i == 0)
    def _():
        for a in range(ahead):
            issue(jnp.minimum(a, n_steps - 1), a)

    def wait_current():
        slot = lax.rem(i, slots)
        wait(slot)
        return buf[slot]

    def issue_next():
        issue(jnp.minimum(i + ahead, n_steps - 1), lax.rem(i + ahead, slots))

    def drain():
        @pl.when(i == n_steps - 1)
        def _():
            for a in range(1, slots):
                wait(lax.rem(i + a, slots))

    return wait_current, issue_next, drain


def _invert_kernel(pos_ref, src_ref):
    unroll = SUBLANES

    def clear(g, c):
        for u in range(unroll):
            src_ref[g * unroll + u] = 0
        return c

    def scatter(g, c):
        for u in range(unroll):
            n = g * unroll + u
            src_ref[pos_ref[n]] = n
        return c

    lax.fori_loop(0, src_ref.shape[0] // unroll, clear, 0)
    lax.fori_loop(0, pos_ref.shape[0] // unroll, scatter, 0)


def _invert_call(pos, n_sorted):
    return pl.pallas_call(
        _invert_kernel,
        in_specs=[pl.BlockSpec(memory_space=pltpu.SMEM)],
        out_specs=pl.BlockSpec(memory_space=pltpu.SMEM),
        out_shape=jax.ShapeDtypeStruct((n_sorted,), jnp.int32),
        name="moe_invert",
    )(pos)


def _moe_kernel(lo_ref, hi_ref, nact_ref, src_ref, hx_ref, wg_lo_ref, wu_lo_ref, wd_lo_ref,
                wg_hi_ref, wu_hi_ref, wd_hi_ref, ys_ref, xbuf, sems):
    del lo_ref, hi_ref
    t = pl.program_id(0)
    n_active = nact_ref[0]

    @pl.when(t < n_active)
    def _():
        wait_current, issue_next, drain = _gathered_rows(src_ref, hx_ref, xbuf, sems, t, n_active)
        tile = wait_current()
        x = tile[:, :D_MODEL].astype(BF16)
        w_lo = tile[:, D_MODEL + 1:D_MODEL + 2]
        w_hi = tile[:, D_MODEL + 2:D_MODEL + 3]
        issue_next()

        def expert(wg_ref, wu_ref, wd_ref):
            gate = jnp.dot(x, wg_ref[...].astype(BF16), preferred_element_type=F32)
            up = jnp.dot(x, wu_ref[...].astype(BF16), preferred_element_type=F32)
            hid = gate * (1.0 / (1.0 + jnp.exp(-gate))) * up
            return jnp.dot(hid.astype(BF16), wd_ref[...].astype(BF16), preferred_element_type=F32)

        ys_ref[...] = (w_lo * expert(wg_lo_ref, wu_lo_ref, wd_lo_ref)
                       + w_hi * expert(wg_hi_ref, wu_hi_ref, wd_hi_ref))
        drain()

    @pl.when(t >= n_active)
    def _():
        ys_ref[...] = jnp.zeros_like(ys_ref)


def _moe_call(tile_lo, tile_hi, nact, src, hx, layer, w_gate, w_up, w_down):
    n_sorted = src.shape[0]
    n_tiles = n_sorted // MOE_TILE

    def w_lo(t, lo, hi, na, sr):
        return (layer, lo[jnp.minimum(t, na[0] - 1)], 0, 0)

    def w_hi(t, lo, hi, na, sr):
        return (layer, hi[jnp.minimum(t, na[0] - 1)], 0, 0)

    up_spec = lambda index: pl.BlockSpec((None, None, D_MODEL, D_EXPERT), index)
    down_spec = lambda index: pl.BlockSpec((None, None, D_EXPERT, D_MODEL), index)
    return pl.pallas_call(
        _moe_kernel,
        grid_spec=pltpu.PrefetchScalarGridSpec(
            num_scalar_prefetch=4,
            grid=(n_tiles,),
            in_specs=[pl.BlockSpec(memory_space=pl.ANY),
                      up_spec(w_lo), up_spec(w_lo), down_spec(w_lo),
                      up_spec(w_hi), up_spec(w_hi), down_spec(w_hi)],
            out_specs=pl.BlockSpec((MOE_TILE, D_MODEL), lambda t, *_: (t, 0)),
            scratch_shapes=[pltpu.VMEM((2, MOE_TILE, HX_W), F32),
                            pltpu.SemaphoreType.DMA((2,))]),
        out_shape=jax.ShapeDtypeStruct((n_sorted, D_MODEL), F32),
        compiler_params=_params("arbitrary"),
        name="moe_experts",
    )(tile_lo, tile_hi, nact, src, hx, w_gate, w_up, w_down, w_gate, w_up, w_down)


COMBINE_AHEAD = 2


def _combine_kernel(pos_ref, x_ref, cond_ref, lng_ref, lnb_ref, ys_ref, o_ref, ybuf, sems):
    wait_current, issue_next, drain = _gathered_rows(pos_ref, ys_ref, ybuf, sems, pl.program_id(0),
                                                     pl.num_programs(0))
    g_f = cond_ref[5:6, :]
    pre = DN_ALPHA * x_ref[...] + (1.0 + g_f) * wait_current()
    issue_next()
    o_ref[...] = _layer_norm(pre, lng_ref[...], lnb_ref[...])
    drain()


def _combine_call(pos, x1, cond, seq, lng, lnb, ys):
    n = x1.shape[0]
    n_t = seq // ROW_TILE
    return pl.pallas_call(
        _combine_kernel,
        grid_spec=pltpu.PrefetchScalarGridSpec(
            num_scalar_prefetch=1,
            grid=(n // ROW_TILE,),
            in_specs=[pl.BlockSpec((ROW_TILE, D_MODEL), lambda i, p: (i, 0)),
                      pl.BlockSpec((None, 6, D_MODEL), lambda i, p: (i // n_t, 0, 0)),
                      pl.BlockSpec((1, D_MODEL), lambda i, p: (0, 0)),
                      pl.BlockSpec((1, D_MODEL), lambda i, p: (0, 0)),
                      pl.BlockSpec(memory_space=pl.ANY)],
            out_specs=pl.BlockSpec((ROW_TILE, D_MODEL), lambda i, p: (i, 0)),
            scratch_shapes=[pltpu.VMEM((COMBINE_AHEAD + 1, ROW_TILE, D_MODEL), F32),
                            pltpu.SemaphoreType.DMA((COMBINE_AHEAD + 1,))]),
        out_shape=jax.ShapeDtypeStruct((n, D_MODEL), F32),
        compiler_params=_params("arbitrary"),
        name="moe_combine",
    )(pos, x1, cond, lng, lnb, ys)


def _moe_sublayer(hx, meta_t, layer, w_gate, w_up, w_down):
    n = hx.shape[0]
    n_sorted = n + N_BUCKETS * MOE_TILE
    pos2d, counts = _rank_call(meta_t)
    pos = pos2d.reshape(n)
    cnt = counts[:N_BUCKETS, 0].astype(jnp.int32)
    padded = ((cnt + MOE_TILE - 1) // MOE_TILE) * MOE_TILE
    ends = jnp.cumsum(padded)
    starts = jnp.arange(n_sorted // MOE_TILE, dtype=jnp.int32) * MOE_TILE
    tile_bucket = jnp.minimum(jnp.sum((ends[None, :] <= starts[:, None]).astype(jnp.int32), axis=1),
                              N_BUCKETS - 1)
    tile_lo = jnp.asarray(_BUCKET_LO)[tile_bucket]
    tile_hi = jnp.asarray(_BUCKET_HI)[tile_bucket]
    nact = (ends[-1:] // MOE_TILE).astype(jnp.int32)
    src = _invert_call(pos, n_sorted)
    ys = _moe_call(tile_lo, tile_hi, nact, src, hx, layer, w_gate, w_up, w_down)
    return pos, ys


def _qkv_kernel(pos_ref, x1_ref, cond_prev_ref, cond_ref, lng_ref, lnb_ref, ys_ref, wq_t_ref, wk_ref, wv_t_ref,
                x_out_ref, q_t_ref, k_ref, v_t_ref, kmean_ref, ybuf, sems):
    wait_current, issue_next, drain = _gathered_rows(pos_ref, ys_ref, ybuf, sems, pl.program_id(0),
                                                     pl.num_programs(0))
    g_f = cond_prev_ref[5:6, :]
    pre = DN_ALPHA * x1_ref[...] + (1.0 + g_f) * wait_current()
    issue_next()
    x = _layer_norm(pre, lng_ref[...], lnb_ref[...])
    x_out_ref[...] = x
    h = (x * (1.0 + cond_ref[1:2, :]) + cond_ref[0:1, :]).astype(BF16)
    nt = (((1,), (1,)), ((), ()))
    q_t_ref[...] = lax.dot_general(wq_t_ref[...], h, nt, preferred_element_type=F32).astype(BF16)
    v_t_ref[...] = lax.dot_general(wv_t_ref[...], h, nt, preferred_element_type=F32).astype(BF16)
    k = jnp.dot(h, wk_ref[...], preferred_element_type=F32)
    k_ref[...] = k.astype(BF16)
    kmean_ref[...] = jnp.broadcast_to(jnp.mean(k, axis=0, keepdims=True), kmean_ref.shape)
    drain()


def _qkv_call(pos, x1, cond_prev, cond, lng, lnb, ys, bsz, seq, wq_t, wk, wv_t):
    n_t = seq // MOBA_BLOCK
    hd = N_HEADS * HEAD_DIM
    n = bsz * seq
    per_batch = lambda i, p: (i // n_t, 0, 0)
    return pl.pallas_call(
        _qkv_kernel,
        grid_spec=pltpu.PrefetchScalarGridSpec(
            num_scalar_prefetch=1,
            grid=(n // MOBA_BLOCK,),
            in_specs=[pl.BlockSpec((MOBA_BLOCK, D_MODEL), lambda i, p: (i, 0)),
                      pl.BlockSpec((None, 6, D_MODEL), per_batch),
                      pl.BlockSpec((None, 6, D_MODEL), per_batch),
                      pl.BlockSpec((1, D_MODEL), lambda i, p: (0, 0)),
                      pl.BlockSpec((1, D_MODEL), lambda i, p: (0, 0)),
                      pl.BlockSpec(memory_space=pl.ANY),
                      _full(wq_t.shape), _full(wk.shape), _full(wv_t.shape)],
            out_specs=(pl.BlockSpec((MOBA_BLOCK, D_MODEL), lambda i, p: (i, 0)),
                       pl.BlockSpec((None, hd, MOBA_BLOCK), lambda i, p: (i // n_t, 0, i % n_t)),
                       pl.BlockSpec((None, MOBA_BLOCK, hd), lambda i, p: (i // n_t, i % n_t, 0)),
                       pl.BlockSpec((None, hd, MOBA_BLOCK), lambda i, p: (i // n_t, 0, i % n_t)),
                       pl.BlockSpec((None, None, SUBLANES, hd), lambda i, p: (i // n_t, i % n_t, 0, 0))),
            scratch_shapes=[pltpu.VMEM((2, MOBA_BLOCK, D_MODEL), F32),
                            pltpu.SemaphoreType.DMA((2,))]),
        out_shape=(jax.ShapeDtypeStruct((n, D_MODEL), F32),
                   jax.ShapeDtypeStruct((bsz, hd, seq), BF16),
                   jax.ShapeDtypeStruct((bsz, seq, hd), BF16),
                   jax.ShapeDtypeStruct((bsz, hd, seq), BF16),
                   jax.ShapeDtypeStruct((bsz, n_t, SUBLANES, hd), F32)),
        compiler_params=_params("arbitrary"),
        name="qkv_proj",
    )(pos, x1, cond_prev, cond, lng, lnb, ys, wq_t, wk, wv_t)


def _moba_kernel(q_t_ref, k_ref, v_t_ref, kmean_ref, o_ref):
    n_blocks = kmean_ref.shape[0]
    blk = MOBA_BLOCK
    c = HEAD_DIM ** -0.5 * LOG2E
    km = kmean_ref[...]
    km_hi = km.astype(BF16)
    km_lo = (km - km_hi.astype(F32)).astype(BF16)
    causal = (lax.broadcasted_iota(jnp.int32, (blk, blk), 0)
              <= lax.broadcasted_iota(jnp.int32, (blk, blk), 1))
    bidx = lax.broadcasted_iota(jnp.int32, (n_blocks, blk), 0)

    def scores(i):
        q_t = q_t_ref[:, i * blk:(i + 1) * blk]
        return q_t, [jnp.dot(k_ref[j * blk:(j + 1) * blk, :], q_t, preferred_element_type=F32)
                     for j in range(i + 1)]

    def attend(i, q_t, s):
        s_own = jnp.where(causal, s[i], NEG_BIG)
        m = jnp.max(s_own, axis=0, keepdims=True)
        sel = []
        if i > 0:
            gate = (jnp.dot(km_hi, q_t, preferred_element_type=F32)
                    + jnp.dot(km_lo, q_t, preferred_element_type=F32))
            gate = jnp.where(bidx < i, gate, -jnp.inf)
            rank = jnp.zeros(gate.shape, F32)
            for j in range(i):
                g_j = gate[j:j + 1, :]
                beats = (g_j > gate) | ((g_j == gate) & (j < bidx))
                rank = rank + jnp.where(beats, 1.0, 0.0)
            chosen = jnp.where((bidx < i) & (rank < MOBA_TOPK), 1.0, 0.0)
            for j in range(i):
                sel_j = chosen[j:j + 1, :] > 0.5
                sel.append(sel_j)
                bm = jnp.max(s[j], axis=0, keepdims=True)
                m = jnp.maximum(m, jnp.where(sel_j, bm, NEG_BIG))
        p_own = jnp.exp2((s_own - m) * c)
        l = jnp.sum(p_own, axis=0, keepdims=True)
        acc = jnp.dot(v_t_ref[:, i * blk:(i + 1) * blk], p_own.astype(BF16), preferred_element_type=F32)
        for j in range(i):
            shift = jnp.where(sel[j], m, -NEG_BIG)
            p_j = jnp.exp2((s[j] - shift) * c)
            l = l + jnp.sum(p_j, axis=0, keepdims=True)
            acc = acc + jnp.dot(v_t_ref[:, j * blk:(j + 1) * blk], p_j.astype(BF16),
                                preferred_element_type=F32)
        o_ref[i * blk:(i + 1) * blk, :] = jnp.transpose(acc * (1.0 / l)).astype(BF16)

    nxt = scores(0)
    for i in range(n_blocks):
        cur = nxt
        if i + 1 < n_blocks:
            nxt = scores(i + 1)
        attend(i, *cur)


def _moba_call(q_t, k, v_t, kmean):
    bsz, hd, seq = q_t.shape
    n_b = seq // MOBA_BLOCK
    return pl.pallas_call(
        _moba_kernel,
        grid=(bsz, N_HEADS),
        in_specs=[pl.BlockSpec((None, HEAD_DIM, seq), lambda b, h: (b, h, 0)),
                  pl.BlockSpec((None, seq, HEAD_DIM), lambda b, h: (b, 0, h)),
                  pl.BlockSpec((None, HEAD_DIM, seq), lambda b, h: (b, h, 0)),
                  pl.BlockSpec((None, n_b, HEAD_DIM), lambda b, h: (b, 0, h))],
        out_specs=pl.BlockSpec((None, seq, HEAD_DIM), lambda b, h: (b, 0, h)),
        out_shape=jax.ShapeDtypeStruct((bsz, seq, hd), BF16),
        compiler_params=_params("arbitrary", "arbitrary"),
        name="moba_attention",
    )(q_t, k, v_t, kmean)


def _attn_out_kernel(x_ref, o_ref, cond_ref, w_out_ref, lng_ref, lnb_ref, wr_ref, br_ref,
                     x_out_ref, hx_ref, meta_t_ref):
    y = jnp.dot(o_ref[...], w_out_ref[...], preferred_element_type=F32)
    _sublayer_epilogue(x_ref[...], y, cond_ref, lng_ref, lnb_ref, wr_ref, br_ref, x_out_ref, hx_ref, meta_t_ref)


def _attn_out_call(x2d, o2d, cond, seq, w_out, lng, lnb, wr, br):
    n = x2d.shape[0]
    n_t = seq // ROW_TILE
    row = lambda i: (i, 0)
    small = [w_out, lng, lnb, wr, br]
    return pl.pallas_call(
        _attn_out_kernel,
        grid=(n // ROW_TILE,),
        in_specs=[pl.BlockSpec((ROW_TILE, D_MODEL), row),
                  pl.BlockSpec((ROW_TILE, D_MODEL), row),
                  pl.BlockSpec((None, 6, D_MODEL), lambda i: (i // n_t, 0, 0))]
                 + [_full(a.shape) for a in small],
        out_specs=(pl.BlockSpec((ROW_TILE, D_MODEL), row),
                   pl.BlockSpec((ROW_TILE, HX_W), row),
                   pl.BlockSpec((SUBLANES, ROW_TILE), lambda i: (0, i))),
        out_shape=_epilogue_out(n),
        compiler_params=_params("arbitrary"),
        name="attn_out",
    )(x2d, o2d, cond, *small)


def _router_weights(w_grp, b_grp, w_er, b_er):
    d = w_grp.shape[0]
    w_exp = jnp.transpose(w_er, (1, 0, 2)).reshape(d, N_EXPERTS)
    pad = LANES - N_GROUPS - N_EXPERTS
    w_cat = jnp.concatenate([w_grp, w_exp, jnp.zeros((d, ROUTER_LO_LANE - N_GROUPS - N_EXPERTS), F32)], axis=1)
    w_hi = w_cat.astype(BF16)
    w_lo = (w_cat - w_hi.astype(F32)).astype(BF16)
    wr = jnp.concatenate([w_hi, w_lo, jnp.zeros((d, LANES - 2 * ROUTER_LO_LANE), BF16)], axis=1)
    br = jnp.concatenate([b_grp, b_er.reshape(N_EXPERTS), jnp.zeros((pad,), F32)]).reshape(1, LANES)
    return wr, br


def kernel(x, c, ada_w, ada_b, ln_mix_g, ln_mix_b, ln_ffn_g, ln_ffn_b, ev_w_in, ev_sgu_ln_g, ev_sgu_ln_b,
           ev_w_s, ev_b_s, ev_w_dw, ev_b_dw, ev_conv_ln_g, ev_conv_ln_b, ev_w_out, od_w_qkv, od_w_out,
           moe_w_grp, moe_b_grp, moe_w_er, moe_b_er, moe_w_gate, moe_w_up, moe_w_down):
    bsz, seq, d = x.shape
    n = bsz * seq
    depth = ada_w.shape[0]
    cond_all = _cond_call(c, ada_w, ada_b).reshape(depth, bsz, 6, d)
    row = lambda v: v.reshape(1, -1)
    xc = x.reshape(n, d)
    pending = None
    for i in range(depth):
        cond = cond_all[i]
        j = i // 2
        wr, br = _router_weights(moe_w_grp[i], moe_b_grp[i], moe_w_er[i], moe_b_er[i])
        lng, lnb = row(ln_mix_g[i]), row(ln_mix_b[i])
        if i % 2 == 0:
            if pending is not None:
                xc = _combine_call(pending[1], pending[0], pending[3], seq, pending[4], pending[5], pending[2])
            gw = A_WIDTH // A_GROUPS
            bs_full = jnp.repeat(ev_b_s[j].T, gw, axis=1)
            w_dw = jnp.concatenate([ev_w_dw[j], jnp.zeros((1, B_WIDTH), F32)], axis=0)
            x1, hx, meta_t = _even_call(
                xc, cond, bsz, seq, ev_w_in[j].astype(BF16), row(ev_sgu_ln_g[j]), row(ev_sgu_ln_b[j]),
                ev_w_s[j], bs_full, w_dw, row(ev_b_dw[j]), row(ev_conv_ln_g[j]), row(ev_conv_ln_b[j]),
                ev_w_out[j].astype(BF16), lng, lnb, wr, br)
        else:
            hd = N_HEADS * HEAD_DIM
            w_qkv = od_w_qkv[j].astype(BF16)
            xc, q_t, k, v_t, kmean8 = _qkv_call(
                pending[1], pending[0], pending[3], cond, pending[4], pending[5], pending[2], bsz, seq,
                w_qkv[:, :hd].T, w_qkv[:, hd:2 * hd], w_qkv[:, 2 * hd:].T)
            o = _moba_call(q_t, k, v_t, kmean8[:, :, 0, :])
            x1, hx, meta_t = _attn_out_call(xc, o.reshape(n, hd), cond, seq, od_w_out[j].astype(BF16),
                                            lng, lnb, wr, br)
        pos, ys = _moe_sublayer(hx, meta_t, i, moe_w_gate, moe_w_up, moe_w_down)
        pending = (x1, pos, ys, cond, row(ln_ffn_g[i]), row(ln_ffn_b[i]))
    xc = _combine_call(pending[1], pending[0], pending[3], seq, pending[4], pending[5], pending[2])
    return xc.reshape(bsz, seq, d)
```

```python
import functools

import jax
import jax.numpy as jnp
import numpy as np
from jax import lax
from jax.experimental import pallas as pl
from jax.experimental.pallas import tpu as pltpu

D_MODEL = 1024
A_WIDTH = 512
A_GROUPS = 4
A_CHUNK = 128
B_WIDTH = 512
CONV_WIDTH = 31
N_HEADS = 8
HEAD_DIM = D_MODEL // N_HEADS
MOBA_BLOCK = 256
MOBA_TOPK = 3
N_GROUPS = 4
EXPERTS_PER_GROUP = 4
N_EXPERTS = N_GROUPS * EXPERTS_PER_GROUP
D_EXPERT = 512
DEPTH = 2
DN_ALPHA = (2.0 * DEPTH) ** 0.25
LN_EPS = 1e-5

LANES = 128
SUBLANES = 8
ROW_TILE = 256
MOE_TILE = 256
CONV_HALO = 32
PAIRS_PER_GROUP = EXPERTS_PER_GROUP * (EXPERTS_PER_GROUP - 1) // 2
N_BUCKETS = N_GROUPS * PAIRS_PER_GROUP
BUCKET_ROWS = 32
ROUTER_LO_LANE = 32
META_W = LANES
HX_W = D_MODEL + META_W
GATHER_AHEAD = 2
VMEM_LIMIT = 56 * 1024 * 1024
NEG_BIG = -1e30
LOG2E = 1.4426950408889634
F32 = jnp.float32
BF16 = jnp.bfloat16

_PAIR_LO = np.array([0, 0, 0, 1, 1, 2], np.int32)
_PAIR_HI = np.array([1, 2, 3, 2, 3, 3], np.int32)
_BUCKET_LO = np.concatenate([g * EXPERTS_PER_GROUP + _PAIR_LO for g in range(N_GROUPS)])
_BUCKET_HI = np.concatenate([g * EXPERTS_PER_GROUP + _PAIR_HI for g in range(N_GROUPS)])


def _params(*sem):
    return pltpu.CompilerParams(dimension_semantics=sem, vmem_limit_bytes=VMEM_LIMIT)


def _layer_norm(x, g, b):
    mu = jnp.mean(x, axis=-1, keepdims=True)
    xc = x - mu
    var = jnp.mean(xc * xc, axis=-1, keepdims=True)
    return xc * lax.rsqrt(var + LN_EPS) * g + b


def _bdot(a, b):
    return jnp.dot(a.astype(BF16), b.astype(BF16), preferred_element_type=F32)


def _cond_kernel(c_ref, w_ref, b_ref, o_ref):
    c = c_ref[...]
    c_act = c * (1.0 / (1.0 + jnp.exp(-c)))
    o_ref[...] = jnp.dot(c_act, w_ref[...], precision=lax.Precision.HIGHEST,
                         preferred_element_type=F32) + b_ref[...]


def _cond_call(c, ada_w, ada_b):
    depth, d, d6 = ada_w.shape
    bsz = c.shape[0]
    n_col = d6 // d
    return pl.pallas_call(
        _cond_kernel,
        grid=(depth, n_col),
        in_specs=[pl.BlockSpec((bsz, d), lambda i, j: (0, 0)),
                  pl.BlockSpec((None, d, d), lambda i, j: (i, 0, j)),
                  pl.BlockSpec((None, 1, d), lambda i, j: (i, 0, j))],
        out_specs=pl.BlockSpec((None, bsz, d), lambda i, j: (i, 0, j)),
        out_shape=jax.ShapeDtypeStruct((depth, bsz, d6), F32),
        compiler_params=_params("arbitrary", "arbitrary"),
        name="cond",
    )(c, ada_w, ada_b.reshape(depth, 1, d6))


def _first_max(rows):
    best = functools.reduce(jnp.maximum, rows)
    idx = jnp.full(best.shape, len(rows) - 1, jnp.int32)
    for k in range(len(rows) - 2, -1, -1):
        idx = jnp.where(rows[k] == best, k, idx)
    return best, idx


def _route(lt):
    row = lambda k: lt[k:k + 1, :]
    g_rows = [row(k) for k in range(N_GROUPS)]
    gmax, gsel = _first_max(g_rows)
    g_w = 1.0 / functools.reduce(lambda a, b: a + b, [jnp.exp(g - gmax) for g in g_rows])
    e_rows = []
    for k in range(EXPERTS_PER_GROUP):
        e = row(N_GROUPS + (N_GROUPS - 1) * EXPERTS_PER_GROUP + k)
        for g in range(N_GROUPS - 2, -1, -1):
            e = jnp.where(gsel == g, row(N_GROUPS + g * EXPERTS_PER_GROUP + k), e)
        e_rows.append(e)
    v1, i1 = _first_max(e_rows)
    v2, i2 = _first_max([jnp.where(i1 == k, -jnp.inf, e) for k, e in enumerate(e_rows)])
    t = jnp.exp(v2 - v1)
    w1 = g_w / (1.0 + t)
    w2 = g_w * t / (1.0 + t)
    first_lower = i1 < i2
    a = jnp.minimum(i1, i2)
    b = jnp.maximum(i1, i2)
    pair = ((a * (2 * EXPERTS_PER_GROUP - 1 - a)) >> 1) + (b - a - 1)
    bucket = gsel * PAIRS_PER_GROUP + pair
    w_lo = jnp.where(first_lower, w1, w2)
    w_hi = jnp.where(first_lower, w2, w1)
    return bucket.astype(F32), w_lo, w_hi


def _sublayer_epilogue(x, y, cond_ref, lng_ref, lnb_ref, wr_ref, br_ref, x_out_ref, hx_ref, meta_t_ref):
    g_m = cond_ref[2:3, :]
    sh_f = cond_ref[3:4, :]
    sc_f = cond_ref[4:5, :]
    x1 = _layer_norm(DN_ALPHA * x + (1.0 + g_m) * y, lng_ref[...], lnb_ref[...])
    x_out_ref[...] = x1
    h2 = x1 * (1.0 + sc_f) + sh_f
    h_hi = h2.astype(BF16)
    h_lo = (h2 - h_hi.astype(F32)).astype(BF16)
    both = (jnp.dot(h_hi, wr_ref[...], preferred_element_type=F32)
            + jnp.dot(h_lo, wr_ref[...], preferred_element_type=F32))
    logits = both + pltpu.roll(both, LANES - ROUTER_LO_LANE, axis=1) + br_ref[...]
    bucket, w_lo, w_hi = _route(jnp.transpose(logits))
    rows = logits.shape[0]
    sub = lax.broadcasted_iota(jnp.int32, (SUBLANES, rows), 0)
    meta_t = jnp.where(sub == 0, bucket, jnp.where(sub == 1, w_lo, jnp.where(sub == 2, w_hi, 0.0)))
    meta_t_ref[...] = meta_t
    hx_ref[:, :D_MODEL] = h2
    hx_ref[:, D_MODEL:] = jnp.transpose(
        jnp.concatenate([meta_t, jnp.zeros((META_W - SUBLANES, rows), F32)], axis=0))


def _epilogue_out(n):
    shapes = (jax.ShapeDtypeStruct((n, D_MODEL), F32),
              jax.ShapeDtypeStruct((n, HX_W), F32),
              jax.ShapeDtypeStruct((SUBLANES, n), F32))
    return shapes


def _gelu(x):
    return 0.5 * x * (1.0 + lax.erf(x * (2.0 ** -0.5)))


def _even_kernel(x_ref, cond_ref, w_in_ref, sgu_g_ref, sgu_b_ref, w_s_ref, bs_ref, w_dw_ref, b_dw_ref,
                 cg_ref, cb_ref, w_out_ref, lng_ref, lnb_ref, wr_ref, br_ref,
                 x_out_ref, hx_ref, meta_t_ref, hist_ref, shift_ref):
    rows = x_ref.shape[0]
    t = pl.program_id(1)

    @pl.when(t == 0)
    def _():
        hist_ref[0:CONV_HALO, :] = jnp.zeros((CONV_HALO, B_WIDTH), F32)

    x = x_ref[...]
    h = x * (1.0 + cond_ref[1:2, :]) + cond_ref[0:1, :]
    z = _bdot(h, w_in_ref[...])

    u = _gelu(z[:, :A_WIDTH])
    v = _layer_norm(_gelu(z[:, A_WIDTH:2 * A_WIDTH]), sgu_g_ref[...], sgu_b_ref[...])
    r_i = lax.broadcasted_iota(jnp.int32, (A_CHUNK, A_CHUNK), 0)
    c_i = lax.broadcasted_iota(jnp.int32, (A_CHUNK, A_CHUNK), 1)
    gw = A_WIDTH // A_GROUPS
    s_rows = []
    for ck in range(rows // A_CHUNK):
        s_cols = []
        for g in range(A_GROUPS):
            w_g = jnp.where(c_i <= r_i, w_s_ref[g], 0.0)
            s_cols.append(_bdot(w_g, v[ck * A_CHUNK:(ck + 1) * A_CHUNK, g * gw:(g + 1) * gw]))
        s_rows.append(jnp.concatenate(s_cols, axis=1) + bs_ref[...])
    y_a = u * jnp.concatenate(s_rows, axis=0)

    zb = z[:, 2 * A_WIDTH:]
    gate = zb[:, B_WIDTH:]
    a = zb[:, :B_WIDTH] * (1.0 / (1.0 + jnp.exp(-gate)))
    hist_ref[CONV_HALO:CONV_HALO + rows, :] = a
    first = CONV_HALO - (CONV_WIDTH - 1)
    acc = jnp.zeros((rows, B_WIDTH), F32) + b_dw_ref[...]
    for r in range(SUBLANES):
        taps = [k for k in range(CONV_WIDTH) if (first + k) % SUBLANES == r]
        if not taps:
            continue
        span = max(first + k for k in taps) - r + rows
        if r == 0:
            shifted = hist_ref
        else:
            shifted = shift_ref.at[r - 1]
            shifted[0:span, :] = hist_ref[r:r + span, :]
        for k in taps:
            lo = first + k - r
            acc = acc + w_dw_ref[k:k + 1, :] * shifted[lo:lo + rows, :]
    hist_ref[0:CONV_HALO, :] = hist_ref[rows:rows + CONV_HALO, :]
    yb = _layer_norm(acc, cg_ref[...], cb_ref[...])
    y_b = yb * (1.0 / (1.0 + jnp.exp(-yb)))

    y = _bdot(jnp.concatenate([y_a, y_b], axis=1), w_out_ref[...])
    _sublayer_epilogue(x, y, cond_ref, lng_ref, lnb_ref, wr_ref, br_ref, x_out_ref, hx_ref, meta_t_ref)


def _full(shape):
    nd = len(shape)
    return pl.BlockSpec(shape, lambda *_: (0,) * nd)


def _even_call(x2d, cond, bsz, seq, w_in, sgu_g, sgu_b, w_s, bs_full, w_dw, b_dw, cg, cb, w_out,
               lng, lnb, wr, br):
    n = bsz * seq
    n_t = seq // ROW_TILE
    row = lambda b, t: (b * n_t + t, 0)
    small = [sgu_g, sgu_b, w_s, bs_full, w_dw, b_dw, cg, cb, w_out, lng, lnb, wr, br]
    return pl.pallas_call(
        _even_kernel,
        grid=(bsz, n_t),
        in_specs=[pl.BlockSpec((ROW_TILE, D_MODEL), row),
                  pl.BlockSpec((None, 6, D_MODEL), lambda b, t: (b, 0, 0)),
                  _full(w_in.shape)] + [_full(a.shape) for a in small],
        out_specs=(pl.BlockSpec((ROW_TILE, D_MODEL), row),
                   pl.BlockSpec((ROW_TILE, HX_W), row),
                   pl.BlockSpec((SUBLANES, ROW_TILE), lambda b, t: (0, b * n_t + t))),
        out_shape=_epilogue_out(n),
        scratch_shapes=[pltpu.VMEM((CONV_HALO + ROW_TILE, B_WIDTH), F32),
                        pltpu.VMEM((SUBLANES - 1, CONV_HALO + ROW_TILE, B_WIDTH), F32)],
        compiler_params=_params("arbitrary", "arbitrary"),
        name="even_mixer",
    )(x2d, cond, w_in, *small)


RANK_SUB = 256
RANK_ROWS = 2048


def _rank_kernel(meta_t_ref, pos_ref, counts_ref, run_ref):
    phase = pl.program_id(0)
    i = pl.program_id(1)
    rows = meta_t_ref.shape[1]

    @pl.when((phase == 0) & (i == 0))
    def _():
        run_ref[...] = jnp.zeros_like(run_ref)

    @pl.when((phase == 1) & (i == 0))
    def _():
        counts = run_ref[...]
        counts_ref[...] = counts
        padded = jnp.floor((counts + (MOE_TILE - 1)) * (1.0 / MOE_TILE)) * MOE_TILE
        sub_l = lax.broadcasted_iota(jnp.int32, counts.shape, 0)
        off = jnp.zeros_like(counts)
        for b in range(N_BUCKETS):
            off = off + jnp.where(sub_l > b, padded[b:b + 1, :], 0.0)
        run_ref[...] = off

    r_i = lax.broadcasted_iota(jnp.int32, (RANK_SUB, RANK_SUB), 0)
    c_i = lax.broadcasted_iota(jnp.int32, (RANK_SUB, RANK_SUB), 1)
    tri = jnp.where(r_i <= c_i, 1.0, 0.0).astype(BF16)
    sub = lax.broadcasted_iota(jnp.int32, (BUCKET_ROWS, RANK_SUB), 0)
    run = run_ref[...]
    for t in range(rows // RANK_SUB):
        bucket = meta_t_ref[0:1, t * RANK_SUB:(t + 1) * RANK_SUB].astype(jnp.int32)
        onehot = jnp.where(sub == bucket, 1.0, 0.0)
        cum = jnp.dot(onehot.astype(BF16), tri, preferred_element_type=F32)
        pos = jnp.sum(onehot * (cum + run[:, 0:1]), axis=0, keepdims=True) - 1.0
        pos_ref[:, t * RANK_SUB:(t + 1) * RANK_SUB] = pos.astype(jnp.int32)
        run = run + cum[:, RANK_SUB - 1:RANK_SUB]
    run_ref[...] = run


def _rank_call(meta_t):
    n = meta_t.shape[1]
    n_t = n // RANK_ROWS
    return pl.pallas_call(
        _rank_kernel,
        grid=(2, n_t),
        in_specs=[pl.BlockSpec((SUBLANES, RANK_ROWS), lambda p, i: (0, i))],
        out_specs=(pl.BlockSpec((1, RANK_ROWS), lambda p, i: (0, i * p)),
                   pl.BlockSpec((BUCKET_ROWS, LANES), lambda p, i: (0, 0))),
        out_shape=(jax.ShapeDtypeStruct((1, n), jnp.int32),
                   jax.ShapeDtypeStruct((BUCKET_ROWS, LANES), F32)),
        scratch_shapes=[pltpu.VMEM((BUCKET_ROWS, LANES), F32)],
        compiler_params=_params("arbitrary", "arbitrary"),
        name="moe_rank",
    )(meta_t)


def _gathered_rows(idx_ref, src_ref, buf, sems, i, n_steps):
    slots, rows = buf.shape[0], buf.shape[1]
    ahead = slots - 1

    def issue(step, slot):
        for r in range(rows):
            pltpu.make_async_copy(src_ref.at[pl.ds(idx_ref[step * rows + r], 1), :],
                                  buf.at[slot, pl.ds(r, 1), :], sems.at[slot]).start()

    def wait(slot):
        pltpu.make_async_copy(src_ref.at[pl.ds(0, rows), :], buf.at[slot], sems.at[slot]).wait()

    @pl.when(i == 0)
    def _():
        for a in range(ahead):
            issue(jnp.minimum(a, n_steps - 1), a)

    def wait_current():
        slot = lax.rem(i, slots)
        wait(slot)
        return buf[slot]

    def issue_next():
        issue(jnp.minimum(i + ahead, n_steps - 1), lax.rem(i + ahead, slots))

    def drain():
        @pl.when(i == n_steps - 1)
        def _():
            for a in range(1, slots):
                wait(lax.rem(i + a, slots))

    return wait_current, issue_next, drain


DISPATCH_ROWS = 1024


def _dispatch_kernel(pos_ref, pad_start_ref, pad_count_ref, nact_ref, hx_ref, xs_ref, zero_ref, sem, zsem):
    rows = hx_ref.shape[0]
    step = pl.program_id(0)
    base = step * rows
    n_tiles = xs_ref.shape[0] // MOE_TILE

    @pl.when(step == 0)
    def _():
        zero_ref[...] = jnp.zeros_like(zero_ref)

        def pad_row(b, r):
            return pltpu.make_async_copy(zero_ref.at[pl.ds(0, 1), :],
                                         xs_ref.at[pl.ds(pad_start_ref[b] + r, 1), :], zsem)

        def tail_tile(t):
            start = pl.multiple_of(t * MOE_TILE, MOE_TILE)
            return pltpu.make_async_copy(zero_ref, xs_ref.at[pl.ds(start, MOE_TILE), :], zsem)

        def for_all(act):
            for b in range(N_BUCKETS):
                lax.fori_loop(0, pad_count_ref[b], lambda r, c, b=b: (act(pad_row(b, r)), c)[1], 0)
            lax.fori_loop(nact_ref[0], n_tiles, lambda t, c: (act(tail_tile(t)), c)[1], 0)

        for_all(lambda cp: cp.start())
        for_all(lambda cp: cp.wait())

    for r in range(rows):
        pltpu.make_async_copy(hx_ref.at[pl.ds(r, 1), :], xs_ref.at[pl.ds(pos_ref[base + r], 1), :],
                              sem).start()
    pltpu.make_async_copy(hx_ref, xs_ref.at[pl.ds(0, rows), :], sem).wait()


def _dispatch_call(pos, pad_start, pad_count, nact, hx, n_sorted):
    n = hx.shape[0]
    return pl.pallas_call(
        _dispatch_kernel,
        grid_spec=pltpu.PrefetchScalarGridSpec(
            num_scalar_prefetch=4,
            grid=(n // DISPATCH_ROWS,),
            in_specs=[pl.BlockSpec((DISPATCH_ROWS, HX_W), lambda i, *_: (i, 0))],
            out_specs=pl.BlockSpec(memory_space=pl.ANY),
            scratch_shapes=[pltpu.VMEM((MOE_TILE, HX_W), F32),
                            pltpu.SemaphoreType.DMA(()), pltpu.SemaphoreType.DMA(())]),
        out_shape=jax.ShapeDtypeStruct((n_sorted, HX_W), F32),
        compiler_params=_params("arbitrary"),
        name="moe_dispatch",
    )(pos, pad_start, pad_count, nact, hx)


def _moe_kernel(lo_ref, hi_ref, nact_ref, xs_ref, wg_lo_ref, wu_lo_ref, wd_lo_ref,
                wg_hi_ref, wu_hi_ref, wd_hi_ref, ys_ref):
    del lo_ref, hi_ref

    @pl.when(pl.program_id(0) < nact_ref[0])
    def _():
        x = xs_ref[:, :D_MODEL].astype(BF16)
        w_lo = xs_ref[:, D_MODEL + 1:D_MODEL + 2]
        w_hi = xs_ref[:, D_MODEL + 2:D_MODEL + 3]

        def expert(wg_ref, wu_ref, wd_ref):
            gate = jnp.dot(x, wg_ref[...].astype(BF16), preferred_element_type=F32)
            up = jnp.dot(x, wu_ref[...].astype(BF16), preferred_element_type=F32)
            hid = gate * (1.0 / (1.0 + jnp.exp(-gate))) * up
            return jnp.dot(hid.astype(BF16), wd_ref[...].astype(BF16), preferred_element_type=F32)

        ys_ref[...] = (w_lo * expert(wg_lo_ref, wu_lo_ref, wd_lo_ref)
                       + w_hi * expert(wg_hi_ref, wu_hi_ref, wd_hi_ref))

    @pl.when(pl.program_id(0) >= nact_ref[0])
    def _():
        ys_ref[...] = jnp.zeros_like(ys_ref)


def _moe_call(tile_lo, tile_hi, nact, xs, layer, w_gate, w_up, w_down):
    n_sorted = xs.shape[0]
    n_tiles = n_sorted // MOE_TILE

    def tile(t, lo, hi, na):
        return (jnp.minimum(t, na[0] - 1), 0)

    def w_lo(t, lo, hi, na):
        return (layer, lo[jnp.minimum(t, na[0] - 1)], 0, 0)

    def w_hi(t, lo, hi, na):
        return (layer, hi[jnp.minimum(t, na[0] - 1)], 0, 0)

    up_spec = lambda index: pl.BlockSpec((None, None, D_MODEL, D_EXPERT), index)
    down_spec = lambda index: pl.BlockSpec((None, None, D_EXPERT, D_MODEL), index)
    return pl.pallas_call(
        _moe_kernel,
        grid_spec=pltpu.PrefetchScalarGridSpec(
            num_scalar_prefetch=3,
            grid=(n_tiles,),
            in_specs=[pl.BlockSpec((MOE_TILE, HX_W), tile),
                      up_spec(w_lo), up_spec(w_lo), down_spec(w_lo),
                      up_spec(w_hi), up_spec(w_hi), down_spec(w_hi)],
            out_specs=pl.BlockSpec((MOE_TILE, D_MODEL), lambda t, lo, hi, na: (t, 0))),
        out_shape=jax.ShapeDtypeStruct((n_sorted, D_MODEL), F32),
        compiler_params=_params("arbitrary"),
        name="moe_experts",
    )(tile_lo, tile_hi, nact, xs, w_gate, w_up, w_down, w_gate, w_up, w_down)


def _combine_kernel(pos_ref, x_ref, cond_ref, lng_ref, lnb_ref, ys_ref, o_ref, ybuf, sems):
    wait_current, issue_next, drain = _gathered_rows(pos_ref, ys_ref, ybuf, sems, pl.program_id(0),
                                                     pl.num_programs(0))
    g_f = cond_ref[5:6, :]
    pre = DN_ALPHA * x_ref[...] + (1.0 + g_f) * wait_current()
    issue_next()
    o_ref[...] = _layer_norm(pre, lng_ref[...], lnb_ref[...])
    drain()


def _combine_call(pos, x1, cond, seq, lng, lnb, ys):
    n = x1.shape[0]
    n_t = seq // ROW_TILE
    return pl.pallas_call(
        _combine_kernel,
        grid_spec=pltpu.PrefetchScalarGridSpec(
            num_scalar_prefetch=1,
            grid=(n // ROW_TILE,),
            in_specs=[pl.BlockSpec((ROW_TILE, D_MODEL), lambda i, p: (i, 0)),
                      pl.BlockSpec((None, 6, D_MODEL), lambda i, p: (i // n_t, 0, 0)),
                      pl.BlockSpec((1, D_MODEL), lambda i, p: (0, 0)),
                      pl.BlockSpec((1, D_MODEL), lambda i, p: (0, 0)),
                      pl.BlockSpec(memory_space=pl.ANY)],
            out_specs=pl.BlockSpec((ROW_TILE, D_MODEL), lambda i, p: (i, 0)),
            scratch_shapes=[pltpu.VMEM((GATHER_AHEAD + 1, ROW_TILE, D_MODEL), F32),
                            pltpu.SemaphoreType.DMA((GATHER_AHEAD + 1,))]),
        out_shape=jax.ShapeDtypeStruct((n, D_MODEL), F32),
        compiler_params=_params("arbitrary"),
        name="moe_combine",
    )(pos, x1, cond, lng, lnb, ys)


def _moe_sublayer(hx, meta_t, layer, w_gate, w_up, w_down):
    n = hx.shape[0]
    n_sorted = n + N_BUCKETS * MOE_TILE
    pos2d, counts = _rank_call(meta_t)
    pos = pos2d.reshape(n)
    cnt = counts[:N_BUCKETS, 0].astype(jnp.int32)
    padded = ((cnt + MOE_TILE - 1) // MOE_TILE) * MOE_TILE
    ends = jnp.cumsum(padded)
    starts = jnp.arange(n_sorted // MOE_TILE, dtype=jnp.int32) * MOE_TILE
    tile_bucket = jnp.minimum(jnp.sum((ends[None, :] <= starts[:, None]).astype(jnp.int32), axis=1),
                              N_BUCKETS - 1)
    tile_lo = jnp.asarray(_BUCKET_LO)[tile_bucket]
    tile_hi = jnp.asarray(_BUCKET_HI)[tile_bucket]
    nact = (ends[-1:] // MOE_TILE).astype(jnp.int32)
    xs = _dispatch_call(pos, ends - padded + cnt, padded - cnt, nact, hx, n_sorted)
    ys = _moe_call(tile_lo, tile_hi, nact, xs, layer, w_gate, w_up, w_down)
    return pos, ys


def _qkv_kernel(pos_ref, x1_ref, cond_prev_ref, cond_ref, lng_ref, lnb_ref, ys_ref, wq_t_ref, wk_ref, wv_t_ref,
                x_out_ref, q_t_ref, k_ref, v_t_ref, kmean_ref, ybuf, sems):
    wait_current, issue_next, drain = _gathered_rows(pos_ref, ys_ref, ybuf, sems, pl.program_id(0),
                                                     pl.num_programs(0))
    g_f = cond_prev_ref[5:6, :]
    pre = DN_ALPHA * x1_ref[...] + (1.0 + g_f) * wait_current()
    issue_next()
    x = _layer_norm(pre, lng_ref[...], lnb_ref[...])
    x_out_ref[...] = x
    h = (x * (1.0 + cond_ref[1:2, :]) + cond_ref[0:1, :]).astype(BF16)
    nt = (((1,), (1,)), ((), ()))
    q_t_ref[...] = lax.dot_general(wq_t_ref[...], h, nt, preferred_element_type=F32).astype(BF16)
    v_t_ref[...] = lax.dot_general(wv_t_ref[...], h, nt, preferred_element_type=F32).astype(BF16)
    k = jnp.dot(h, wk_ref[...], preferred_element_type=F32)
    k_ref[...] = k.astype(BF16)
    kmean_ref[...] = jnp.broadcast_to(jnp.mean(k, axis=0, keepdims=True), kmean_ref.shape)
    drain()


def _qkv_call(pos, x1, cond_prev, cond, lng, lnb, ys, bsz, seq, wq_t, wk, wv_t):
    n_t = seq // MOBA_BLOCK
    hd = N_HEADS * HEAD_DIM
    n = bsz * seq
    per_batch = lambda i, p: (i // n_t, 0, 0)
    return pl.pallas_call(
        _qkv_kernel,
        grid_spec=pltpu.PrefetchScalarGridSpec(
            num_scalar_prefetch=1,
            grid=(n // MOBA_BLOCK,),
            in_specs=[pl.BlockSpec((MOBA_BLOCK, D_MODEL), lambda i, p: (i, 0)),
                      pl.BlockSpec((None, 6, D_MODEL), per_batch),
                      pl.BlockSpec((None, 6, D_MODEL), per_batch),
                      pl.BlockSpec((1, D_MODEL), lambda i, p: (0, 0)),
                      pl.BlockSpec((1, D_MODEL), lambda i, p: (0, 0)),
                      pl.BlockSpec(memory_space=pl.ANY),
                      _full(wq_t.shape), _full(wk.shape), _full(wv_t.shape)],
            out_specs=(pl.BlockSpec((MOBA_BLOCK, D_MODEL), lambda i, p: (i, 0)),
                       pl.BlockSpec((None, hd, MOBA_BLOCK), lambda i, p: (i // n_t, 0, i % n_t)),
                       pl.BlockSpec((None, MOBA_BLOCK, hd), lambda i, p: (i // n_t, i % n_t, 0)),
                       pl.BlockSpec((None, hd, MOBA_BLOCK), lambda i, p: (i // n_t, 0, i % n_t)),
                       pl.BlockSpec((None, None, SUBLANES, hd), lambda i, p: (i // n_t, i % n_t, 0, 0))),
            scratch_shapes=[pltpu.VMEM((GATHER_AHEAD + 1, MOBA_BLOCK, D_MODEL), F32),
                            pltpu.SemaphoreType.DMA((GATHER_AHEAD + 1,))]),
        out_shape=(jax.ShapeDtypeStruct((n, D_MODEL), F32),
                   jax.ShapeDtypeStruct((bsz, hd, seq), BF16),
                   jax.ShapeDtypeStruct((bsz, seq, hd), BF16),
                   jax.ShapeDtypeStruct((bsz, hd, seq), BF16),
                   jax.ShapeDtypeStruct((bsz, n_t, SUBLANES, hd), F32)),
        compiler_params=_params("arbitrary"),
        name="qkv_proj",
    )(pos, x1, cond_prev, cond, lng, lnb, ys, wq_t, wk, wv_t)


def _moba_kernel(q_t_ref, k_ref, v_t_ref, kmean_ref, o_ref):
    n_blocks = kmean_ref.shape[0]
    blk = MOBA_BLOCK
    c = HEAD_DIM ** -0.5 * LOG2E
    km = kmean_ref[...]
    km_hi = km.astype(BF16)
    km_lo = (km - km_hi.astype(F32)).astype(BF16)
    causal = (lax.broadcasted_iota(jnp.int32, (blk, blk), 0)
              <= lax.broadcasted_iota(jnp.int32, (blk, blk), 1))
    bidx = lax.broadcasted_iota(jnp.int32, (n_blocks, blk), 0)

    def scores(i):
        q_t = q_t_ref[:, i * blk:(i + 1) * blk]
        return q_t, [jnp.dot(k_ref[j * blk:(j + 1) * blk, :], q_t, preferred_element_type=F32)
                     for j in range(i + 1)]

    def attend(i, q_t, s):
        s_own = jnp.where(causal, s[i], NEG_BIG)
        m = jnp.max(s_own, axis=0, keepdims=True)
        sel = []
        if i > 0:
            gate = (jnp.dot(km_hi, q_t, preferred_element_type=F32)
                    + jnp.dot(km_lo, q_t, preferred_element_type=F32))
            gate = jnp.where(bidx < i, gate, -jnp.inf)
            rank = jnp.zeros(gate.shape, F32)
            for j in range(i):
                g_j = gate[j:j + 1, :]
                beats = (g_j > gate) | ((g_j == gate) & (j < bidx))
                rank = rank + jnp.where(beats, 1.0, 0.0)
            chosen = jnp.where((bidx < i) & (rank < MOBA_TOPK), 1.0, 0.0)
            for j in range(i):
                sel_j = chosen[j:j + 1, :] > 0.5
                sel.append(sel_j)
                bm = jnp.max(s[j], axis=0, keepdims=True)
                m = jnp.maximum(m, jnp.where(sel_j, bm, NEG_BIG))
        p_own = jnp.exp2((s_own - m) * c)
        l = jnp.sum(p_own, axis=0, keepdims=True)
        acc = jnp.dot(v_t_ref[:, i * blk:(i + 1) * blk], p_own.astype(BF16), preferred_element_type=F32)
        for j in range(i):
            shift = jnp.where(sel[j], m, -NEG_BIG)
            p_j = jnp.exp2((s[j] - shift) * c)
            l = l + jnp.sum(p_j, axis=0, keepdims=True)
            acc = acc + jnp.dot(v_t_ref[:, j * blk:(j + 1) * blk], p_j.astype(BF16),
                                preferred_element_type=F32)
        o_ref[i * blk:(i + 1) * blk, :] = jnp.transpose(acc * (1.0 / l)).astype(BF16)

    nxt = scores(0)
    for i in range(n_blocks):
        cur = nxt
        if i + 1 < n_blocks:
            nxt = scores(i + 1)
        attend(i, *cur)


def _moba_call(q_t, k, v_t, kmean):
    bsz, hd, seq = q_t.shape
    n_b = seq // MOBA_BLOCK
    return pl.pallas_call(
        _moba_kernel,
        grid=(bsz, N_HEADS),
        in_specs=[pl.BlockSpec((None, HEAD_DIM, seq), lambda b, h: (b, h, 0)),
                  pl.BlockSpec((None, seq, HEAD_DIM), lambda b, h: (b, 0, h)),
                  pl.BlockSpec((None, HEAD_DIM, seq), lambda b, h: (b, h, 0)),
                  pl.BlockSpec((None, n_b, HEAD_DIM), lambda b, h: (b, 0, h))],
        out_specs=pl.BlockSpec((None, seq, HEAD_DIM), lambda b, h: (b, 0, h)),
        out_shape=jax.ShapeDtypeStruct((bsz, seq, hd), BF16),
        compiler_params=_params("arbitrary", "arbitrary"),
        name="moba_attention",
    )(q_t, k, v_t, kmean)


def _attn_out_kernel(x_ref, o_ref, cond_ref, w_out_ref, lng_ref, lnb_ref, wr_ref, br_ref,
                     x_out_ref, hx_ref, meta_t_ref):
    y = jnp.dot(o_ref[...], w_out_ref[...], preferred_element_type=F32)
    _sublayer_epilogue(x_ref[...], y, cond_ref, lng_ref, lnb_ref, wr_ref, br_ref, x_out_ref, hx_ref, meta_t_ref)


def _attn_out_call(x2d, o2d, cond, seq, w_out, lng, lnb, wr, br):
    n = x2d.shape[0]
    n_t = seq // ROW_TILE
    row = lambda i: (i, 0)
    small = [w_out, lng, lnb, wr, br]
    return pl.pallas_call(
        _attn_out_kernel,
        grid=(n // ROW_TILE,),
        in_specs=[pl.BlockSpec((ROW_TILE, D_MODEL), row),
                  pl.BlockSpec((ROW_TILE, D_MODEL), row),
                  pl.BlockSpec((None, 6, D_MODEL), lambda i: (i // n_t, 0, 0))]
                 + [_full(a.shape) for a in small],
        out_specs=(pl.BlockSpec((ROW_TILE, D_MODEL), row),
                   pl.BlockSpec((ROW_TILE, HX_W), row),
                   pl.BlockSpec((SUBLANES, ROW_TILE), lambda i: (0, i))),
        out_shape=_epilogue_out(n),
        compiler_params=_params("arbitrary"),
        name="attn_out",
    )(x2d, o2d, cond, *small)


def _router_weights(w_grp, b_grp, w_er, b_er):
    d = w_grp.shape[0]
    w_exp = jnp.transpose(w_er, (1, 0, 2)).reshape(d, N_EXPERTS)
    pad = LANES - N_GROUPS - N_EXPERTS
    w_cat = jnp.concatenate([w_grp, w_exp, jnp.zeros((d, ROUTER_LO_LANE - N_GROUPS - N_EXPERTS), F32)], axis=1)
    w_hi = w_cat.astype(BF16)
    w_lo = (w_cat - w_hi.astype(F32)).astype(BF16)
    wr = jnp.concatenate([w_hi, w_lo, jnp.zeros((d, LANES - 2 * ROUTER_LO_LANE), BF16)], axis=1)
    br = jnp.concatenate([b_grp, b_er.reshape(N_EXPERTS), jnp.zeros((pad,), F32)]).reshape(1, LANES)
    return wr, br


def kernel(x, c, ada_w, ada_b, ln_mix_g, ln_mix_b, ln_ffn_g, ln_ffn_b, ev_w_in, ev_sgu_ln_g, ev_sgu_ln_b,
           ev_w_s, ev_b_s, ev_w_dw, ev_b_dw, ev_conv_ln_g, ev_conv_ln_b, ev_w_out, od_w_qkv, od_w_out,
           moe_w_grp, moe_b_grp, moe_w_er, moe_b_er, moe_w_gate, moe_w_up, moe_w_down):
    bsz, seq, d = x.shape
    n = bsz * seq
    depth = ada_w.shape[0]
    cond_all = _cond_call(c, ada_w, ada_b).reshape(depth, bsz, 6, d)
    row = lambda v: v.reshape(1, -1)
    xc = x.reshape(n, d)
    pending = None
    for i in range(depth):
        cond = cond_all[i]
        j = i // 2
        wr, br = _router_weights(moe_w_grp[i], moe_b_grp[i], moe_w_er[i], moe_b_er[i])
        lng, lnb = row(ln_mix_g[i]), row(ln_mix_b[i])
        if i % 2 == 0:
            if pending is not None:
                xc = _combine_call(pending[1], pending[0], pending[3], seq, pending[4], pending[5], pending[2])
            gw = A_WIDTH // A_GROUPS
            bs_full = jnp.repeat(ev_b_s[j].T, gw, axis=1)
            w_dw = jnp.concatenate([ev_w_dw[j], jnp.zeros((1, B_WIDTH), F32)], axis=0)
            x1, hx, meta_t = _even_call(
                xc, cond, bsz, seq, ev_w_in[j].astype(BF16), row(ev_sgu_ln_g[j]), row(ev_sgu_ln_b[j]),
                ev_w_s[j], bs_full, w_dw, row(ev_b_dw[j]), row(ev_conv_ln_g[j]), row(ev_conv_ln_b[j]),
                ev_w_out[j].astype(BF16), lng, lnb, wr, br)
        else:
            hd = N_HEADS * HEAD_DIM
            w_qkv = od_w_qkv[j].astype(BF16)
            xc, q_t, k, v_t, kmean8 = _qkv_call(
                pending[1], pending[0], pending[3], cond, pending[4], pending[5], pending[2], bsz, seq,
                w_qkv[:, :hd].T, w_qkv[:, hd:2 * hd], w_qkv[:, 2 * hd:].T)
            o = _moba_call(q_t, k, v_t, kmean8[:, :, 0, :])
            x1, hx, meta_t = _attn_out_call(xc, o.reshape(n, hd), cond, seq, od_w_out[j].astype(BF16),
                                            lng, lnb, wr, br)
        pos, ys = _moe_sublayer(hx, meta_t, i, moe_w_gate, moe_w_up, moe_w_down)
        pending = (x1, pos, ys, cond, row(ln_ffn_g[i]), row(ln_ffn_b[i]))
    xc = _combine_call(pending[1], pending[0], pending[3], seq, pending[4], pending[5], pending[2])
    return xc.reshape(bsz, seq, d)
```

```python
import functools

import jax
import jax.numpy as jnp
import numpy as np
from jax import lax
from jax.experimental import pallas as pl
from jax.experimental.pallas import tpu as pltpu

D_MODEL = 1024
A_WIDTH = 512
A_GROUPS = 4
A_CHUNK = 128
B_WIDTH = 512
CONV_WIDTH = 31
N_HEADS = 8
HEAD_DIM = D_MODEL // N_HEADS
MOBA_BLOCK = 256
MOBA_TOPK = 3
N_GROUPS = 4
EXPERTS_PER_GROUP = 4
N_EXPERTS = N_GROUPS * EXPERTS_PER_GROUP
D_EXPERT = 512
DEPTH = 2
DN_ALPHA = (2.0 * DEPTH) ** 0.25
LN_EPS = 1e-5

LANES = 128
SUBLANES = 8
ROW_TILE = 256
EVEN_ROWS = 512
MOE_TILE = 256
CONV_HALO = 32
PAIRS_PER_GROUP = EXPERTS_PER_GROUP * (EXPERTS_PER_GROUP - 1) // 2
N_BUCKETS = N_GROUPS * PAIRS_PER_GROUP
BUCKET_ROWS = 32
ROUTER_LO_LANE = 32
META_W = LANES
HX_W = D_MODEL + META_W
GATHER_AHEAD = 2
VMEM_LIMIT = 56 * 1024 * 1024
NEG_BIG = -1e30
LOG2E = 1.4426950408889634
F32 = jnp.float32
BF16 = jnp.bfloat16

_PAIR_LO = np.array([0, 0, 0, 1, 1, 2], np.int32)
_PAIR_HI = np.array([1, 2, 3, 2, 3, 3], np.int32)
_BUCKET_LO = np.concatenate([g * EXPERTS_PER_GROUP + _PAIR_LO for g in range(N_GROUPS)])
_BUCKET_HI = np.concatenate([g * EXPERTS_PER_GROUP + _PAIR_HI for g in range(N_GROUPS)])


def _params(*sem):
    return pltpu.CompilerParams(dimension_semantics=sem, vmem_limit_bytes=VMEM_LIMIT)


def _layer_norm(x, g, b):
    mu = jnp.mean(x, axis=-1, keepdims=True)
    xc = x - mu
    var = jnp.mean(xc * xc, axis=-1, keepdims=True)
    return xc * lax.rsqrt(var + LN_EPS) * g + b


def _bdot(a, b):
    return jnp.dot(a.astype(BF16), b.astype(BF16), preferred_element_type=F32)


def _cond_kernel(c_ref, w_ref, b_ref, o_ref):
    c = c_ref[...]
    c_act = c * (1.0 / (1.0 + jnp.exp(-c)))
    o_ref[...] = jnp.dot(c_act, w_ref[...], precision=lax.Precision.HIGHEST,
                         preferred_element_type=F32) + b_ref[...]


def _cond_call(c, ada_w, ada_b):
    depth, d, d6 = ada_w.shape
    bsz = c.shape[0]
    n_col = d6 // d
    return pl.pallas_call(
        _cond_kernel,
        grid=(depth, n_col),
        in_specs=[pl.BlockSpec((bsz, d), lambda i, j: (0, 0)),
                  pl.BlockSpec((None, d, d), lambda i, j: (i, 0, j)),
                  pl.BlockSpec((None, 1, d), lambda i, j: (i, 0, j))],
        out_specs=pl.BlockSpec((None, bsz, d), lambda i, j: (i, 0, j)),
        out_shape=jax.ShapeDtypeStruct((depth, bsz, d6), F32),
        compiler_params=_params("arbitrary", "arbitrary"),
        name="cond",
    )(c, ada_w, ada_b.reshape(depth, 1, d6))


def _first_max(rows):
    best = functools.reduce(jnp.maximum, rows)
    idx = jnp.full(best.shape, len(rows) - 1, jnp.int32)
    for k in range(len(rows) - 2, -1, -1):
        idx = jnp.where(rows[k] == best, k, idx)
    return best, idx


def _route(lt):
    row = lambda k: lt[k:k + 1, :]
    g_rows = [row(k) for k in range(N_GROUPS)]
    gmax, gsel = _first_max(g_rows)
    g_w = 1.0 / functools.reduce(lambda a, b: a + b, [jnp.exp(g - gmax) for g in g_rows])
    e_rows = []
    for k in range(EXPERTS_PER_GROUP):
        e = row(N_GROUPS + (N_GROUPS - 1) * EXPERTS_PER_GROUP + k)
        for g in range(N_GROUPS - 2, -1, -1):
            e = jnp.where(gsel == g, row(N_GROUPS + g * EXPERTS_PER_GROUP + k), e)
        e_rows.append(e)
    v1, i1 = _first_max(e_rows)
    v2, i2 = _first_max([jnp.where(i1 == k, -jnp.inf, e) for k, e in enumerate(e_rows)])
    t = jnp.exp(v2 - v1)
    w1 = g_w / (1.0 + t)
    w2 = g_w * t / (1.0 + t)
    first_lower = i1 < i2
    a = jnp.minimum(i1, i2)
    b = jnp.maximum(i1, i2)
    pair = ((a * (2 * EXPERTS_PER_GROUP - 1 - a)) >> 1) + (b - a - 1)
    bucket = gsel * PAIRS_PER_GROUP + pair
    w_lo = jnp.where(first_lower, w1, w2)
    w_hi = jnp.where(first_lower, w2, w1)
    return bucket.astype(F32), w_lo, w_hi


def _sublayer_epilogue(x, y, cond_ref, lng_ref, lnb_ref, wr_ref, br_ref, x_out_ref, hx_ref, meta_t_ref):
    g_m = cond_ref[2:3, :]
    sh_f = cond_ref[3:4, :]
    sc_f = cond_ref[4:5, :]
    x1 = _layer_norm(DN_ALPHA * x + (1.0 + g_m) * y, lng_ref[...], lnb_ref[...])
    x_out_ref[...] = x1
    h2 = x1 * (1.0 + sc_f) + sh_f
    h_hi = h2.astype(BF16)
    h_lo = (h2 - h_hi.astype(F32)).astype(BF16)
    both = (jnp.dot(h_hi, wr_ref[...], preferred_element_type=F32)
            + jnp.dot(h_lo, wr_ref[...], preferred_element_type=F32))
    logits = both + pltpu.roll(both, LANES - ROUTER_LO_LANE, axis=1) + br_ref[...]
    bucket, w_lo, w_hi = _route(jnp.transpose(logits))
    rows = logits.shape[0]
    sub = lax.broadcasted_iota(jnp.int32, (SUBLANES, rows), 0)
    meta_t = jnp.where(sub == 0, bucket, jnp.where(sub == 1, w_lo, jnp.where(sub == 2, w_hi, 0.0)))
    meta_t_ref[...] = meta_t
    hx_ref[:, :D_MODEL] = h2
    hx_ref[:, D_MODEL:] = jnp.transpose(
        jnp.concatenate([meta_t, jnp.zeros((META_W - SUBLANES, rows), F32)], axis=0))


def _epilogue_out(n):
    shapes = (jax.ShapeDtypeStruct((n, D_MODEL), F32),
              jax.ShapeDtypeStruct((n, HX_W), F32),
              jax.ShapeDtypeStruct((SUBLANES, n), F32))
    return shapes


def _gelu(x):
    return 0.5 * x * (1.0 + lax.erf(x * (2.0 ** -0.5)))


def _even_kernel(x_ref, cond_ref, w_in_ref, sgu_g_ref, sgu_b_ref, w_s_ref, bs_ref, w_dw_ref, b_dw_ref,
                 cg_ref, cb_ref, w_out_ref, lng_ref, lnb_ref, wr_ref, br_ref,
                 x_out_ref, hx_ref, meta_t_ref, hist_ref, shift_ref):
    rows = x_ref.shape[0]
    t = pl.program_id(1)

    @pl.when(t == 0)
    def _():
        hist_ref[0:CONV_HALO, :] = jnp.zeros((CONV_HALO, B_WIDTH), F32)

    x = x_ref[...]
    h = x * (1.0 + cond_ref[1:2, :]) + cond_ref[0:1, :]
    z = _bdot(h, w_in_ref[...])

    u = _gelu(z[:, :A_WIDTH])
    v = _layer_norm(_gelu(z[:, A_WIDTH:2 * A_WIDTH]), sgu_g_ref[...], sgu_b_ref[...])
    r_i = lax.broadcasted_iota(jnp.int32, (A_CHUNK, A_CHUNK), 0)
    c_i = lax.broadcasted_iota(jnp.int32, (A_CHUNK, A_CHUNK), 1)
    gw = A_WIDTH // A_GROUPS
    w_causal = [jnp.where(c_i <= r_i, w_s_ref[g], 0.0).astype(BF16) for g in range(A_GROUPS)]
    v16 = v.astype(BF16)
    s_rows = []
    for ck in range(rows // A_CHUNK):
        s_cols = [jnp.dot(w_causal[g], v16[ck * A_CHUNK:(ck + 1) * A_CHUNK, g * gw:(g + 1) * gw],
                          preferred_element_type=F32) for g in range(A_GROUPS)]
        s_rows.append(jnp.concatenate(s_cols, axis=1) + bs_ref[...])
    y_a = u * jnp.concatenate(s_rows, axis=0)

    zb = z[:, 2 * A_WIDTH:]
    gate = zb[:, B_WIDTH:]
    a = zb[:, :B_WIDTH] * (1.0 / (1.0 + jnp.exp(-gate)))
    hist_ref[CONV_HALO:CONV_HALO + rows, :] = a
    first = CONV_HALO - (CONV_WIDTH - 1)
    acc = jnp.zeros((rows, B_WIDTH), F32) + b_dw_ref[...]
    for r in range(SUBLANES):
        taps = [k for k in range(CONV_WIDTH) if (first + k) % SUBLANES == r]
        if not taps:
            continue
        span = max(first + k for k in taps) - r + rows
        if r == 0:
            shifted = hist_ref
        else:
            shifted = shift_ref.at[r - 1]
            shifted[0:span, :] = hist_ref[r:r + span, :]
        for k in taps:
            lo = first + k - r
            acc = acc + w_dw_ref[k:k + 1, :] * shifted[lo:lo + rows, :]
    hist_ref[0:CONV_HALO, :] = hist_ref[rows:rows + CONV_HALO, :]
    yb = _layer_norm(acc, cg_ref[...], cb_ref[...])
    y_b = yb * (1.0 / (1.0 + jnp.exp(-yb)))

    y = _bdot(jnp.concatenate([y_a, y_b], axis=1), w_out_ref[...])
    _sublayer_epilogue(x, y, cond_ref, lng_ref, lnb_ref, wr_ref, br_ref, x_out_ref, hx_ref, meta_t_ref)


def _full(shape):
    nd = len(shape)
    return pl.BlockSpec(shape, lambda *_: (0,) * nd)


def _even_call(x2d, cond, bsz, seq, w_in, sgu_g, sgu_b, w_s, bs_full, w_dw, b_dw, cg, cb, w_out,
               lng, lnb, wr, br):
    n = bsz * seq
    n_t = seq // EVEN_ROWS
    row = lambda b, t: (b * n_t + t, 0)
    small = [sgu_g, sgu_b, w_s, bs_full, w_dw, b_dw, cg, cb, w_out, lng, lnb, wr, br]
    return pl.pallas_call(
        _even_kernel,
        grid=(bsz, n_t),
        in_specs=[pl.BlockSpec((EVEN_ROWS, D_MODEL), row),
                  pl.BlockSpec((None, 6, D_MODEL), lambda b, t: (b, 0, 0)),
                  _full(w_in.shape)] + [_full(a.shape) for a in small],
        out_specs=(pl.BlockSpec((EVEN_ROWS, D_MODEL), row),
                   pl.BlockSpec((EVEN_ROWS, HX_W), row),
                   pl.BlockSpec((SUBLANES, EVEN_ROWS), lambda b, t: (0, b * n_t + t))),
        out_shape=_epilogue_out(n),
        scratch_shapes=[pltpu.VMEM((CONV_HALO + EVEN_ROWS, B_WIDTH), F32),
                        pltpu.VMEM((SUBLANES - 1, CONV_HALO + EVEN_ROWS, B_WIDTH), F32)],
        compiler_params=_params("arbitrary", "arbitrary"),
        name="even_mixer",
    )(x2d, cond, w_in, *small)


RANK_SUB = 256
RANK_ROWS = 2048


def _rank_kernel(meta_t_ref, pos_ref, counts_ref, run_ref):
    phase = pl.program_id(0)
    i = pl.program_id(1)
    rows = meta_t_ref.shape[1]

    @pl.when((phase == 0) & (i == 0))
    def _():
        run_ref[...] = jnp.zeros_like(run_ref)

    @pl.when((phase == 1) & (i == 0))
    def _():
        counts = run_ref[...]
        counts_ref[...] = counts
        padded = jnp.floor((counts + (MOE_TILE - 1)) * (1.0 / MOE_TILE)) * MOE_TILE
        sub_l = lax.broadcasted_iota(jnp.int32, counts.shape, 0)
        off = jnp.zeros_like(counts)
        for b in range(N_BUCKETS):
            off = off + jnp.where(sub_l > b, padded[b:b + 1, :], 0.0)
        run_ref[...] = off

    r_i = lax.broadcasted_iota(jnp.int32, (RANK_SUB, RANK_SUB), 0)
    c_i = lax.broadcasted_iota(jnp.int32, (RANK_SUB, RANK_SUB), 1)
    tri = jnp.where(r_i <= c_i, 1.0, 0.0).astype(BF16)
    sub = lax.broadcasted_iota(jnp.int32, (BUCKET_ROWS, RANK_SUB), 0)
    run = run_ref[...]
    for t in range(rows // RANK_SUB):
        bucket = meta_t_ref[0:1, t * RANK_SUB:(t + 1) * RANK_SUB].astype(jnp.int32)
        onehot = jnp.where(sub == bucket, 1.0, 0.0)
        cum = jnp.dot(onehot.astype(BF16), tri, preferred_element_type=F32)
        pos = jnp.sum(onehot * (cum + run[:, 0:1]), axis=0, keepdims=True) - 1.0
        pos_ref[:, t * RANK_SUB:(t + 1) * RANK_SUB] = pos.astype(jnp.int32)
        run = run + cum[:, RANK_SUB - 1:RANK_SUB]
    run_ref[...] = run


def _rank_call(meta_t):
    n = meta_t.shape[1]
    n_t = n // RANK_ROWS
    return pl.pallas_call(
        _rank_kernel,
        grid=(2, n_t),
        in_specs=[pl.BlockSpec((SUBLANES, RANK_ROWS), lambda p, i: (0, i))],
        out_specs=(pl.BlockSpec((1, RANK_ROWS), lambda p, i: (0, i * p)),
                   pl.BlockSpec((BUCKET_ROWS, LANES), lambda p, i: (0, 0))),
        out_shape=(jax.ShapeDtypeStruct((1, n), jnp.int32),
                   jax.ShapeDtypeStruct((BUCKET_ROWS, LANES), F32)),
        scratch_shapes=[pltpu.VMEM((BUCKET_ROWS, LANES), F32)],
        compiler_params=_params("arbitrary", "arbitrary"),
        name="moe_rank",
    )(meta_t)


def _gathered_rows(idx_ref, src_ref, buf, sems, i, n_steps):
    slots, rows = buf.shape[0], buf.shape[1]
    ahead = slots - 1

    def issue(step, slot):
        for r in range(rows):
            pltpu.make_async_copy(src_ref.at[pl.ds(idx_ref[step * rows + r], 1), :],
                                  buf.at[slot, pl.ds(r, 1), :], sems.at[slot]).start()

    def wait(slot):
        pltpu.make_async_copy(src_ref.at[pl.ds(0, rows), :], buf.at[slot], sems.at[slot]).wait()

    @pl.when(i == 0)
    def _():
        for a in range(ahead):
            issue(jnp.minimum(a, n_steps - 1), a)

    def wait_current():
        slot = lax.rem(i, slots)
        wait(slot)
        return buf[slot]

    def issue_next():
        issue(jnp.minimum(i + ahead, n_steps - 1), lax.rem(i + ahead, slots))

    def drain():
        @pl.when(i == n_steps - 1)
        def _():
            for a in range(1, slots):
                wait(lax.rem(i + a, slots))

    return wait_current, issue_next, drain


DISPATCH_ROWS = 1024


def _dispatch_kernel(pos_ref, pad_start_ref, pad_count_ref, nact_ref, hx_ref, xs_ref, zero_ref, sem, zsem):
    rows = hx_ref.shape[0]
    step = pl.program_id(0)
    base = step * rows
    n_tiles = xs_ref.shape[0] // MOE_TILE

    @pl.when(step == 0)
    def _():
        zero_ref[...] = jnp.zeros_like(zero_ref)

        def tail_tile(t):
            start = pl.multiple_of(t * MOE_TILE, MOE_TILE)
            return pltpu.make_async_copy(zero_ref, xs_ref.at[pl.ds(start, MOE_TILE), :], zsem)

        def for_all(act):
            for b in range(N_BUCKETS):
                count = pad_count_ref[b]
                start = pad_start_ref[b]
                done = 0
                size = MOE_TILE // 2
                while size >= SUBLANES:
                    @pl.when((count & size) != 0)
                    def _(done=done, size=size):
                        off = pl.multiple_of(start + count - done - size, SUBLANES)
                        act(pltpu.make_async_copy(zero_ref.at[pl.ds(0, size), :],
                                                  xs_ref.at[pl.ds(off, size), :], zsem))
                    done = done + (count & size)
                    size //= 2
                for r in range(SUBLANES - 1):
                    @pl.when(r < (count & (SUBLANES - 1)))
                    def _(r=r):
                        act(pltpu.make_async_copy(zero_ref.at[pl.ds(0, 1), :],
                                                  xs_ref.at[pl.ds(start + r, 1), :], zsem))
            lax.fori_loop(nact_ref[0], n_tiles, lambda t, c: (act(tail_tile(t)), c)[1], 0)

        for_all(lambda cp: cp.start())
        for_all(lambda cp: cp.wait())

    for r in range(rows):
        pltpu.make_async_copy(hx_ref.at[pl.ds(r, 1), :], xs_ref.at[pl.ds(pos_ref[base + r], 1), :],
                              sem).start()
    pltpu.make_async_copy(hx_ref, xs_ref.at[pl.ds(0, rows), :], sem).wait()


def _dispatch_call(pos, pad_start, pad_count, nact, hx, n_sorted):
    n = hx.shape[0]
    return pl.pallas_call(
        _dispatch_kernel,
        grid_spec=pltpu.PrefetchScalarGridSpec(
            num_scalar_prefetch=4,
            grid=(n // DISPATCH_ROWS,),
            in_specs=[pl.BlockSpec((DISPATCH_ROWS, HX_W), lambda i, *_: (i, 0))],
            out_specs=pl.BlockSpec(memory_space=pl.ANY),
            scratch_shapes=[pltpu.VMEM((MOE_TILE, HX_W), F32),
                            pltpu.SemaphoreType.DMA(()), pltpu.SemaphoreType.DMA(())]),
        out_shape=jax.ShapeDtypeStruct((n_sorted, HX_W), F32),
        compiler_params=_params("arbitrary"),
        name="moe_dispatch",
    )(pos, pad_start, pad_count, nact, hx)


def _moe_kernel(lo_ref, hi_ref, nact_ref, xs_ref, wg_lo_ref, wu_lo_ref, wd_lo_ref,
                wg_hi_ref, wu_hi_ref, wd_hi_ref, ys_ref):
    del lo_ref, hi_ref

    @pl.when(pl.program_id(0) < nact_ref[0])
    def _():
        x = xs_ref[:, :D_MODEL].astype(BF16)
        w_lo = xs_ref[:, D_MODEL + 1:D_MODEL + 2]
        w_hi = xs_ref[:, D_MODEL + 2:D_MODEL + 3]

        def expert(wg_ref, wu_ref, wd_ref):
            gate = jnp.dot(x, wg_ref[...].astype(BF16), preferred_element_type=F32)
            up = jnp.dot(x, wu_ref[...].astype(BF16), preferred_element_type=F32)
            hid = gate * (1.0 / (1.0 + jnp.exp(-gate))) * up
            return jnp.dot(hid.astype(BF16), wd_ref[...].astype(BF16), preferred_element_type=F32)

        ys_ref[...] = (w_lo * expert(wg_lo_ref, wu_lo_ref, wd_lo_ref)
                       + w_hi * expert(wg_hi_ref, wu_hi_ref, wd_hi_ref))

    @pl.when(pl.program_id(0) >= nact_ref[0])
    def _():
        ys_ref[...] = jnp.zeros_like(ys_ref)


def _moe_call(tile_lo, tile_hi, nact, xs, layer, w_gate, w_up, w_down):
    n_sorted = xs.shape[0]
    n_tiles = n_sorted // MOE_TILE

    def tile(t, lo, hi, na):
        return (jnp.minimum(t, na[0] - 1), 0)

    def w_lo(t, lo, hi, na):
        return (layer, lo[jnp.minimum(t, na[0] - 1)], 0, 0)

    def w_hi(t, lo, hi, na):
        return (layer, hi[jnp.minimum(t, na[0] - 1)], 0, 0)

    up_spec = lambda index: pl.BlockSpec((None, None, D_MODEL, D_EXPERT), index)
    down_spec = lambda index: pl.BlockSpec((None, None, D_EXPERT, D_MODEL), index)
    return pl.pallas_call(
        _moe_kernel,
        grid_spec=pltpu.PrefetchScalarGridSpec(
            num_scalar_prefetch=3,
            grid=(n_tiles,),
            in_specs=[pl.BlockSpec((MOE_TILE, HX_W), tile),
                      up_spec(w_lo), up_spec(w_lo), down_spec(w_lo),
                      up_spec(w_hi), up_spec(w_hi), down_spec(w_hi)],
            out_specs=pl.BlockSpec((MOE_TILE, D_MODEL), lambda t, lo, hi, na: (t, 0))),
        out_shape=jax.ShapeDtypeStruct((n_sorted, D_MODEL), F32),
        compiler_params=_params("arbitrary"),
        name="moe_experts",
    )(tile_lo, tile_hi, nact, xs, w_gate, w_up, w_down, w_gate, w_up, w_down)


def _combine_kernel(pos_ref, x_ref, cond_ref, lng_ref, lnb_ref, ys_ref, o_ref, ybuf, sems):
    wait_current, issue_next, drain = _gathered_rows(pos_ref, ys_ref, ybuf, sems, pl.program_id(0),
                                                     pl.num_programs(0))
    g_f = cond_ref[5:6, :]
    pre = DN_ALPHA * x_ref[...] + (1.0 + g_f) * wait_current()
    issue_next()
    o_ref[...] = _layer_norm(pre, lng_ref[...], lnb_ref[...])
    drain()


def _combine_call(pos, x1, cond, seq, lng, lnb, ys):
    n = x1.shape[0]
    n_t = seq // ROW_TILE
    return pl.pallas_call(
        _combine_kernel,
        grid_spec=pltpu.PrefetchScalarGridSpec(
            num_scalar_prefetch=1,
            grid=(n // ROW_TILE,),
            in_specs=[pl.BlockSpec((ROW_TILE, D_MODEL), lambda i, p: (i, 0)),
                      pl.BlockSpec((None, 6, D_MODEL), lambda i, p: (i // n_t, 0, 0)),
                      pl.BlockSpec((1, D_MODEL), lambda i, p: (0, 0)),
                      pl.BlockSpec((1, D_MODEL), lambda i, p: (0, 0)),
                      pl.BlockSpec(memory_space=pl.ANY)],
            out_specs=pl.BlockSpec((ROW_TILE, D_MODEL), lambda i, p: (i, 0)),
            scratch_shapes=[pltpu.VMEM((GATHER_AHEAD + 1, ROW_TILE, D_MODEL), F32),
                            pltpu.SemaphoreType.DMA((GATHER_AHEAD + 1,))]),
        out_shape=jax.ShapeDtypeStruct((n, D_MODEL), F32),
        compiler_params=_params("arbitrary"),
        name="moe_combine",
    )(pos, x1, cond, lng, lnb, ys)


def _moe_sublayer(hx, meta_t, layer, w_gate, w_up, w_down):
    n = hx.shape[0]
    n_sorted = n + N_BUCKETS * MOE_TILE
    pos2d, counts = _rank_call(meta_t)
    pos = pos2d.reshape(n)
    cnt = counts[:N_BUCKETS, 0].astype(jnp.int32)
    padded = ((cnt + MOE_TILE - 1) // MOE_TILE) * MOE_TILE
    ends = jnp.cumsum(padded)
    starts = jnp.arange(n_sorted // MOE_TILE, dtype=jnp.int32) * MOE_TILE
    tile_bucket = jnp.minimum(jnp.sum((ends[None, :] <= starts[:, None]).astype(jnp.int32), axis=1),
                              N_BUCKETS - 1)
    tile_lo = jnp.asarray(_BUCKET_LO)[tile_bucket]
    tile_hi = jnp.asarray(_BUCKET_HI)[tile_bucket]
    nact = (ends[-1:] // MOE_TILE).astype(jnp.int32)
    xs = _dispatch_call(pos, ends - padded + cnt, padded - cnt, nact, hx, n_sorted)
    ys = _moe_call(tile_lo, tile_hi, nact, xs, layer, w_gate, w_up, w_down)
    return pos, ys


def _qkv_kernel(pos_ref, x1_ref, cond_prev_ref, cond_ref, lng_ref, lnb_ref, ys_ref, wq_t_ref, wk_ref, wv_t_ref,
                x_out_ref, q_t_ref, k_ref, v_t_ref, kmean_ref, ybuf, sems):
    wait_current, issue_next, drain = _gathered_rows(pos_ref, ys_ref, ybuf, sems, pl.program_id(0),
                                                     pl.num_programs(0))
    g_f = cond_prev_ref[5:6, :]
    pre = DN_ALPHA * x1_ref[...] + (1.0 + g_f) * wait_current()
    issue_next()
    x = _layer_norm(pre, lng_ref[...], lnb_ref[...])
    x_out_ref[...] = x
    h = (x * (1.0 + cond_ref[1:2, :]) + cond_ref[0:1, :]).astype(BF16)
    nt = (((1,), (1,)), ((), ()))
    q_t_ref[...] = lax.dot_general(wq_t_ref[...], h, nt, preferred_element_type=F32).astype(BF16)
    v_t_ref[...] = lax.dot_general(wv_t_ref[...], h, nt, preferred_element_type=F32).astype(BF16)
    k = jnp.dot(h, wk_ref[...], preferred_element_type=F32)
    k_ref[...] = k.astype(BF16)
    kmean_ref[...] = jnp.broadcast_to(jnp.mean(k, axis=0, keepdims=True), kmean_ref.shape)
    drain()


def _qkv_call(pos, x1, cond_prev, cond, lng, lnb, ys, bsz, seq, wq_t, wk, wv_t):
    n_t = seq // MOBA_BLOCK
    hd = N_HEADS * HEAD_DIM
    n = bsz * seq
    per_batch = lambda i, p: (i // n_t, 0, 0)
    return pl.pallas_call(
        _qkv_kernel,
        grid_spec=pltpu.PrefetchScalarGridSpec(
            num_scalar_prefetch=1,
            grid=(n // MOBA_BLOCK,),
            in_specs=[pl.BlockSpec((MOBA_BLOCK, D_MODEL), lambda i, p: (i, 0)),
                      pl.BlockSpec((None, 6, D_MODEL), per_batch),
                      pl.BlockSpec((None, 6, D_MODEL), per_batch),
                      pl.BlockSpec((1, D_MODEL), lambda i, p: (0, 0)),
                      pl.BlockSpec((1, D_MODEL), lambda i, p: (0, 0)),
                      pl.BlockSpec(memory_space=pl.ANY),
                      _full(wq_t.shape), _full(wk.shape), _full(wv_t.shape)],
            out_specs=(pl.BlockSpec((MOBA_BLOCK, D_MODEL), lambda i, p: (i, 0)),
                       pl.BlockSpec((None, hd, MOBA_BLOCK), lambda i, p: (i // n_t, 0, i % n_t)),
                       pl.BlockSpec((None, MOBA_BLOCK, hd), lambda i, p: (i // n_t, i % n_t, 0)),
                       pl.BlockSpec((None, hd, MOBA_BLOCK), lambda i, p: (i // n_t, 0, i % n_t)),
                       pl.BlockSpec((None, None, SUBLANES, hd), lambda i, p: (i // n_t, i % n_t, 0, 0))),
            scratch_shapes=[pltpu.VMEM((GATHER_AHEAD + 1, MOBA_BLOCK, D_MODEL), F32),
                            pltpu.SemaphoreType.DMA((GATHER_AHEAD + 1,))]),
        out_shape=(jax.ShapeDtypeStruct((n, D_MODEL), F32),
                   jax.ShapeDtypeStruct((bsz, hd, seq), BF16),
                   jax.ShapeDtypeStruct((bsz, seq, hd), BF16),
                   jax.ShapeDtypeStruct((bsz, hd, seq), BF16),
                   jax.ShapeDtypeStruct((bsz, n_t, SUBLANES, hd), F32)),
        compiler_params=_params("arbitrary"),
        name="qkv_proj",
    )(pos, x1, cond_prev, cond, lng, lnb, ys, wq_t, wk, wv_t)


def _moba_kernel(q_t_ref, k_ref, v_t_ref, kmean_ref, o_ref):
    n_blocks = kmean_ref.shape[0]
    blk = MOBA_BLOCK
    c = HEAD_DIM ** -0.5 * LOG2E
    km = kmean_ref[...]
    km_hi = km.astype(BF16)
    km_lo = (km - km_hi.astype(F32)).astype(BF16)
    causal = (lax.broadcasted_iota(jnp.int32, (blk, blk), 0)
              <= lax.broadcasted_iota(jnp.int32, (blk, blk), 1))
    bidx = lax.broadcasted_iota(jnp.int32, (n_blocks, blk), 0)

    def scores(i):
        q_t = q_t_ref[:, i * blk:(i + 1) * blk]
        return q_t, [jnp.dot(k_ref[j * blk:(j + 1) * blk, :], q_t, preferred_element_type=F32)
                     for j in range(i + 1)]

    def attend(i, q_t, s):
        s_own = jnp.where(causal, s[i], NEG_BIG)
        m = jnp.max(s_own, axis=0, keepdims=True)
        sel = []
        if i > 0:
            gate = (jnp.dot(km_hi, q_t, preferred_element_type=F32)
                    + jnp.dot(km_lo, q_t, preferred_element_type=F32))
            gate = jnp.where(bidx < i, gate, -jnp.inf)
            rank = jnp.zeros(gate.shape, F32)
            for j in range(i):
                g_j = gate[j:j + 1, :]
                beats = (g_j > gate) | ((g_j == gate) & (j < bidx))
                rank = rank + jnp.where(beats, 1.0, 0.0)
            chosen = jnp.where((bidx < i) & (rank < MOBA_TOPK), 1.0, 0.0)
            for j in range(i):
                sel_j = chosen[j:j + 1, :] > 0.5
                sel.append(sel_j)
                bm = jnp.max(s[j], axis=0, keepdims=True)
                m = jnp.maximum(m, jnp.where(sel_j, bm, NEG_BIG))
        p_own = jnp.exp2((s_own - m) * c)
        l = jnp.sum(p_own, axis=0, keepdims=True)
        acc = jnp.dot(v_t_ref[:, i * blk:(i + 1) * blk], p_own.astype(BF16), preferred_element_type=F32)
        for j in range(i):
            shift = jnp.where(sel[j], m, -NEG_BIG)
            p_j = jnp.exp2((s[j] - shift) * c)
            l = l + jnp.sum(p_j, axis=0, keepdims=True)
            acc = acc + jnp.dot(v_t_ref[:, j * blk:(j + 1) * blk], p_j.astype(BF16),
                                preferred_element_type=F32)
        o_ref[i * blk:(i + 1) * blk, :] = jnp.transpose(acc * (1.0 / l)).astype(BF16)

    nxt = scores(0)
    for i in range(n_blocks):
        cur = nxt
        if i + 1 < n_blocks:
            nxt = scores(i + 1)
        attend(i, *cur)


def _moba_call(q_t, k, v_t, kmean):
    bsz, hd, seq = q_t.shape
    n_b = seq // MOBA_BLOCK
    return pl.pallas_call(
        _moba_kernel,
        grid=(bsz, N_HEADS),
        in_specs=[pl.BlockSpec((None, HEAD_DIM, seq), lambda b, h: (b, h, 0)),
                  pl.BlockSpec((None, seq, HEAD_DIM), lambda b, h: (b, 0, h)),
                  pl.BlockSpec((None, HEAD_DIM, seq), lambda b, h: (b, h, 0)),
                  pl.BlockSpec((None, n_b, HEAD_DIM), lambda b, h: (b, 0, h))],
        out_specs=pl.BlockSpec((None, seq, HEAD_DIM), lambda b, h: (b, 0, h)),
        out_shape=jax.ShapeDtypeStruct((bsz, seq, hd), BF16),
        compiler_params=_params("arbitrary", "arbitrary"),
        name="moba_attention",
    )(q_t, k, v_t, kmean)


def _attn_out_kernel(x_ref, o_ref, cond_ref, w_out_ref, lng_ref, lnb_ref, wr_ref, br_ref,
                     x_out_ref, hx_ref, meta_t_ref):
    y = jnp.dot(o_ref[...], w_out_ref[...], preferred_element_type=F32)
    _sublayer_epilogue(x_ref[...], y, cond_ref, lng_ref, lnb_ref, wr_ref, br_ref, x_out_ref, hx_ref, meta_t_ref)


def _attn_out_call(x2d, o2d, cond, seq, w_out, lng, lnb, wr, br):
    n = x2d.shape[0]
    n_t = seq // ROW_TILE
    row = lambda i: (i, 0)
    small = [w_out, lng, lnb, wr, br]
    return pl.pallas_call(
        _attn_out_kernel,
        grid=(n // ROW_TILE,),
        in_specs=[pl.BlockSpec((ROW_TILE, D_MODEL), row),
                  pl.BlockSpec((ROW_TILE, D_MODEL), row),
                  pl.BlockSpec((None, 6, D_MODEL), lambda i: (i // n_t, 0, 0))]
                 + [_full(a.shape) for a in small],
        out_specs=(pl.BlockSpec((ROW_TILE, D_MODEL), row),
                   pl.BlockSpec((ROW_TILE, HX_W), row),
                   pl.BlockSpec((SUBLANES, ROW_TILE), lambda i: (0, i))),
        out_shape=_epilogue_out(n),
        compiler_params=_params("arbitrary"),
        name="attn_out",
    )(x2d, o2d, cond, *small)


def _router_weights(w_grp, b_grp, w_er, b_er):
    d = w_grp.shape[0]
    w_exp = jnp.transpose(w_er, (1, 0, 2)).reshape(d, N_EXPERTS)
    pad = LANES - N_GROUPS - N_EXPERTS
    w_cat = jnp.concatenate([w_grp, w_exp, jnp.zeros((d, ROUTER_LO_LANE - N_GROUPS - N_EXPERTS), F32)], axis=1)
    w_hi = w_cat.astype(BF16)
    w_lo = (w_cat - w_hi.astype(F32)).astype(BF16)
    wr = jnp.concatenate([w_hi, w_lo, jnp.zeros((d, LANES - 2 * ROUTER_LO_LANE), BF16)], axis=1)
    br = jnp.concatenate([b_grp, b_er.reshape(N_EXPERTS), jnp.zeros((pad,), F32)]).reshape(1, LANES)
    return wr, br


def kernel(x, c, ada_w, ada_b, ln_mix_g, ln_mix_b, ln_ffn_g, ln_ffn_b, ev_w_in, ev_sgu_ln_g, ev_sgu_ln_b,
           ev_w_s, ev_b_s, ev_w_dw, ev_b_dw, ev_conv_ln_g, ev_conv_ln_b, ev_w_out, od_w_qkv, od_w_out,
           moe_w_grp, moe_b_grp, moe_w_er, moe_b_er, moe_w_gate, moe_w_up, moe_w_down):
    bsz, seq, d = x.shape
    n = bsz * seq
    depth = ada_w.shape[0]
    cond_all = _cond_call(c, ada_w, ada_b).reshape(depth, bsz, 6, d)
    row = lambda v: v.reshape(1, -1)
    xc = x.reshape(n, d)
    pending = None
    for i in range(depth):
        cond = cond_all[i]
        j = i // 2
        wr, br = _router_weights(moe_w_grp[i], moe_b_grp[i], moe_w_er[i], moe_b_er[i])
        lng, lnb = row(ln_mix_g[i]), row(ln_mix_b[i])
        if i % 2 == 0:
            if pending is not None:
                xc = _combine_call(pending[1], pending[0], pending[3], seq, pending[4], pending[5], pending[2])
            gw = A_WIDTH // A_GROUPS
            bs_full = jnp.repeat(ev_b_s[j].T, gw, axis=1)
            w_dw = jnp.concatenate([ev_w_dw[j], jnp.zeros((1, B_WIDTH), F32)], axis=0)
            x1, hx, meta_t = _even_call(
                xc, cond, bsz, seq, ev_w_in[j].astype(BF16), row(ev_sgu_ln_g[j]), row(ev_sgu_ln_b[j]),
                ev_w_s[j], bs_full, w_dw, row(ev_b_dw[j]), row(ev_conv_ln_g[j]), row(ev_conv_ln_b[j]),
                ev_w_out[j].astype(BF16), lng, lnb, wr, br)
        else:
            hd = N_HEADS * HEAD_DIM
            w_qkv = od_w_qkv[j].astype(BF16)
            xc, q_t, k, v_t, kmean8 = _qkv_call(
                pending[1], pending[0], pending[3], cond, pending[4], pending[5], pending[2], bsz, seq,
                w_qkv[:, :hd].T, w_qkv[:, hd:2 * hd], w_qkv[:, 2 * hd:].T)
            o = _moba_call(q_t, k, v_t, kmean8[:, :, 0, :])
            x1, hx, meta_t = _attn_out_call(xc, o.reshape(n, hd), cond, seq, od_w_out[j].astype(BF16),
                                            lng, lnb, wr, br)
        pos, ys = _moe_sublayer(hx, meta_t, i, moe_w_gate, moe_w_up, moe_w_down)
        pending = (x1, pos, ys, cond, row(ln_ffn_g[i]), row(ln_ffn_b[i]))
    xc = _combine_call(pending[1], pending[0], pending[3], seq, pending[4], pending[5], pending[2])
    return xc.reshape(bsz, seq, d)
```

```python
import functools

import jax
import jax.numpy as jnp
import numpy as np
from jax import lax
from jax.experimental import pallas as pl
from jax.experimental.pallas import tpu as pltpu

D_MODEL = 1024
A_WIDTH = 512
A_GROUPS = 4
A_CHUNK = 128
B_WIDTH = 512
CONV_WIDTH = 31
N_HEADS = 8
HEAD_DIM = D_MODEL // N_HEADS
MOBA_BLOCK = 256
MOBA_TOPK = 3
N_GROUPS = 4
EXPERTS_PER_GROUP = 4
N_EXPERTS = N_GROUPS * EXPERTS_PER_GROUP
D_EXPERT = 512
DEPTH = 2
DN_ALPHA = (2.0 * DEPTH) ** 0.25
LN_EPS = 1e-5

LANES = 128
SUBLANES = 8
ROW_TILE = 256
EVEN_ROWS = 512
MOE_TILE = 256
CONV_HALO = 32
PAIRS_PER_GROUP = EXPERTS_PER_GROUP * (EXPERTS_PER_GROUP - 1) // 2
N_BUCKETS = N_GROUPS * PAIRS_PER_GROUP
BUCKET_ROWS = 32
ROUTER_LO_LANE = 32
META_W = LANES
HX_W = D_MODEL + META_W
GATHER_AHEAD = 2
VMEM_LIMIT = 56 * 1024 * 1024
NEG_BIG = -1e30
LOG2E = 1.4426950408889634
F32 = jnp.float32
BF16 = jnp.bfloat16

_PAIR_LO = np.array([0, 0, 1, 1, 0, 2], np.int32)
_PAIR_HI = np.array([1, 2, 2, 3, 3, 3], np.int32)
_PAIR_RANK = (0, 1, 4, 2, 3, 5)
_BUCKET_LO = np.concatenate([g * EXPERTS_PER_GROUP + _PAIR_LO for g in range(N_GROUPS)])
_BUCKET_HI = np.concatenate([g * EXPERTS_PER_GROUP + _PAIR_HI for g in range(N_GROUPS)])


def _params(*sem):
    return pltpu.CompilerParams(dimension_semantics=sem, vmem_limit_bytes=VMEM_LIMIT)


def _layer_norm(x, g, b):
    mu = jnp.mean(x, axis=-1, keepdims=True)
    xc = x - mu
    var = jnp.mean(xc * xc, axis=-1, keepdims=True)
    return xc * lax.rsqrt(var + LN_EPS) * g + b


def _bdot(a, b):
    return jnp.dot(a.astype(BF16), b.astype(BF16), preferred_element_type=F32)


def _cond_kernel(c_ref, w_ref, b_ref, o_ref):
    c = c_ref[...]
    c_act = c * (1.0 / (1.0 + jnp.exp(-c)))
    o_ref[...] = jnp.dot(c_act, w_ref[...], precision=lax.Precision.HIGHEST,
                         preferred_element_type=F32) + b_ref[...]


def _cond_call(c, ada_w, ada_b):
    depth, d, d6 = ada_w.shape
    bsz = c.shape[0]
    n_col = d6 // d
    return pl.pallas_call(
        _cond_kernel,
        grid=(depth, n_col),
        in_specs=[pl.BlockSpec((bsz, d), lambda i, j: (0, 0)),
                  pl.BlockSpec((None, d, d), lambda i, j: (i, 0, j)),
                  pl.BlockSpec((None, 1, d), lambda i, j: (i, 0, j))],
        out_specs=pl.BlockSpec((None, bsz, d), lambda i, j: (i, 0, j)),
        out_shape=jax.ShapeDtypeStruct((depth, bsz, d6), F32),
        compiler_params=_params("arbitrary", "arbitrary"),
        name="cond",
    )(c, ada_w, ada_b.reshape(depth, 1, d6))


def _first_max(rows):
    best = functools.reduce(jnp.maximum, rows)
    idx = jnp.full(best.shape, len(rows) - 1, jnp.int32)
    for k in range(len(rows) - 2, -1, -1):
        idx = jnp.where(rows[k] == best, k, idx)
    return best, idx


def _route(lt):
    row = lambda k: lt[k:k + 1, :]
    g_rows = [row(k) for k in range(N_GROUPS)]
    gmax, gsel = _first_max(g_rows)
    g_w = 1.0 / functools.reduce(lambda a, b: a + b, [jnp.exp(g - gmax) for g in g_rows])
    e_rows = []
    for k in range(EXPERTS_PER_GROUP):
        e = row(N_GROUPS + (N_GROUPS - 1) * EXPERTS_PER_GROUP + k)
        for g in range(N_GROUPS - 2, -1, -1):
            e = jnp.where(gsel == g, row(N_GROUPS + g * EXPERTS_PER_GROUP + k), e)
        e_rows.append(e)
    v1, i1 = _first_max(e_rows)
    v2, i2 = _first_max([jnp.where(i1 == k, -jnp.inf, e) for k, e in enumerate(e_rows)])
    t = jnp.exp(v2 - v1)
    w1 = g_w / (1.0 + t)
    w2 = g_w * t / (1.0 + t)
    first_lower = i1 < i2
    a = jnp.minimum(i1, i2)
    b = jnp.maximum(i1, i2)
    pair = ((a * (2 * EXPERTS_PER_GROUP - 1 - a)) >> 1) + (b - a - 1)
    rank = pair
    for lex, ordered in enumerate(_PAIR_RANK):
        if lex != ordered:
            rank = jnp.where(pair == lex, ordered, rank)
    bucket = gsel * PAIRS_PER_GROUP + rank
    w_lo = jnp.where(first_lower, w1, w2)
    w_hi = jnp.where(first_lower, w2, w1)
    return bucket.astype(F32), w_lo, w_hi


def _sublayer_epilogue(x, y, cond_ref, lng_ref, lnb_ref, wr_ref, br_ref, x_out_ref, hx_ref, meta_t_ref):
    g_m = cond_ref[2:3, :]
    sh_f = cond_ref[3:4, :]
    sc_f = cond_ref[4:5, :]
    x1 = _layer_norm(DN_ALPHA * x + (1.0 + g_m) * y, lng_ref[...], lnb_ref[...])
    x_out_ref[...] = x1
    h2 = x1 * (1.0 + sc_f) + sh_f
    h_hi = h2.astype(BF16)
    h_lo = (h2 - h_hi.astype(F32)).astype(BF16)
    both = (jnp.dot(h_hi, wr_ref[...], preferred_element_type=F32)
            + jnp.dot(h_lo, wr_ref[...], preferred_element_type=F32))
    logits = both + pltpu.roll(both, LANES - ROUTER_LO_LANE, axis=1) + br_ref[...]
    bucket, w_lo, w_hi = _route(jnp.transpose(logits))
    rows = logits.shape[0]
    sub = lax.broadcasted_iota(jnp.int32, (SUBLANES, rows), 0)
    meta_t = jnp.where(sub == 0, bucket, jnp.where(sub == 1, w_lo, jnp.where(sub == 2, w_hi, 0.0)))
    meta_t_ref[...] = meta_t
    hx_ref[:, :D_MODEL] = h2
    hx_ref[:, D_MODEL:] = jnp.transpose(
        jnp.concatenate([meta_t, jnp.zeros((META_W - SUBLANES, rows), F32)], axis=0))


def _epilogue_out(n):
    shapes = (jax.ShapeDtypeStruct((n, D_MODEL), F32),
              jax.ShapeDtypeStruct((n, HX_W), F32),
              jax.ShapeDtypeStruct((SUBLANES, n), F32))
    return shapes


def _gelu(x):
    return 0.5 * x * (1.0 + lax.erf(x * (2.0 ** -0.5)))


def _even_kernel(x_ref, cond_ref, w_in_ref, sgu_g_ref, sgu_b_ref, w_s_ref, bs_ref, w_dw_ref, b_dw_ref,
                 cg_ref, cb_ref, w_out_ref, lng_ref, lnb_ref, wr_ref, br_ref,
                 x_out_ref, hx_ref, meta_t_ref, hist_ref, shift_ref):
    rows = x_ref.shape[0]
    t = pl.program_id(1)

    @pl.when(t == 0)
    def _():
        hist_ref[0:CONV_HALO, :] = jnp.zeros((CONV_HALO, B_WIDTH), F32)

    x = x_ref[...]
    h = x * (1.0 + cond_ref[1:2, :]) + cond_ref[0:1, :]
    z = _bdot(h, w_in_ref[...])

    u = _gelu(z[:, :A_WIDTH])
    v = _layer_norm(_gelu(z[:, A_WIDTH:2 * A_WIDTH]), sgu_g_ref[...], sgu_b_ref[...])
    r_i = lax.broadcasted_iota(jnp.int32, (A_CHUNK, A_CHUNK), 0)
    c_i = lax.broadcasted_iota(jnp.int32, (A_CHUNK, A_CHUNK), 1)
    gw = A_WIDTH // A_GROUPS
    w_causal = [jnp.where(c_i <= r_i, w_s_ref[g], 0.0).astype(BF16) for g in range(A_GROUPS)]
    v16 = v.astype(BF16)
    s_rows = []
    for ck in range(rows // A_CHUNK):
        s_cols = [jnp.dot(w_causal[g], v16[ck * A_CHUNK:(ck + 1) * A_CHUNK, g * gw:(g + 1) * gw],
                          preferred_element_type=F32) for g in range(A_GROUPS)]
        s_rows.append(jnp.concatenate(s_cols, axis=1) + bs_ref[...])
    y_a = u * jnp.concatenate(s_rows, axis=0)

    zb = z[:, 2 * A_WIDTH:]
    gate = zb[:, B_WIDTH:]
    a = zb[:, :B_WIDTH] * (1.0 / (1.0 + jnp.exp(-gate)))
    hist_ref[CONV_HALO:CONV_HALO + rows, :] = a
    first = CONV_HALO - (CONV_WIDTH - 1)
    acc = jnp.zeros((rows, B_WIDTH), F32) + b_dw_ref[...]
    for r in range(SUBLANES):
        taps = [k for k in range(CONV_WIDTH) if (first + k) % SUBLANES == r]
        if not taps:
            continue
        span = max(first + k for k in taps) - r + rows
        if r == 0:
            shifted = hist_ref
        else:
            shifted = shift_ref.at[r - 1]
            shifted[0:span, :] = hist_ref[r:r + span, :]
        for k in taps:
            lo = first + k - r
            acc = acc + w_dw_ref[k:k + 1, :] * shifted[lo:lo + rows, :]
    hist_ref[0:CONV_HALO, :] = hist_ref[rows:rows + CONV_HALO, :]
    yb = _layer_norm(acc, cg_ref[...], cb_ref[...])
    y_b = yb * (1.0 / (1.0 + jnp.exp(-yb)))

    y = _bdot(jnp.concatenate([y_a, y_b], axis=1), w_out_ref[...])
    _sublayer_epilogue(x, y, cond_ref, lng_ref, lnb_ref, wr_ref, br_ref, x_out_ref, hx_ref, meta_t_ref)


def _full(shape):
    nd = len(shape)
    return pl.BlockSpec(shape, lambda *_: (0,) * nd)


def _even_call(x2d, cond, bsz, seq, w_in, sgu_g, sgu_b, w_s, bs_full, w_dw, b_dw, cg, cb, w_out,
               lng, lnb, wr, br):
    n = bsz * seq
    n_t = seq // EVEN_ROWS
    row = lambda b, t: (b * n_t + t, 0)
    small = [sgu_g, sgu_b, w_s, bs_full, w_dw, b_dw, cg, cb, w_out, lng, lnb, wr, br]
    return pl.pallas_call(
        _even_kernel,
        grid=(bsz, n_t),
        in_specs=[pl.BlockSpec((EVEN_ROWS, D_MODEL), row),
                  pl.BlockSpec((None, 6, D_MODEL), lambda b, t: (b, 0, 0)),
                  _full(w_in.shape)] + [_full(a.shape) for a in small],
        out_specs=(pl.BlockSpec((EVEN_ROWS, D_MODEL), row),
                   pl.BlockSpec((EVEN_ROWS, HX_W), row),
                   pl.BlockSpec((SUBLANES, EVEN_ROWS), lambda b, t: (0, b * n_t + t))),
        out_shape=_epilogue_out(n),
        scratch_shapes=[pltpu.VMEM((CONV_HALO + EVEN_ROWS, B_WIDTH), F32),
                        pltpu.VMEM((SUBLANES - 1, CONV_HALO + EVEN_ROWS, B_WIDTH), F32)],
        compiler_params=_params("arbitrary", "arbitrary"),
        name="even_mixer",
    )(x2d, cond, w_in, *small)


RANK_SUB = 256
RANK_ROWS = 2048


def _rank_kernel(meta_t_ref, pos_ref, counts_ref, run_ref):
    phase = pl.program_id(0)
    i = pl.program_id(1)
    rows = meta_t_ref.shape[1]

    @pl.when((phase == 0) & (i == 0))
    def _():
        run_ref[...] = jnp.zeros_like(run_ref)

    @pl.when((phase == 1) & (i == 0))
    def _():
        counts = run_ref[...]
        counts_ref[...] = counts
        padded = jnp.floor((counts + (MOE_TILE - 1)) * (1.0 / MOE_TILE)) * MOE_TILE
        sub_l = lax.broadcasted_iota(jnp.int32, counts.shape, 0)
        off = jnp.zeros_like(counts)
        for b in range(N_BUCKETS):
            off = off + jnp.where(sub_l > b, padded[b:b + 1, :], 0.0)
        run_ref[...] = off

    r_i = lax.broadcasted_iota(jnp.int32, (RANK_SUB, RANK_SUB), 0)
    c_i = lax.broadcasted_iota(jnp.int32, (RANK_SUB, RANK_SUB), 1)
    tri = jnp.where(r_i <= c_i, 1.0, 0.0).astype(BF16)
    sub = lax.broadcasted_iota(jnp.int32, (BUCKET_ROWS, RANK_SUB), 0)
    run = run_ref[...]
    for t in range(rows // RANK_SUB):
        bucket = meta_t_ref[0:1, t * RANK_SUB:(t + 1) * RANK_SUB].astype(jnp.int32)
        onehot = jnp.where(sub == bucket, 1.0, 0.0)
        cum = jnp.dot(onehot.astype(BF16), tri, preferred_element_type=F32)
        pos = jnp.sum(onehot * (cum + run[:, 0:1]), axis=0, keepdims=True) - 1.0
        pos_ref[:, t * RANK_SUB:(t + 1) * RANK_SUB] = pos.astype(jnp.int32)
        run = run + cum[:, RANK_SUB - 1:RANK_SUB]
    run_ref[...] = run


def _rank_call(meta_t):
    n = meta_t.shape[1]
    n_t = n // RANK_ROWS
    return pl.pallas_call(
        _rank_kernel,
        grid=(2, n_t),
        in_specs=[pl.BlockSpec((SUBLANES, RANK_ROWS), lambda p, i: (0, i))],
        out_specs=(pl.BlockSpec((1, RANK_ROWS), lambda p, i: (0, i * p)),
                   pl.BlockSpec((BUCKET_ROWS, LANES), lambda p, i: (0, 0))),
        out_shape=(jax.ShapeDtypeStruct((1, n), jnp.int32),
                   jax.ShapeDtypeStruct((BUCKET_ROWS, LANES), F32)),
        scratch_shapes=[pltpu.VMEM((BUCKET_ROWS, LANES), F32)],
        compiler_params=_params("arbitrary", "arbitrary"),
        name="moe_rank",
    )(meta_t)


def _gathered_rows(idx_ref, src_ref, buf, sems, i, n_steps):
    slots, rows = buf.shape[0], buf.shape[1]
    ahead = slots - 1

    def issue(step, slot):
        for r in range(rows):
            pltpu.make_async_copy(src_ref.at[pl.ds(idx_ref[step * rows + r], 1), :],
                                  buf.at[slot, pl.ds(r, 1), :], sems.at[slot]).start()

    def wait(slot):
        pltpu.make_async_copy(src_ref.at[pl.ds(0, rows), :], buf.at[slot], sems.at[slot]).wait()

    @pl.when(i == 0)
    def _():
        for a in range(ahead):
            issue(jnp.minimum(a, n_steps - 1), a)

    def wait_current():
        slot = lax.rem(i, slots)
        wait(slot)
        return buf[slot]

    def issue_next():
        issue(jnp.minimum(i + ahead, n_steps - 1), lax.rem(i + ahead, slots))

    def drain():
        @pl.when(i == n_steps - 1)
        def _():
            for a in range(1, slots):
                wait(lax.rem(i + a, slots))

    return wait_current, issue_next, drain


DISPATCH_ROWS = 2048


def _dispatch_kernel(pos_ref, pad_start_ref, pad_count_ref, nact_ref, hx_ref, xs_ref, zero_ref, sem, zsem):
    rows = hx_ref.shape[0]
    step = pl.program_id(0)
    base = step * rows
    n_tiles = xs_ref.shape[0] // MOE_TILE

    @pl.when(step == 0)
    def _():
        zero_ref[...] = jnp.zeros_like(zero_ref)

        def tail_tile(t):
            start = pl.multiple_of(t * MOE_TILE, MOE_TILE)
            return pltpu.make_async_copy(zero_ref, xs_ref.at[pl.ds(start, MOE_TILE), :], zsem)

        def for_all(act):
            for b in range(N_BUCKETS):
                count = pad_count_ref[b]
                start = pad_start_ref[b]
                done = 0
                size = MOE_TILE // 2
                while size >= SUBLANES:
                    @pl.when((count & size) != 0)
                    def _(done=done, size=size):
                        off = pl.multiple_of(start + count - done - size, SUBLANES)
                        act(pltpu.make_async_copy(zero_ref.at[pl.ds(0, size), :],
                                                  xs_ref.at[pl.ds(off, size), :], zsem))
                    done = done + (count & size)
                    size //= 2
                for r in range(SUBLANES - 1):
                    @pl.when(r < (count & (SUBLANES - 1)))
                    def _(r=r):
                        act(pltpu.make_async_copy(zero_ref.at[pl.ds(0, 1), :],
                                                  xs_ref.at[pl.ds(start + r, 1), :], zsem))
            lax.fori_loop(nact_ref[0], n_tiles, lambda t, c: (act(tail_tile(t)), c)[1], 0)

        for_all(lambda cp: cp.start())
        for_all(lambda cp: cp.wait())

    for r in range(rows):
        pltpu.make_async_copy(hx_ref.at[pl.ds(r, 1), :], xs_ref.at[pl.ds(pos_ref[base + r], 1), :],
                              sem).start()
    pltpu.make_async_copy(hx_ref, xs_ref.at[pl.ds(0, rows), :], sem).wait()


def _dispatch_call(pos, pad_start, pad_count, nact, hx, n_sorted):
    n = hx.shape[0]
    return pl.pallas_call(
        _dispatch_kernel,
        grid_spec=pltpu.PrefetchScalarGridSpec(
            num_scalar_prefetch=4,
            grid=(n // DISPATCH_ROWS,),
            in_specs=[pl.BlockSpec((DISPATCH_ROWS, HX_W), lambda i, *_: (i, 0))],
            out_specs=pl.BlockSpec(memory_space=pl.ANY),
            scratch_shapes=[pltpu.VMEM((MOE_TILE, HX_W), F32),
                            pltpu.SemaphoreType.DMA(()), pltpu.SemaphoreType.DMA(())]),
        out_shape=jax.ShapeDtypeStruct((n_sorted, HX_W), F32),
        compiler_params=_params("arbitrary"),
        name="moe_dispatch",
    )(pos, pad_start, pad_count, nact, hx)


def _moe_kernel(lo_ref, hi_ref, nact_ref, xs_ref, wg_lo_ref, wu_lo_ref, wd_lo_ref,
                wg_hi_ref, wu_hi_ref, wd_hi_ref, ys_ref):
    del lo_ref, hi_ref

    @pl.when(pl.program_id(0) < nact_ref[0])
    def _():
        x = xs_ref[:, :D_MODEL].astype(BF16)
        w_lo = xs_ref[:, D_MODEL + 1:D_MODEL + 2]
        w_hi = xs_ref[:, D_MODEL + 2:D_MODEL + 3]

        def expert(wg_ref, wu_ref, wd_ref):
            gate = jnp.dot(x, wg_ref[...].astype(BF16), preferred_element_type=F32)
            up = jnp.dot(x, wu_ref[...].astype(BF16), preferred_element_type=F32)
            hid = gate * (1.0 / (1.0 + jnp.exp(-gate))) * up
            return jnp.dot(hid.astype(BF16), wd_ref[...].astype(BF16), preferred_element_type=F32)

        ys_ref[...] = (w_lo * expert(wg_lo_ref, wu_lo_ref, wd_lo_ref)
                       + w_hi * expert(wg_hi_ref, wu_hi_ref, wd_hi_ref))

    @pl.when(pl.program_id(0) >= nact_ref[0])
    def _():
        ys_ref[...] = jnp.zeros_like(ys_ref)


def _moe_call(tile_lo, tile_hi, nact, xs, layer, w_gate, w_up, w_down):
    n_sorted = xs.shape[0]
    n_tiles = n_sorted // MOE_TILE

    def tile(t, lo, hi, na):
        return (jnp.minimum(t, na[0] - 1), 0)

    def w_lo(t, lo, hi, na):
        return (layer, lo[jnp.minimum(t, na[0] - 1)], 0, 0)

    def w_hi(t, lo, hi, na):
        return (layer, hi[jnp.minimum(t, na[0] - 1)], 0, 0)

    up_spec = lambda index: pl.BlockSpec((None, None, D_MODEL, D_EXPERT), index)
    down_spec = lambda index: pl.BlockSpec((None, None, D_EXPERT, D_MODEL), index)
    return pl.pallas_call(
        _moe_kernel,
        grid_spec=pltpu.PrefetchScalarGridSpec(
            num_scalar_prefetch=3,
            grid=(n_tiles,),
            in_specs=[pl.BlockSpec((MOE_TILE, HX_W), tile),
                      up_spec(w_lo), up_spec(w_lo), down_spec(w_lo),
                      up_spec(w_hi), up_spec(w_hi), down_spec(w_hi)],
            out_specs=pl.BlockSpec((MOE_TILE, D_MODEL), lambda t, lo, hi, na: (t, 0))),
        out_shape=jax.ShapeDtypeStruct((n_sorted, D_MODEL), F32),
        compiler_params=_params("arbitrary"),
        name="moe_experts",
    )(tile_lo, tile_hi, nact, xs, w_gate, w_up, w_down, w_gate, w_up, w_down)


def _combine_kernel(pos_ref, x_ref, cond_ref, lng_ref, lnb_ref, ys_ref, o_ref, ybuf, sems):
    wait_current, issue_next, drain = _gathered_rows(pos_ref, ys_ref, ybuf, sems, pl.program_id(0),
                                                     pl.num_programs(0))
    g_f = cond_ref[5:6, :]
    pre = DN_ALPHA * x_ref[...] + (1.0 + g_f) * wait_current()
    issue_next()
    o_ref[...] = _layer_norm(pre, lng_ref[...], lnb_ref[...])
    drain()


def _combine_call(pos, x1, cond, seq, lng, lnb, ys):
    n = x1.shape[0]
    n_t = seq // ROW_TILE
    return pl.pallas_call(
        _combine_kernel,
        grid_spec=pltpu.PrefetchScalarGridSpec(
            num_scalar_prefetch=1,
            grid=(n // ROW_TILE,),
            in_specs=[pl.BlockSpec((ROW_TILE, D_MODEL), lambda i, p: (i, 0)),
                      pl.BlockSpec((None, 6, D_MODEL), lambda i, p: (i // n_t, 0, 0)),
                      pl.BlockSpec((1, D_MODEL), lambda i, p: (0, 0)),
                      pl.BlockSpec((1, D_MODEL), lambda i, p: (0, 0)),
                      pl.BlockSpec(memory_space=pl.ANY)],
            out_specs=pl.BlockSpec((ROW_TILE, D_MODEL), lambda i, p: (i, 0)),
            scratch_shapes=[pltpu.VMEM((GATHER_AHEAD + 1, ROW_TILE, D_MODEL), F32),
                            pltpu.SemaphoreType.DMA((GATHER_AHEAD + 1,))]),
        out_shape=jax.ShapeDtypeStruct((n, D_MODEL), F32),
        compiler_params=_params("arbitrary"),
        name="moe_combine",
    )(pos, x1, cond, lng, lnb, ys)


def _moe_sublayer(hx, meta_t, layer, w_gate, w_up, w_down):
    n = hx.shape[0]
    n_sorted = n + N_BUCKETS * MOE_TILE
    pos2d, counts = _rank_call(meta_t)
    pos = pos2d.reshape(n)
    cnt = counts[:N_BUCKETS, 0].astype(jnp.int32)
    padded = ((cnt + MOE_TILE - 1) // MOE_TILE) * MOE_TILE
    ends = jnp.cumsum(padded)
    starts = jnp.arange(n_sorted // MOE_TILE, dtype=jnp.int32) * MOE_TILE
    tile_bucket = jnp.minimum(jnp.sum((ends[None, :] <= starts[:, None]).astype(jnp.int32), axis=1),
                              N_BUCKETS - 1)
    tile_lo = jnp.asarray(_BUCKET_LO)[tile_bucket]
    tile_hi = jnp.asarray(_BUCKET_HI)[tile_bucket]
    nact = (ends[-1:] // MOE_TILE).astype(jnp.int32)
    xs = _dispatch_call(pos, ends - padded + cnt, padded - cnt, nact, hx, n_sorted)
    ys = _moe_call(tile_lo, tile_hi, nact, xs, layer, w_gate, w_up, w_down)
    return pos, ys


def _qkv_kernel(pos_ref, x1_ref, cond_prev_ref, cond_ref, lng_ref, lnb_ref, ys_ref, wq_t_ref, wk_ref, wv_t_ref,
                x_out_ref, q_t_ref, k_ref, v_t_ref, kmean_ref, ybuf, sems):
    wait_current, issue_next, drain = _gathered_rows(pos_ref, ys_ref, ybuf, sems, pl.program_id(0),
                                                     pl.num_programs(0))
    g_f = cond_prev_ref[5:6, :]
    pre = DN_ALPHA * x1_ref[...] + (1.0 + g_f) * wait_current()
    issue_next()
    x = _layer_norm(pre, lng_ref[...], lnb_ref[...])
    x_out_ref[...] = x
    h = (x * (1.0 + cond_ref[1:2, :]) + cond_ref[0:1, :]).astype(BF16)
    nt = (((1,), (1,)), ((), ()))
    q_t_ref[...] = lax.dot_general(wq_t_ref[...], h, nt, preferred_element_type=F32).astype(BF16)
    v_t_ref[...] = lax.dot_general(wv_t_ref[...], h, nt, preferred_element_type=F32).astype(BF16)
    k = jnp.dot(h, wk_ref[...], preferred_element_type=F32)
    k_ref[...] = k.astype(BF16)
    kmean_ref[...] = jnp.broadcast_to(jnp.mean(k, axis=0, keepdims=True), kmean_ref.shape)
    drain()


def _qkv_call(pos, x1, cond_prev, cond, lng, lnb, ys, bsz, seq, wq_t, wk, wv_t):
    n_t = seq // MOBA_BLOCK
    hd = N_HEADS * HEAD_DIM
    n = bsz * seq
    per_batch = lambda i, p: (i // n_t, 0, 0)
    return pl.pallas_call(
        _qkv_kernel,
        grid_spec=pltpu.PrefetchScalarGridSpec(
            num_scalar_prefetch=1,
            grid=(n // MOBA_BLOCK,),
            in_specs=[pl.BlockSpec((MOBA_BLOCK, D_MODEL), lambda i, p: (i, 0)),
                      pl.BlockSpec((None, 6, D_MODEL), per_batch),
                      pl.BlockSpec((None, 6, D_MODEL), per_batch),
                      pl.BlockSpec((1, D_MODEL), lambda i, p: (0, 0)),
                      pl.BlockSpec((1, D_MODEL), lambda i, p: (0, 0)),
                      pl.BlockSpec(memory_space=pl.ANY),
                      _full(wq_t.shape), _full(wk.shape), _full(wv_t.shape)],
            out_specs=(pl.BlockSpec((MOBA_BLOCK, D_MODEL), lambda i, p: (i, 0)),
                       pl.BlockSpec((None, hd, MOBA_BLOCK), lambda i, p: (i // n_t, 0, i % n_t)),
                       pl.BlockSpec((None, MOBA_BLOCK, hd), lambda i, p: (i // n_t, i % n_t, 0)),
                       pl.BlockSpec((None, hd, MOBA_BLOCK), lambda i, p: (i // n_t, 0, i % n_t)),
                       pl.BlockSpec((None, None, SUBLANES, hd), lambda i, p: (i // n_t, i % n_t, 0, 0))),
            scratch_shapes=[pltpu.VMEM((GATHER_AHEAD + 1, MOBA_BLOCK, D_MODEL), F32),
                            pltpu.SemaphoreType.DMA((GATHER_AHEAD + 1,))]),
        out_shape=(jax.ShapeDtypeStruct((n, D_MODEL), F32),
                   jax.ShapeDtypeStruct((bsz, hd, seq), BF16),
                   jax.ShapeDtypeStruct((bsz, seq, hd), BF16),
                   jax.ShapeDtypeStruct((bsz, hd, seq), BF16),
                   jax.ShapeDtypeStruct((bsz, n_t, SUBLANES, hd), F32)),
        compiler_params=_params("arbitrary"),
        name="qkv_proj",
    )(pos, x1, cond_prev, cond, lng, lnb, ys, wq_t, wk, wv_t)


def _moba_kernel(q_t_ref, k_ref, v_t_ref, kmean_ref, o_ref):
    n_blocks = kmean_ref.shape[0]
    blk = MOBA_BLOCK
    c = HEAD_DIM ** -0.5 * LOG2E
    km = kmean_ref[...]
    km_hi = km.astype(BF16)
    km_lo = (km - km_hi.astype(F32)).astype(BF16)
    causal = (lax.broadcasted_iota(jnp.int32, (blk, blk), 0)
              <= lax.broadcasted_iota(jnp.int32, (blk, blk), 1))
    bidx = lax.broadcasted_iota(jnp.int32, (n_blocks, blk), 0)

    def scores(i):
        q_t = q_t_ref[:, i * blk:(i + 1) * blk]
        return q_t, [jnp.dot(k_ref[j * blk:(j + 1) * blk, :], q_t, preferred_element_type=F32)
                     for j in range(i + 1)]

    def attend(i, q_t, s):
        s_own = jnp.where(causal, s[i], NEG_BIG)
        m = jnp.max(s_own, axis=0, keepdims=True)
        sel = []
        if i > 0:
            gate = (jnp.dot(km_hi, q_t, preferred_element_type=F32)
                    + jnp.dot(km_lo, q_t, preferred_element_type=F32))
            gate = jnp.where(bidx < i, gate, -jnp.inf)
            rank = jnp.zeros(gate.shape, F32)
            for j in range(i):
                g_j = gate[j:j + 1, :]
                beats = (g_j > gate) | ((g_j == gate) & (j < bidx))
                rank = rank + jnp.where(beats, 1.0, 0.0)
            chosen = jnp.where((bidx < i) & (rank < MOBA_TOPK), 1.0, 0.0)
            for j in range(i):
                sel_j = chosen[j:j + 1, :] > 0.5
                sel.append(sel_j)
                bm = jnp.max(s[j], axis=0, keepdims=True)
                m = jnp.maximum(m, jnp.where(sel_j, bm, NEG_BIG))
        p_own = jnp.exp2((s_own - m) * c)
        l = jnp.sum(p_own, axis=0, keepdims=True)
        acc = jnp.dot(v_t_ref[:, i * blk:(i + 1) * blk], p_own.astype(BF16), preferred_element_type=F32)
        for j in range(i):
            shift = jnp.where(sel[j], m, -NEG_BIG)
            p_j = jnp.exp2((s[j] - shift) * c)
            l = l + jnp.sum(p_j, axis=0, keepdims=True)
            acc = acc + jnp.dot(v_t_ref[:, j * blk:(j + 1) * blk], p_j.astype(BF16),
                                preferred_element_type=F32)
        o_ref[i * blk:(i + 1) * blk, :] = jnp.transpose(acc * (1.0 / l)).astype(BF16)

    nxt = scores(0)
    for i in range(n_blocks):
        cur = nxt
        if i + 1 < n_blocks:
            nxt = scores(i + 1)
        attend(i, *cur)


def _moba_call(q_t, k, v_t, kmean):
    bsz, hd, seq = q_t.shape
    n_b = seq // MOBA_BLOCK
    return pl.pallas_call(
        _moba_kernel,
        grid=(bsz, N_HEADS),
        in_specs=[pl.BlockSpec((None, HEAD_DIM, seq), lambda b, h: (b, h, 0)),
                  pl.BlockSpec((None, seq, HEAD_DIM), lambda b, h: (b, 0, h)),
                  pl.BlockSpec((None, HEAD_DIM, seq), lambda b, h: (b, h, 0)),
                  pl.BlockSpec((None, n_b, HEAD_DIM), lambda b, h: (b, 0, h))],
        out_specs=pl.BlockSpec((None, seq, HEAD_DIM), lambda b, h: (b, 0, h)),
        out_shape=jax.ShapeDtypeStruct((bsz, seq, hd), BF16),
        compiler_params=_params("arbitrary", "arbitrary"),
        name="moba_attention",
    )(q_t, k, v_t, kmean)


def _attn_out_kernel(x_ref, o_ref, cond_ref, w_out_ref, lng_ref, lnb_ref, wr_ref, br_ref,
                     x_out_ref, hx_ref, meta_t_ref):
    y = jnp.dot(o_ref[...], w_out_ref[...], preferred_element_type=F32)
    _sublayer_epilogue(x_ref[...], y, cond_ref, lng_ref, lnb_ref, wr_ref, br_ref, x_out_ref, hx_ref, meta_t_ref)


def _attn_out_call(x2d, o2d, cond, seq, w_out, lng, lnb, wr, br):
    n = x2d.shape[0]
    n_t = seq // ROW_TILE
    row = lambda i: (i, 0)
    small = [w_out, lng, lnb, wr, br]
    return pl.pallas_call(
        _attn_out_kernel,
        grid=(n // ROW_TILE,),
        in_specs=[pl.BlockSpec((ROW_TILE, D_MODEL), row),
                  pl.BlockSpec((ROW_TILE, D_MODEL), row),
                  pl.BlockSpec((None, 6, D_MODEL), lambda i: (i // n_t, 0, 0))]
                 + [_full(a.shape) for a in small],
        out_specs=(pl.BlockSpec((ROW_TILE, D_MODEL), row),
                   pl.BlockSpec((ROW_TILE, HX_W), row),
                   pl.BlockSpec((SUBLANES, ROW_TILE), lambda i: (0, i))),
        out_shape=_epilogue_out(n),
        compiler_params=_params("arbitrary"),
        name="attn_out",
    )(x2d, o2d, cond, *small)


def _router_weights(w_grp, b_grp, w_er, b_er):
    d = w_grp.shape[0]
    w_exp = jnp.transpose(w_er, (1, 0, 2)).reshape(d, N_EXPERTS)
    pad = LANES - N_GROUPS - N_EXPERTS
    w_cat = jnp.concatenate([w_grp, w_exp, jnp.zeros((d, ROUTER_LO_LANE - N_GROUPS - N_EXPERTS), F32)], axis=1)
    w_hi = w_cat.astype(BF16)
    w_lo = (w_cat - w_hi.astype(F32)).astype(BF16)
    wr = jnp.concatenate([w_hi, w_lo, jnp.zeros((d, LANES - 2 * ROUTER_LO_LANE), BF16)], axis=1)
    br = jnp.concatenate([b_grp, b_er.reshape(N_EXPERTS), jnp.zeros((pad,), F32)]).reshape(1, LANES)
    return wr, br


def kernel(x, c, ada_w, ada_b, ln_mix_g, ln_mix_b, ln_ffn_g, ln_ffn_b, ev_w_in, ev_sgu_ln_g, ev_sgu_ln_b,
           ev_w_s, ev_b_s, ev_w_dw, ev_b_dw, ev_conv_ln_g, ev_conv_ln_b, ev_w_out, od_w_qkv, od_w_out,
           moe_w_grp, moe_b_grp, moe_w_er, moe_b_er, moe_w_gate, moe_w_up, moe_w_down):
    bsz, seq, d = x.shape
    n = bsz * seq
    depth = ada_w.shape[0]
    cond_all = _cond_call(c, ada_w, ada_b).reshape(depth, bsz, 6, d)
    row = lambda v: v.reshape(1, -1)
    xc = x.reshape(n, d)
    pending = None
    for i in range(depth):
        cond = cond_all[i]
        j = i // 2
        wr, br = _router_weights(moe_w_grp[i], moe_b_grp[i], moe_w_er[i], moe_b_er[i])
        lng, lnb = row(ln_mix_g[i]), row(ln_mix_b[i])
        if i % 2 == 0:
            if pending is not None:
                xc = _combine_call(pending[1], pending[0], pending[3], seq, pending[4], pending[5], pending[2])
            gw = A_WIDTH // A_GROUPS
            bs_full = jnp.repeat(ev_b_s[j].T, gw, axis=1)
            w_dw = jnp.concatenate([ev_w_dw[j], jnp.zeros((1, B_WIDTH), F32)], axis=0)
            x1, hx, meta_t = _even_call(
                xc, cond, bsz, seq, ev_w_in[j].astype(BF16), row(ev_sgu_ln_g[j]), row(ev_sgu_ln_b[j]),
                ev_w_s[j], bs_full, w_dw, row(ev_b_dw[j]), row(ev_conv_ln_g[j]), row(ev_conv_ln_b[j]),
                ev_w_out[j].astype(BF16), lng, lnb, wr, br)
        else:
            hd = N_HEADS * HEAD_DIM
            w_qkv = od_w_qkv[j].astype(BF16)
            xc, q_t, k, v_t, kmean8 = _qkv_call(
                pending[1], pending[0], pending[3], cond, pending[4], pending[5], pending[2], bsz, seq,
                w_qkv[:, :hd].T, w_qkv[:, hd:2 * hd], w_qkv[:, 2 * hd:].T)
            o = _moba_call(q_t, k, v_t, kmean8[:, :, 0, :])
            x1, hx, meta_t = _attn_out_call(xc, o.reshape(n, hd), cond, seq, od_w_out[j].astype(BF16),
                                            lng, lnb, wr, br)
        pos, ys = _moe_sublayer(hx, meta_t, i, moe_w_gate, moe_w_up, moe_w_down)
        pending = (x1, pos, ys, cond, row(ln_ffn_g[i]), row(ln_ffn_b[i]))
    xc = _combine_call(pending[1], pending[0], pending[3], seq, pending[4], pending[5], pending[2])
    return xc.reshape(bsz, seq, d)
```

```python
import functools
from typing import NamedTuple

import jax
import jax.numpy as jnp
import numpy as np
from jax import lax
from jax.experimental import pallas as pl
from jax.experimental.pallas import tpu as pltpu

D_MODEL = 1024
A_WIDTH = 512
A_GROUPS = 4
A_CHUNK = 128
B_WIDTH = 512
CONV_WIDTH = 31
N_HEADS = 8
HEAD_DIM = D_MODEL // N_HEADS
MOBA_BLOCK = 256
MOBA_TOPK = 3
N_GROUPS = 4
EXPERTS_PER_GROUP = 4
N_EXPERTS = N_GROUPS * EXPERTS_PER_GROUP
D_EXPERT = 512
DEPTH = 2
DN_ALPHA = (2.0 * DEPTH) ** 0.25
LN_EPS = 1e-5

LANES = 128
SUBLANES = 8
ROW_TILE = 256
EVEN_ROWS = 512
MOE_TILE = 256
CONV_HALO = 32
PAIRS_PER_GROUP = EXPERTS_PER_GROUP * (EXPERTS_PER_GROUP - 1) // 2
N_BUCKETS = N_GROUPS * PAIRS_PER_GROUP
BUCKET_ROWS = 32
ROUTER_LO_LANE = 32
META_W = LANES
HX_W = D_MODEL + META_W
GATHER_AHEAD = 2
VMEM_LIMIT = 56 * 1024 * 1024
NEG_BIG = -1e30
LOG2E = 1.4426950408889634
F32 = jnp.float32
BF16 = jnp.bfloat16

_PAIR_LO = np.array([0, 0, 1, 1, 0, 2], np.int32)
_PAIR_HI = np.array([1, 2, 2, 3, 3, 3], np.int32)
_PAIR_RANK = (0, 1, 4, 2, 3, 5)
_BUCKET_LO = np.concatenate([g * EXPERTS_PER_GROUP + _PAIR_LO for g in range(N_GROUPS)])
_BUCKET_HI = np.concatenate([g * EXPERTS_PER_GROUP + _PAIR_HI for g in range(N_GROUPS)])


def _params(*sem):
    return pltpu.CompilerParams(dimension_semantics=sem, vmem_limit_bytes=VMEM_LIMIT)


def _layer_norm(x, g, b):
    mu = jnp.mean(x, axis=-1, keepdims=True)
    xc = x - mu
    var = jnp.mean(xc * xc, axis=-1, keepdims=True)
    return xc * lax.rsqrt(var + LN_EPS) * g + b


def _bdot(a, b):
    return jnp.dot(a.astype(BF16), b.astype(BF16), preferred_element_type=F32)


def _cond_kernel(c_ref, w_ref, b_ref, o_ref):
    c = c_ref[...]
    c_act = c * (1.0 / (1.0 + jnp.exp(-c)))
    o_ref[...] = jnp.dot(c_act, w_ref[...], precision=lax.Precision.HIGHEST,
                         preferred_element_type=F32) + b_ref[...]


def _cond_call(c, ada_w, ada_b):
    depth, d, d6 = ada_w.shape
    bsz = c.shape[0]
    n_col = d6 // d
    return pl.pallas_call(
        _cond_kernel,
        grid=(depth, n_col),
        in_specs=[pl.BlockSpec((bsz, d), lambda i, j: (0, 0)),
                  pl.BlockSpec((None, d, d), lambda i, j: (i, 0, j)),
                  pl.BlockSpec((None, 1, d), lambda i, j: (i, 0, j))],
        out_specs=pl.BlockSpec((None, bsz, d), lambda i, j: (i, 0, j)),
        out_shape=jax.ShapeDtypeStruct((depth, bsz, d6), F32),
        compiler_params=_params("arbitrary", "arbitrary"),
        name="cond",
    )(c, ada_w, ada_b.reshape(depth, 1, d6))


def _first_max(rows):
    best = functools.reduce(jnp.maximum, rows)
    idx = jnp.full(best.shape, len(rows) - 1, jnp.int32)
    for k in range(len(rows) - 2, -1, -1):
        idx = jnp.where(rows[k] == best, k, idx)
    return best, idx


def _route(lt):
    row = lambda k: lt[k:k + 1, :]
    g_rows = [row(k) for k in range(N_GROUPS)]
    gmax, gsel = _first_max(g_rows)
    g_w = 1.0 / functools.reduce(lambda a, b: a + b, [jnp.exp(g - gmax) for g in g_rows])
    e_rows = []
    for k in range(EXPERTS_PER_GROUP):
        e = row(N_GROUPS + (N_GROUPS - 1) * EXPERTS_PER_GROUP + k)
        for g in range(N_GROUPS - 2, -1, -1):
            e = jnp.where(gsel == g, row(N_GROUPS + g * EXPERTS_PER_GROUP + k), e)
        e_rows.append(e)
    v1, i1 = _first_max(e_rows)
    v2, i2 = _first_max([jnp.where(i1 == k, -jnp.inf, e) for k, e in enumerate(e_rows)])
    t = jnp.exp(v2 - v1)
    w1 = g_w / (1.0 + t)
    w2 = g_w * t / (1.0 + t)
    first_lower = i1 < i2
    a = jnp.minimum(i1, i2)
    b = jnp.maximum(i1, i2)
    pair = ((a * (2 * EXPERTS_PER_GROUP - 1 - a)) >> 1) + (b - a - 1)
    rank = pair
    for lex, ordered in enumerate(_PAIR_RANK):
        if lex != ordered:
            rank = jnp.where(pair == lex, ordered, rank)
    bucket = gsel * PAIRS_PER_GROUP + rank
    w_lo = jnp.where(first_lower, w1, w2)
    w_hi = jnp.where(first_lower, w2, w1)
    return bucket.astype(F32), w_lo, w_hi


def _sublayer_epilogue(x, y, cond_ref, lng_ref, lnb_ref, wr_ref, br_ref, x_out_ref, hx_ref, meta_t_ref):
    g_m = cond_ref[2:3, :]
    sh_f = cond_ref[3:4, :]
    sc_f = cond_ref[4:5, :]
    x1 = _layer_norm(DN_ALPHA * x + (1.0 + g_m) * y, lng_ref[...], lnb_ref[...])
    x_out_ref[...] = x1
    h2 = x1 * (1.0 + sc_f) + sh_f
    h_hi = h2.astype(BF16)
    h_lo = (h2 - h_hi.astype(F32)).astype(BF16)
    both = (jnp.dot(h_hi, wr_ref[...], preferred_element_type=F32)
            + jnp.dot(h_lo, wr_ref[...], preferred_element_type=F32))
    logits = both + pltpu.roll(both, LANES - ROUTER_LO_LANE, axis=1) + br_ref[...]
    bucket, w_lo, w_hi = _route(jnp.transpose(logits))
    rows = logits.shape[0]
    sub = lax.broadcasted_iota(jnp.int32, (SUBLANES, rows), 0)
    meta_t = jnp.where(sub == 0, bucket, jnp.where(sub == 1, w_lo, jnp.where(sub == 2, w_hi, 0.0)))
    meta_t_ref[...] = meta_t
    hx_ref[:, :D_MODEL] = h2
    hx_ref[:, D_MODEL:] = jnp.transpose(
        jnp.concatenate([meta_t, jnp.zeros((META_W - SUBLANES, rows), F32)], axis=0))


def _epilogue_out(n):
    shapes = (jax.ShapeDtypeStruct((n, D_MODEL), F32),
              jax.ShapeDtypeStruct((n, HX_W), F32),
              jax.ShapeDtypeStruct((SUBLANES, n), F32))
    return shapes


def _gelu(x):
    return 0.5 * x * (1.0 + lax.erf(x * (2.0 ** -0.5)))


def _even_kernel(x_ref, cond_ref, w_in_ref, sgu_g_ref, sgu_b_ref, w_s_ref, bs_ref, w_dw_ref, b_dw_ref,
                 cg_ref, cb_ref, w_out_ref, lng_ref, lnb_ref, wr_ref, br_ref,
                 x_out_ref, hx_ref, meta_t_ref, hist_ref, shift_ref):
    rows = x_ref.shape[0]
    t = pl.program_id(1)

    @pl.when(t == 0)
    def _():
        hist_ref[0:CONV_HALO, :] = jnp.zeros((CONV_HALO, B_WIDTH), F32)

    x = x_ref[...]
    h = x * (1.0 + cond_ref[1:2, :]) + cond_ref[0:1, :]
    z = _bdot(h, w_in_ref[...])

    u = _gelu(z[:, :A_WIDTH])
    v = _layer_norm(_gelu(z[:, A_WIDTH:2 * A_WIDTH]), sgu_g_ref[...], sgu_b_ref[...])
    r_i = lax.broadcasted_iota(jnp.int32, (A_CHUNK, A_CHUNK), 0)
    c_i = lax.broadcasted_iota(jnp.int32, (A_CHUNK, A_CHUNK), 1)
    gw = A_WIDTH // A_GROUPS
    w_causal = [jnp.where(c_i <= r_i, w_s_ref[g], 0.0).astype(BF16) for g in range(A_GROUPS)]
    v16 = v.astype(BF16)
    s_rows = []
    for ck in range(rows // A_CHUNK):
        s_cols = [jnp.dot(w_causal[g], v16[ck * A_CHUNK:(ck + 1) * A_CHUNK, g * gw:(g + 1) * gw],
                          preferred_element_type=F32) for g in range(A_GROUPS)]
        s_rows.append(jnp.concatenate(s_cols, axis=1) + bs_ref[...])
    y_a = u * jnp.concatenate(s_rows, axis=0)

    zb = z[:, 2 * A_WIDTH:]
    gate = zb[:, B_WIDTH:]
    a = zb[:, :B_WIDTH] * (1.0 / (1.0 + jnp.exp(-gate)))
    hist_ref[CONV_HALO:CONV_HALO + rows, :] = a
    first = CONV_HALO - (CONV_WIDTH - 1)
    acc = jnp.zeros((rows, B_WIDTH), F32) + b_dw_ref[...]
    for r in range(SUBLANES):
        taps = [k for k in range(CONV_WIDTH) if (first + k) % SUBLANES == r]
        if not taps:
            continue
        span = max(first + k for k in taps) - r + rows
        if r == 0:
            shifted = hist_ref
        else:
            shifted = shift_ref.at[r - 1]
            shifted[0:span, :] = hist_ref[r:r + span, :]
        for k in taps:
            lo = first + k - r
            acc = acc + w_dw_ref[k:k + 1, :] * shifted[lo:lo + rows, :]
    hist_ref[0:CONV_HALO, :] = hist_ref[rows:rows + CONV_HALO, :]
    yb = _layer_norm(acc, cg_ref[...], cb_ref[...])
    y_b = yb * (1.0 / (1.0 + jnp.exp(-yb)))

    y = _bdot(jnp.concatenate([y_a, y_b], axis=1), w_out_ref[...])
    _sublayer_epilogue(x, y, cond_ref, lng_ref, lnb_ref, wr_ref, br_ref, x_out_ref, hx_ref, meta_t_ref)


def _full(shape):
    nd = len(shape)
    return pl.BlockSpec(shape, lambda *_: (0,) * nd)


def _even_call(x2d, cond, bsz, seq, w_in, sgu_g, sgu_b, w_s, bs_full, w_dw, b_dw, cg, cb, w_out,
               lng, lnb, wr, br):
    n = bsz * seq
    n_t = seq // EVEN_ROWS
    row = lambda b, t: (b * n_t + t, 0)
    small = [sgu_g, sgu_b, w_s, bs_full, w_dw, b_dw, cg, cb, w_out, lng, lnb, wr, br]
    return pl.pallas_call(
        _even_kernel,
        grid=(bsz, n_t),
        in_specs=[pl.BlockSpec((EVEN_ROWS, D_MODEL), row),
                  pl.BlockSpec((None, 6, D_MODEL), lambda b, t: (b, 0, 0)),
                  _full(w_in.shape)] + [_full(a.shape) for a in small],
        out_specs=(pl.BlockSpec((EVEN_ROWS, D_MODEL), row),
                   pl.BlockSpec((EVEN_ROWS, HX_W), row),
                   pl.BlockSpec((SUBLANES, EVEN_ROWS), lambda b, t: (0, b * n_t + t))),
        out_shape=_epilogue_out(n),
        scratch_shapes=[pltpu.VMEM((CONV_HALO + EVEN_ROWS, B_WIDTH), F32),
                        pltpu.VMEM((SUBLANES - 1, CONV_HALO + EVEN_ROWS, B_WIDTH), F32)],
        compiler_params=_params("arbitrary", "arbitrary"),
        name="even_mixer",
    )(x2d, cond, w_in, *small)


RANK_SUB = 256
RANK_ROWS = 2048


def _rank_kernel(meta_t_ref, pos_ref, counts_ref, run_ref):
    phase = pl.program_id(0)
    i = pl.program_id(1)
    rows = meta_t_ref.shape[1]

    @pl.when((phase == 0) & (i == 0))
    def _():
        run_ref[...] = jnp.zeros_like(run_ref)

    @pl.when((phase == 1) & (i == 0))
    def _():
        counts = run_ref[...]
        counts_ref[...] = counts
        padded = jnp.floor((counts + (MOE_TILE - 1)) * (1.0 / MOE_TILE)) * MOE_TILE
        sub_l = lax.broadcasted_iota(jnp.int32, counts.shape, 0)
        off = jnp.zeros_like(counts)
        for b in range(N_BUCKETS):
            off = off + jnp.where(sub_l > b, padded[b:b + 1, :], 0.0)
        run_ref[...] = off

    r_i = lax.broadcasted_iota(jnp.int32, (RANK_SUB, RANK_SUB), 0)
    c_i = lax.broadcasted_iota(jnp.int32, (RANK_SUB, RANK_SUB), 1)
    tri = jnp.where(r_i <= c_i, 1.0, 0.0).astype(BF16)
    sub = lax.broadcasted_iota(jnp.int32, (BUCKET_ROWS, RANK_SUB), 0)
    run = run_ref[...]
    for t in range(rows // RANK_SUB):
        bucket = meta_t_ref[0:1, t * RANK_SUB:(t + 1) * RANK_SUB].astype(jnp.int32)
        onehot = jnp.where(sub == bucket, 1.0, 0.0)
        cum = jnp.dot(onehot.astype(BF16), tri, preferred_element_type=F32)
        pos = jnp.sum(onehot * (cum + run[:, 0:1]), axis=0, keepdims=True) - 1.0
        pos_ref[:, t * RANK_SUB:(t + 1) * RANK_SUB] = pos.astype(jnp.int32)
        run = run + cum[:, RANK_SUB - 1:RANK_SUB]
    run_ref[...] = run


def _rank_call(meta_t):
    n = meta_t.shape[1]
    n_t = n // RANK_ROWS
    return pl.pallas_call(
        _rank_kernel,
        grid=(2, n_t),
        in_specs=[pl.BlockSpec((SUBLANES, RANK_ROWS), lambda p, i: (0, i))],
        out_specs=(pl.BlockSpec((1, RANK_ROWS), lambda p, i: (0, i * p)),
                   pl.BlockSpec((BUCKET_ROWS, LANES), lambda p, i: (0, 0))),
        out_shape=(jax.ShapeDtypeStruct((1, n), jnp.int32),
                   jax.ShapeDtypeStruct((BUCKET_ROWS, LANES), F32)),
        scratch_shapes=[pltpu.VMEM((BUCKET_ROWS, LANES), F32)],
        compiler_params=_params("arbitrary", "arbitrary"),
        name="moe_rank",
    )(meta_t)


def _gathered_rows(idx_ref, src_ref, buf, sems, i, n_steps):
    slots, rows = buf.shape[0], buf.shape[1]
    ahead = slots - 1

    def issue(step, slot):
        for r in range(rows):
            pltpu.make_async_copy(src_ref.at[pl.ds(idx_ref[step * rows + r], 1), :],
                                  buf.at[slot, pl.ds(r, 1), :], sems.at[slot]).start()

    def wait(slot):
        pltpu.make_async_copy(src_ref.at[pl.ds(0, rows), :], buf.at[slot], sems.at[slot]).wait()

    @pl.when(i == 0)
    def _():
        for a in range(ahead):
            issue(jnp.minimum(a, n_steps - 1), a)

    def wait_current():
        slot = lax.rem(i, slots)
        wait(slot)
        return buf[slot]

    def issue_next():
        issue(jnp.minimum(i + ahead, n_steps - 1), lax.rem(i + ahead, slots))

    def drain():
        @pl.when(i == n_steps - 1)
        def _():
            for a in range(1, slots):
                wait(lax.rem(i + a, slots))

    return wait_current, issue_next, drain


DISPATCH_ROWS = 2048


def _dispatch_kernel(pos_ref, pad_start_ref, pad_count_ref, nact_ref, hx_ref, xs_ref, zero_ref, sem, zsem):
    rows = hx_ref.shape[0]
    step = pl.program_id(0)
    base = step * rows
    n_tiles = xs_ref.shape[0] // MOE_TILE

    @pl.when(step == 0)
    def _():
        zero_ref[...] = jnp.zeros_like(zero_ref)

        def tail_tile(t):
            start = pl.multiple_of(t * MOE_TILE, MOE_TILE)
            return pltpu.make_async_copy(zero_ref, xs_ref.at[pl.ds(start, MOE_TILE), :], zsem)

        def for_all(act):
            for b in range(N_BUCKETS):
                count = pad_count_ref[b]
                start = pad_start_ref[b]
                done = 0
                size = MOE_TILE // 2
                while size >= SUBLANES:
                    @pl.when((count & size) != 0)
                    def _(done=done, size=size):
                        off = pl.multiple_of(start + count - done - size, SUBLANES)
                        act(pltpu.make_async_copy(zero_ref.at[pl.ds(0, size), :],
                                                  xs_ref.at[pl.ds(off, size), :], zsem))
                    done = done + (count & size)
                    size //= 2
                for r in range(SUBLANES - 1):
                    @pl.when(r < (count & (SUBLANES - 1)))
                    def _(r=r):
                        act(pltpu.make_async_copy(zero_ref.at[pl.ds(0, 1), :],
                                                  xs_ref.at[pl.ds(start + r, 1), :], zsem))
            lax.fori_loop(nact_ref[0], n_tiles, lambda t, c: (act(tail_tile(t)), c)[1], 0)

        for_all(lambda cp: cp.start())
        for_all(lambda cp: cp.wait())

    for r in range(rows):
        pltpu.make_async_copy(hx_ref.at[pl.ds(r, 1), :], xs_ref.at[pl.ds(pos_ref[base + r], 1), :],
                              sem).start()
    pltpu.make_async_copy(hx_ref, xs_ref.at[pl.ds(0, rows), :], sem).wait()


def _dispatch_call(pos, pad_start, pad_count, nact, hx, n_sorted):
    n = hx.shape[0]
    return pl.pallas_call(
        _dispatch_kernel,
        grid_spec=pltpu.PrefetchScalarGridSpec(
            num_scalar_prefetch=4,
            grid=(n // DISPATCH_ROWS,),
            in_specs=[pl.BlockSpec((DISPATCH_ROWS, HX_W), lambda i, *_: (i, 0))],
            out_specs=pl.BlockSpec(memory_space=pl.ANY),
            scratch_shapes=[pltpu.VMEM((MOE_TILE, HX_W), F32),
                            pltpu.SemaphoreType.DMA(()), pltpu.SemaphoreType.DMA(())]),
        out_shape=jax.ShapeDtypeStruct((n_sorted, HX_W), F32),
        compiler_params=_params("arbitrary"),
        name="moe_dispatch",
    )(pos, pad_start, pad_count, nact, hx)


def _moe_kernel(lo_ref, hi_ref, nact_ref, xs_ref, wg_lo_ref, wu_lo_ref, wd_lo_ref,
                wg_hi_ref, wu_hi_ref, wd_hi_ref, ys_ref):
    del lo_ref, hi_ref

    @pl.when(pl.program_id(0) < nact_ref[0])
    def _():
        x = xs_ref[:, :D_MODEL].astype(BF16)
        w_lo = xs_ref[:, D_MODEL + 1:D_MODEL + 2]
        w_hi = xs_ref[:, D_MODEL + 2:D_MODEL + 3]

        def expert(wg_ref, wu_ref, wd_ref):
            gate = jnp.dot(x, wg_ref[...].astype(BF16), preferred_element_type=F32)
            up = jnp.dot(x, wu_ref[...].astype(BF16), preferred_element_type=F32)
            hid = gate * (1.0 / (1.0 + jnp.exp(-gate))) * up
            return jnp.dot(hid.astype(BF16), wd_ref[...].astype(BF16), preferred_element_type=F32)

        ys_ref[...] = (w_lo * expert(wg_lo_ref, wu_lo_ref, wd_lo_ref)
                       + w_hi * expert(wg_hi_ref, wu_hi_ref, wd_hi_ref))

    @pl.when(pl.program_id(0) >= nact_ref[0])
    def _():
        ys_ref[...] = jnp.zeros_like(ys_ref)


def _moe_call(tile_lo, tile_hi, nact, xs, layer, w_gate, w_up, w_down):
    n_sorted = xs.shape[0]
    n_tiles = n_sorted // MOE_TILE

    def tile(t, lo, hi, na):
        return (jnp.minimum(t, na[0] - 1), 0)

    def w_lo(t, lo, hi, na):
        return (layer, lo[jnp.minimum(t, na[0] - 1)], 0, 0)

    def w_hi(t, lo, hi, na):
        return (layer, hi[jnp.minimum(t, na[0] - 1)], 0, 0)

    up_spec = lambda index: pl.BlockSpec((None, None, D_MODEL, D_EXPERT), index)
    down_spec = lambda index: pl.BlockSpec((None, None, D_EXPERT, D_MODEL), index)
    return pl.pallas_call(
        _moe_kernel,
        grid_spec=pltpu.PrefetchScalarGridSpec(
            num_scalar_prefetch=3,
            grid=(n_tiles,),
            in_specs=[pl.BlockSpec((MOE_TILE, HX_W), tile),
                      up_spec(w_lo), up_spec(w_lo), down_spec(w_lo),
                      up_spec(w_hi), up_spec(w_hi), down_spec(w_hi)],
            out_specs=pl.BlockSpec((MOE_TILE, D_MODEL), lambda t, lo, hi, na: (t, 0))),
        out_shape=jax.ShapeDtypeStruct((n_sorted, D_MODEL), F32),
        compiler_params=_params("arbitrary"),
        name="moe_experts",
    )(tile_lo, tile_hi, nact, xs, w_gate, w_up, w_down, w_gate, w_up, w_down)


def _combine_kernel(pos_ref, x_ref, cond_ref, lng_ref, lnb_ref, ys_ref, o_ref, ybuf, sems):
    wait_current, issue_next, drain = _gathered_rows(pos_ref, ys_ref, ybuf, sems, pl.program_id(0),
                                                     pl.num_programs(0))
    g_f = cond_ref[5:6, :]
    pre = DN_ALPHA * x_ref[...] + (1.0 + g_f) * wait_current()
    issue_next()
    o_ref[...] = _layer_norm(pre, lng_ref[...], lnb_ref[...])
    drain()


def _combine_call(pos, x1, cond, seq, lng, lnb, ys):
    n = x1.shape[0]
    n_t = seq // ROW_TILE
    return pl.pallas_call(
        _combine_kernel,
        grid_spec=pltpu.PrefetchScalarGridSpec(
            num_scalar_prefetch=1,
            grid=(n // ROW_TILE,),
            in_specs=[pl.BlockSpec((ROW_TILE, D_MODEL), lambda i, p: (i, 0)),
                      pl.BlockSpec((None, 6, D_MODEL), lambda i, p: (i // n_t, 0, 0)),
                      pl.BlockSpec((1, D_MODEL), lambda i, p: (0, 0)),
                      pl.BlockSpec((1, D_MODEL), lambda i, p: (0, 0)),
                      pl.BlockSpec(memory_space=pl.ANY)],
            out_specs=pl.BlockSpec((ROW_TILE, D_MODEL), lambda i, p: (i, 0)),
            scratch_shapes=[pltpu.VMEM((GATHER_AHEAD + 1, ROW_TILE, D_MODEL), F32),
                            pltpu.SemaphoreType.DMA((GATHER_AHEAD + 1,))]),
        out_shape=jax.ShapeDtypeStruct((n, D_MODEL), F32),
        compiler_params=_params("arbitrary"),
        name="moe_combine",
    )(pos, x1, cond, lng, lnb, ys)


def _moe_sublayer(hx, meta_t, layer, w_gate, w_up, w_down):
    n = hx.shape[0]
    n_sorted = n + N_BUCKETS * MOE_TILE
    pos2d, counts = _rank_call(meta_t)
    pos = pos2d.reshape(n)
    cnt = counts[:N_BUCKETS, 0].astype(jnp.int32)
    padded = ((cnt + MOE_TILE - 1) // MOE_TILE) * MOE_TILE
    ends = jnp.cumsum(padded)
    starts = jnp.arange(n_sorted // MOE_TILE, dtype=jnp.int32) * MOE_TILE
    tile_bucket = jnp.minimum(jnp.sum((ends[None, :] <= starts[:, None]).astype(jnp.int32), axis=1),
                              N_BUCKETS - 1)
    tile_lo = jnp.asarray(_BUCKET_LO)[tile_bucket]
    tile_hi = jnp.asarray(_BUCKET_HI)[tile_bucket]
    nact = (ends[-1:] // MOE_TILE).astype(jnp.int32)
    xs = _dispatch_call(pos, ends - padded + cnt, padded - cnt, nact, hx, n_sorted)
    ys = _moe_call(tile_lo, tile_hi, nact, xs, layer, w_gate, w_up, w_down)
    return pos, ys


def _qkv_kernel(pos_ref, x1_ref, cond_prev_ref, cond_ref, lng_ref, lnb_ref, ys_ref, wq_t_ref, wk_ref, wv_t_ref,
                x_out_ref, q_t_ref, k_ref, v_t_ref, kmean_ref, ybuf, sems):
    wait_current, issue_next, drain = _gathered_rows(pos_ref, ys_ref, ybuf, sems, pl.program_id(0),
                                                     pl.num_programs(0))
    g_f = cond_prev_ref[5:6, :]
    pre = DN_ALPHA * x1_ref[...] + (1.0 + g_f) * wait_current()
    issue_next()
    x = _layer_norm(pre, lng_ref[...], lnb_ref[...])
    x_out_ref[...] = x
    h = (x * (1.0 + cond_ref[1:2, :]) + cond_ref[0:1, :]).astype(BF16)
    nt = (((1,), (1,)), ((), ()))
    q_t_ref[...] = lax.dot_general(wq_t_ref[...], h, nt, preferred_element_type=F32).astype(BF16)
    v_t_ref[...] = lax.dot_general(wv_t_ref[...], h, nt, preferred_element_type=F32).astype(BF16)
    k = jnp.dot(h, wk_ref[...], preferred_element_type=F32)
    k_ref[...] = k.astype(BF16)
    kmean_ref[...] = jnp.broadcast_to(jnp.mean(k, axis=0, keepdims=True), kmean_ref.shape)
    drain()


def _qkv_call(pos, x1, cond_prev, cond, lng, lnb, ys, bsz, seq, wq_t, wk, wv_t):
    n_t = seq // MOBA_BLOCK
    hd = N_HEADS * HEAD_DIM
    n = bsz * seq
    per_batch = lambda i, p: (i // n_t, 0, 0)
    return pl.pallas_call(
        _qkv_kernel,
        grid_spec=pltpu.PrefetchScalarGridSpec(
            num_scalar_prefetch=1,
            grid=(n // MOBA_BLOCK,),
            in_specs=[pl.BlockSpec((MOBA_BLOCK, D_MODEL), lambda i, p: (i, 0)),
                      pl.BlockSpec((None, 6, D_MODEL), per_batch),
                      pl.BlockSpec((None, 6, D_MODEL), per_batch),
                      pl.BlockSpec((1, D_MODEL), lambda i, p: (0, 0)),
                      pl.BlockSpec((1, D_MODEL), lambda i, p: (0, 0)),
                      pl.BlockSpec(memory_space=pl.ANY),
                      _full(wq_t.shape), _full(wk.shape), _full(wv_t.shape)],
            out_specs=(pl.BlockSpec((MOBA_BLOCK, D_MODEL), lambda i, p: (i, 0)),
                       pl.BlockSpec((None, hd, MOBA_BLOCK), lambda i, p: (i // n_t, 0, i % n_t)),
                       pl.BlockSpec((None, MOBA_BLOCK, hd), lambda i, p: (i // n_t, i % n_t, 0)),
                       pl.BlockSpec((None, hd, MOBA_BLOCK), lambda i, p: (i // n_t, 0, i % n_t)),
                       pl.BlockSpec((None, None, SUBLANES, hd), lambda i, p: (i // n_t, i % n_t, 0, 0))),
            scratch_shapes=[pltpu.VMEM((GATHER_AHEAD + 1, MOBA_BLOCK, D_MODEL), F32),
                            pltpu.SemaphoreType.DMA((GATHER_AHEAD + 1,))]),
        out_shape=(jax.ShapeDtypeStruct((n, D_MODEL), F32),
                   jax.ShapeDtypeStruct((bsz, hd, seq), BF16),
                   jax.ShapeDtypeStruct((bsz, seq, hd), BF16),
                   jax.ShapeDtypeStruct((bsz, hd, seq), BF16),
                   jax.ShapeDtypeStruct((bsz, n_t, SUBLANES, hd), F32)),
        compiler_params=_params("arbitrary"),
        name="qkv_proj",
    )(pos, x1, cond_prev, cond, lng, lnb, ys, wq_t, wk, wv_t)


def _moba_kernel(q_t_ref, k_ref, v_t_ref, kmean_ref, o_ref):
    n_blocks = kmean_ref.shape[0]
    blk = MOBA_BLOCK
    c = HEAD_DIM ** -0.5 * LOG2E
    km = kmean_ref[...]
    km_hi = km.astype(BF16)
    km_lo = (km - km_hi.astype(F32)).astype(BF16)
    causal = (lax.broadcasted_iota(jnp.int32, (blk, blk), 0)
              <= lax.broadcasted_iota(jnp.int32, (blk, blk), 1))
    bidx = lax.broadcasted_iota(jnp.int32, (n_blocks, blk), 0)

    def scores(i):
        q_t = q_t_ref[:, i * blk:(i + 1) * blk]
        return q_t, [jnp.dot(k_ref[j * blk:(j + 1) * blk, :], q_t, preferred_element_type=F32)
                     for j in range(i + 1)]

    def attend(i, q_t, s):
        s_own = jnp.where(causal, s[i], NEG_BIG)
        m = jnp.max(s_own, axis=0, keepdims=True)
        sel = []
        if i > 0:
            gate = (jnp.dot(km_hi, q_t, preferred_element_type=F32)
                    + jnp.dot(km_lo, q_t, preferred_element_type=F32))
            gate = jnp.where(bidx < i, gate, -jnp.inf)
            rank = jnp.zeros(gate.shape, F32)
            for j in range(i):
                g_j = gate[j:j + 1, :]
                beats = (g_j > gate) | ((g_j == gate) & (j < bidx))
                rank = rank + jnp.where(beats, 1.0, 0.0)
            chosen = jnp.where((bidx < i) & (rank < MOBA_TOPK), 1.0, 0.0)
            for j in range(i):
                sel_j = chosen[j:j + 1, :] > 0.5
                sel.append(sel_j)
                bm = jnp.max(s[j], axis=0, keepdims=True)
                m = jnp.maximum(m, jnp.where(sel_j, bm, NEG_BIG))
        p_own = jnp.exp2((s_own - m) * c)
        l = jnp.sum(p_own, axis=0, keepdims=True)
        acc = jnp.dot(v_t_ref[:, i * blk:(i + 1) * blk], p_own.astype(BF16), preferred_element_type=F32)
        for j in range(i):
            shift = jnp.where(sel[j], m, -NEG_BIG)
            p_j = jnp.exp2((s[j] - shift) * c)
            l = l + jnp.sum(p_j, axis=0, keepdims=True)
            acc = acc + jnp.dot(v_t_ref[:, j * blk:(j + 1) * blk], p_j.astype(BF16),
                                preferred_element_type=F32)
        o_ref[i * blk:(i + 1) * blk, :] = jnp.transpose(acc * (1.0 / l)).astype(BF16)

    nxt = scores(0)
    for i in range(n_blocks):
        cur = nxt
        if i + 1 < n_blocks:
            nxt = scores(i + 1)
        attend(i, *cur)


def _moba_call(q_t, k, v_t, kmean):
    bsz, hd, seq = q_t.shape
    n_b = seq // MOBA_BLOCK
    return pl.pallas_call(
        _moba_kernel,
        grid=(bsz, N_HEADS),
        in_specs=[pl.BlockSpec((None, HEAD_DIM, seq), lambda b, h: (b, h, 0)),
                  pl.BlockSpec((None, seq, HEAD_DIM), lambda b, h: (b, 0, h)),
                  pl.BlockSpec((None, HEAD_DIM, seq), lambda b, h: (b, h, 0)),
                  pl.BlockSpec((None, n_b, HEAD_DIM), lambda b, h: (b, 0, h))],
        out_specs=pl.BlockSpec((None, seq, HEAD_DIM), lambda b, h: (b, 0, h)),
        out_shape=jax.ShapeDtypeStruct((bsz, seq, hd), BF16),
        compiler_params=_params("arbitrary", "arbitrary"),
        name="moba_attention",
    )(q_t, k, v_t, kmean)


def _attn_out_kernel(x_ref, o_ref, cond_ref, w_out_ref, lng_ref, lnb_ref, wr_ref, br_ref,
                     x_out_ref, hx_ref, meta_t_ref):
    y = jnp.dot(o_ref[...], w_out_ref[...], preferred_element_type=F32)
    _sublayer_epilogue(x_ref[...], y, cond_ref, lng_ref, lnb_ref, wr_ref, br_ref, x_out_ref, hx_ref, meta_t_ref)


def _attn_out_call(x2d, o2d, cond, seq, w_out, lng, lnb, wr, br):
    n = x2d.shape[0]
    n_t = seq // ROW_TILE
    row = lambda i: (i, 0)
    small = [w_out, lng, lnb, wr, br]
    return pl.pallas_call(
        _attn_out_kernel,
        grid=(n // ROW_TILE,),
        in_specs=[pl.BlockSpec((ROW_TILE, D_MODEL), row),
                  pl.BlockSpec((ROW_TILE, D_MODEL), row),
                  pl.BlockSpec((None, 6, D_MODEL), lambda i: (i // n_t, 0, 0))]
                 + [_full(a.shape) for a in small],
        out_specs=(pl.BlockSpec((ROW_TILE, D_MODEL), row),
                   pl.BlockSpec((ROW_TILE, HX_W), row),
                   pl.BlockSpec((SUBLANES, ROW_TILE), lambda i: (0, i))),
        out_shape=_epilogue_out(n),
        compiler_params=_params("arbitrary"),
        name="attn_out",
    )(x2d, o2d, cond, *small)


class _PendingCombine(NamedTuple):
    x1: jax.Array
    pos: jax.Array
    ys: jax.Array
    cond: jax.Array
    lng: jax.Array
    lnb: jax.Array


def _router_weights(w_grp, b_grp, w_er, b_er):
    d = w_grp.shape[0]
    w_exp = jnp.transpose(w_er, (1, 0, 2)).reshape(d, N_EXPERTS)
    pad = LANES - N_GROUPS - N_EXPERTS
    w_cat = jnp.concatenate([w_grp, w_exp, jnp.zeros((d, ROUTER_LO_LANE - N_GROUPS - N_EXPERTS), F32)], axis=1)
    w_hi = w_cat.astype(BF16)
    w_lo = (w_cat - w_hi.astype(F32)).astype(BF16)
    wr = jnp.concatenate([w_hi, w_lo, jnp.zeros((d, LANES - 2 * ROUTER_LO_LANE), BF16)], axis=1)
    br = jnp.concatenate([b_grp, b_er.reshape(N_EXPERTS), jnp.zeros((pad,), F32)]).reshape(1, LANES)
    return wr, br


def kernel(x, c, ada_w, ada_b, ln_mix_g, ln_mix_b, ln_ffn_g, ln_ffn_b, ev_w_in, ev_sgu_ln_g, ev_sgu_ln_b,
           ev_w_s, ev_b_s, ev_w_dw, ev_b_dw, ev_conv_ln_g, ev_conv_ln_b, ev_w_out, od_w_qkv, od_w_out,
           moe_w_grp, moe_b_grp, moe_w_er, moe_b_er, moe_w_gate, moe_w_up, moe_w_down):
    bsz, seq, d = x.shape
    n = bsz * seq
    depth = ada_w.shape[0]
    assert d == D_MODEL and depth == DEPTH
    assert seq % EVEN_ROWS == 0 and seq % MOBA_BLOCK == 0 and seq % ROW_TILE == 0
    assert n % DISPATCH_ROWS == 0 and n % RANK_ROWS == 0
    cond_all = _cond_call(c, ada_w, ada_b).reshape(depth, bsz, 6, d)
    row = lambda v: v.reshape(1, -1)
    xc = x.reshape(n, d)
    pending = None
    for i in range(depth):
        cond = cond_all[i]
        j = i // 2
        wr, br = _router_weights(moe_w_grp[i], moe_b_grp[i], moe_w_er[i], moe_b_er[i])
        lng, lnb = row(ln_mix_g[i]), row(ln_mix_b[i])
        if i % 2 == 0:
            if pending is not None:
                xc = _combine_call(pending.pos, pending.x1, pending.cond, seq, pending.lng, pending.lnb,
                                   pending.ys)
            gw = A_WIDTH // A_GROUPS
            bs_full = jnp.repeat(ev_b_s[j].T, gw, axis=1)
            w_dw = jnp.concatenate([ev_w_dw[j], jnp.zeros((1, B_WIDTH), F32)], axis=0)
            x1, hx, meta_t = _even_call(
                xc, cond, bsz, seq, ev_w_in[j].astype(BF16), row(ev_sgu_ln_g[j]), row(ev_sgu_ln_b[j]),
                ev_w_s[j], bs_full, w_dw, row(ev_b_dw[j]), row(ev_conv_ln_g[j]), row(ev_conv_ln_b[j]),
                ev_w_out[j].astype(BF16), lng, lnb, wr, br)
        else:
            hd = N_HEADS * HEAD_DIM
            w_qkv = od_w_qkv[j].astype(BF16)
            assert pending is not None
            xc, q_t, k, v_t, kmean8 = _qkv_call(
                pending.pos, pending.x1, pending.cond, cond, pending.lng, pending.lnb, pending.ys, bsz, seq,
                w_qkv[:, :hd].T, w_qkv[:, hd:2 * hd], w_qkv[:, 2 * hd:].T)
            o = _moba_call(q_t, k, v_t, kmean8[:, :, 0, :])
            x1, hx, meta_t = _attn_out_call(xc, o.reshape(n, hd), cond, seq, od_w_out[j].astype(BF16),
                                            lng, lnb, wr, br)
        pos, ys = _moe_sublayer(hx, meta_t, i, moe_w_gate, moe_w_up, moe_w_down)
        pending = _PendingCombine(x1, pos, ys, cond, row(ln_ffn_g[i]), row(ln_ffn_b[i]))
    xc = _combine_call(pending.pos, pending.x1, pending.cond, seq, pending.lng, pending.lnb, pending.ys)
    return xc.reshape(bsz, seq, d)
```
